```python
import jax
import jax.numpy as jnp
from jax import lax
import numpy as np

D_MODEL = 2048
BATCH = 32
SEQ = 256
DEPTH = 2
DEC_BATCH = 2
DEC_SEQ = 1024
PAST_LEN = 256

GRID_W = 64
N_HEADS = 16
N_KV_HEADS = 2
HEAD_DIM = 64
GQA_GROUP = N_HEADS // N_KV_HEADS
ATTN_WIDTH = N_HEADS * HEAD_DIM
KV_WIDTH = N_KV_HEADS * HEAD_DIM
WINDOW = 128
BLOCK = 128
ROPE_THETA = 10000.0
FOURIER_WIDTH = D_MODEL - ATTN_WIDTH
N_FOURIER_GROUPS = 4
FOURIER_GROUP = FOURIER_WIDTH // N_FOURIER_GROUPS
EVEN_IN_WIDTH = ATTN_WIDTH + 2 * KV_WIDTH + FOURIER_WIDTH
EVEN_OUT_WIDTH = ATTN_WIDTH + FOURIER_WIDTH
D_RNN = D_MODEL
N_RNN_BLOCKS = 8
RNN_BLOCK = D_RNN // N_RNN_BLOCKS
CONV_WIDTH = 4
CONV_LEFT = 2
RG_C = 8.0
N_EXPERTS = 32
TOP_K = 4
D_EXPERT = D_MODEL
SWIGLU_LIMIT = 7.0
SWIGLU_ALPHA = 1.702
N_ATTN_LAYERS = (DEPTH + 1) // 2
N_RNN_LAYERS = DEPTH // 2
N_MOD = 6
EPS = 1e-6

kernel_name = "hybrid_flow_prefix_trunk_step"


def _rmsnorm(x, g):
    x32 = x.astype(jnp.float32)
    y = x32 * lax.rsqrt(jnp.mean(x32 * x32, axis=-1, keepdims=True) + EPS)
    return (y * g.astype(jnp.float32)).astype(x.dtype)


def _modulation(cond, w_mod, b_mod):
    m = jax.nn.silu(cond) @ w_mod + b_mod
    return jnp.split(m[:, None, :], N_MOD, axis=-1)


def _rope_2d(x):
    T = x.shape[1]
    rows = T // GRID_W
    row = jnp.repeat(jnp.arange(rows, dtype=jnp.float32), GRID_W)
    col = jnp.tile(jnp.arange(GRID_W, dtype=jnp.float32), rows)
    n_freq = HEAD_DIM // 4
    inv = ROPE_THETA ** (-jnp.arange(n_freq, dtype=jnp.float32) / n_freq)
    ang = jnp.concatenate([row[:, None] * inv, col[:, None] * inv], axis=-1)
    cos = jnp.cos(ang)[None, :, None, :]
    sin = jnp.sin(ang)[None, :, None, :]
    x32 = x.astype(jnp.float32)
    x1 = x32[..., 0::2]
    x2 = x32[..., 1::2]
    y = jnp.stack([x1 * cos - x2 * sin, x1 * sin + x2 * cos], axis=-1).reshape(x.shape)
    return y.astype(x.dtype)


def _attend(q, k, v, valid, sink):
    s = jnp.einsum('bqkgd,bskd->bkgqs', q, k, preferred_element_type=jnp.float32) * (HEAD_DIM ** -0.5)
    if valid is not None:
        s = jnp.where(valid, s, -jnp.inf)
    sk = sink.astype(jnp.float32).reshape(1, N_KV_HEADS, GQA_GROUP, 1, 1)
    m = jnp.maximum(jnp.max(s, axis=-1, keepdims=True), sk)
    p = jnp.exp(s - m)
    den = jnp.sum(p, axis=-1, keepdims=True) + jnp.exp(sk - m)
    return jnp.einsum('bkgqs,bskd->bqkgd', (p / den).astype(v.dtype), v)


def _context_attention(q, k, v, sink):
    B, L = q.shape[0], q.shape[1]
    nb = L // BLOCK
    qb = jnp.moveaxis(q.reshape(B, nb, BLOCK, N_KV_HEADS, GQA_GROUP, HEAD_DIM), 1, 0)
    o = lax.map(lambda qblk: _attend(qblk, k, v, None, sink), qb)
    return jnp.moveaxis(o, 0, 1).reshape(B, L, ATTN_WIDTH)


def _latent_attention(q, k, v, ck, cv, sink):
    B, T = q.shape[0], q.shape[1]
    nb = T // BLOCK
    Lc = ck.shape[1]

    def band(x):
        xp = jnp.pad(x, ((0, 0), (BLOCK, BLOCK), (0, 0), (0, 0))).reshape(B, nb + 2, BLOCK, N_KV_HEADS, HEAD_DIM)
        return jnp.concatenate([xp[:, :nb], xp[:, 1:nb + 1], xp[:, 2:]], axis=2)

    kb, vb = band(k), band(v)
    qi = jnp.arange(BLOCK)[:, None]
    kj = jnp.arange(3 * BLOCK)[None, :]
    kpos = (jnp.arange(nb)[:, None, None] - 1) * BLOCK + kj
    valid_band = (jnp.abs(kj - BLOCK - qi) <= WINDOW) & (kpos >= 0) & (kpos < T)
    valid = jnp.concatenate([jnp.ones((nb, BLOCK, Lc), dtype=bool), valid_band], axis=-1)
    qb = jnp.moveaxis(q.reshape(B, nb, BLOCK, N_KV_HEADS, GQA_GROUP, HEAD_DIM), 1, 0)

    def one(args):
        qblk, kblk, vblk, vm = args
        kk = jnp.concatenate([ck, kblk], axis=1)
        vv = jnp.concatenate([cv, vblk], axis=1)
        return _attend(qblk, kk, vv, vm, sink)

    o = lax.map(one, (qb, jnp.moveaxis(kb, 1, 0), jnp.moveaxis(vb, 1, 0), valid))
    return jnp.moveaxis(o, 0, 1).reshape(B, T, ATTN_WIDTH)


def _fourier_mix(f):
    B, T = f.shape[0], f.shape[1]
    u = f.astype(jnp.float32).reshape(B, T, N_FOURIER_GROUPS, FOURIER_GROUP)
    z = jnp.fft.fft2(u, axes=(1, 3), norm="ortho").real
    return z.reshape(B, T, FOURIER_WIDTH).astype(f.dtype)


def _even_project(h, w_in, b_in):
    B, T = h.shape[0], h.shape[1]
    p = h @ w_in + b_in
    q = p[..., :ATTN_WIDTH].reshape(B, T, N_HEADS, HEAD_DIM)
    k = p[..., ATTN_WIDTH:ATTN_WIDTH + KV_WIDTH].reshape(B, T, N_KV_HEADS, HEAD_DIM)
    v = p[..., ATTN_WIDTH + KV_WIDTH:ATTN_WIDTH + 2 * KV_WIDTH].reshape(B, T, N_KV_HEADS, HEAD_DIM)
    f = p[..., ATTN_WIDTH + 2 * KV_WIDTH:]
    return q, k, v, f


def _even_mixer_context(h, w_in, b_in, w_out, b_out, sink):
    q, k, v, f = _even_project(h, w_in, b_in)
    a = _context_attention(q, k, v, sink)
    y = jnp.concatenate([a, _fourier_mix(f)], axis=-1) @ w_out + b_out
    return y, k, v


def _even_mixer_latent(h, ck, cv, w_in, b_in, w_out, b_out, sink):
    q, k, v, f = _even_project(h, w_in, b_in)
    a = _latent_attention(_rope_2d(q), _rope_2d(k), v, ck, cv, sink)
    return jnp.concatenate([a, _fourier_mix(f)], axis=-1) @ w_out + b_out


def _conv_centred(x, w, b):
    T = x.shape[1]
    xp = jnp.pad(x, ((0, 0), (CONV_LEFT, CONV_WIDTH - 1 - CONV_LEFT), (0, 0)))
    y = b
    for tap in range(CONV_WIDTH):
        y = y + xp[:, tap:tap + T] * w[tap]
    return y


def _lin_combine(l, r):
    return (l[0] * r[0], r[0] * l[1] + r[1])


def _rglru(x, w_a, b_a, w_x, b_x, lam, h0, reverse):
    B, T = x.shape[0], x.shape[1]
    xb = x.reshape(B, T, N_RNN_BLOCKS, RNN_BLOCK)
    r = jax.nn.sigmoid((jnp.einsum('btnc,ncd->btnd', xb, w_a).reshape(B, T, D_RNN) + b_a).astype(jnp.float32))
    i = jax.nn.sigmoid((jnp.einsum('btnc,ncd->btnd', xb, w_x).reshape(B, T, D_RNN) + b_x).astype(jnp.float32))
    log_a = -RG_C * r * jax.nn.softplus(-lam.astype(jnp.float32))
    a = jnp.exp(log_a)
    b = jnp.sqrt(-jnp.expm1(2.0 * log_a)) * (i * x.astype(jnp.float32))
    first = T - 1 if reverse else 0
    b = b.at[:, first].add(a[:, first] * h0.astype(jnp.float32))
    _, h = lax.associative_scan(_lin_combine, (a, b), axis=1, reverse=reverse)
    return h


def _rnn_mixer(h, h0, w_in, b_in, conv_w, conv_b, w_a, b_a, w_x, b_x, lam, w_out, b_out, return_state):
    p = h @ w_in + b_in
    xr, gr = jnp.split(p, 2, axis=-1)
    xc = _conv_centred(xr, conv_w, conv_b)
    hf = _rglru(xc, w_a[0], b_a[0], w_x[0], b_x[0], lam[0], h0[:, 0], False)
    hb = _rglru(xc, w_a[1], b_a[1], w_x[1], b_x[1], lam[1], h0[:, 1], True)
    y = (jax.nn.gelu(gr.astype(jnp.float32)) * (hf + hb)).astype(h.dtype) @ w_out + b_out
    if return_state:
        return y, jnp.stack([hf[:, -1], hb[:, 0]], axis=1)
    return y


def _moe(h, w_router, b_router, w_gate, b_gate, w_up, b_up, w_down, b_down):
    B, T, D = h.shape
    x = h.reshape(B * T, D)
    logits = (x @ w_router + b_router).astype(jnp.float32)
    top_v, top_i = lax.top_k(logits, TOP_K)
    gates = jax.nn.softmax(top_v, axis=-1)
    combine = jnp.sum(jax.nn.one_hot(top_i, N_EXPERTS, dtype=jnp.float32) * gates[..., None], axis=1)

    def expert_step(acc, prm):
        wg, bg, wu, bu, wd, bd, cw = prm
        g = jnp.minimum(x @ wg + bg, SWIGLU_LIMIT)
        u = jnp.clip(x @ wu + bu, -SWIGLU_LIMIT, SWIGLU_LIMIT)
        o = ((u + 1.0) * (g * jax.nn.sigmoid(SWIGLU_ALPHA * g))) @ wd + bd
        return (acc + cw[:, None].astype(o.dtype) * o).astype(acc.dtype), None

    acc0 = jnp.zeros((B * T, D), h.dtype)
    out, _ = lax.scan(expert_step, acc0, (w_gate, b_gate, w_up, b_up, w_down, b_down, combine.T))
    return out.reshape(B, T, D)


def setup_inputs(seed: int = 0) -> dict:
    key = jax.random.key(seed)
    ks = iter(jax.random.split(key, 48))
    f32 = jnp.float32
    D = D_MODEL

    def nrm(shape, scale):
        return jax.random.normal(next(ks), shape, f32) * scale

    def gain(shape):
        return 1.0 + nrm(shape, 0.05)

    u = jax.random.uniform(next(ks), (N_RNN_LAYERS, 2, D_RNN), f32, 0.9, 0.999)
    return {
        "x_prompt": nrm((BATCH, SEQ, D), 1.0),
        "x_sample": nrm((DEC_BATCH, DEC_SEQ, D), 1.0),
        "cache_k": nrm((DEC_BATCH, N_ATTN_LAYERS, PAST_LEN, N_KV_HEADS, HEAD_DIM), 1.0),
        "cache_v": nrm((DEC_BATCH, N_ATTN_LAYERS, PAST_LEN, N_KV_HEADS, HEAD_DIM), 1.0),
        "state_rglru": nrm((DEC_BATCH, N_RNN_LAYERS, 2, D_RNN), 0.5),
        "c": nrm((DEC_BATCH, D), 1.0),
        "c_ctx": nrm((D,), 1.0),
        "norm_mix": gain((DEPTH, D)),
        "norm_ffn": gain((DEPTH, D)),
        "w_mod": nrm((DEPTH, D, N_MOD * D), 0.5 * D ** -0.5),
        "b_mod": nrm((DEPTH, N_MOD * D), 0.02),
        "attn_w_in": nrm((N_ATTN_LAYERS, D, EVEN_IN_WIDTH), D ** -0.5),
        "attn_b_in": nrm((N_ATTN_LAYERS, EVEN_IN_WIDTH), 0.02),
        "attn_w_out": nrm((N_ATTN_LAYERS, EVEN_OUT_WIDTH, D), EVEN_OUT_WIDTH ** -0.5),
        "attn_b_out": nrm((N_ATTN_LAYERS, D), 0.02),
        "attn_sink": nrm((N_ATTN_LAYERS, N_HEADS), 0.5),
        "rnn_w_in": nrm((N_RNN_LAYERS, D, 2 * D_RNN), D ** -0.5),
        "rnn_b_in": nrm((N_RNN_LAYERS, 2 * D_RNN), 0.02),
        "rnn_conv_w": nrm((N_RNN_LAYERS, CONV_WIDTH, D_RNN), CONV_WIDTH ** -0.5),
        "rnn_conv_b": nrm((N_RNN_LAYERS, D_RNN), 0.02),
        "rnn_w_a": nrm((N_RNN_LAYERS, 2, N_RNN_BLOCKS, RNN_BLOCK, RNN_BLOCK), RNN_BLOCK ** -0.5),
        "rnn_b_a": nrm((N_RNN_LAYERS, 2, D_RNN), 0.02),
        "rnn_w_x": nrm((N_RNN_LAYERS, 2, N_RNN_BLOCKS, RNN_BLOCK, RNN_BLOCK), RNN_BLOCK ** -0.5),
        "rnn_b_x": nrm((N_RNN_LAYERS, 2, D_RNN), 0.02),
        "rnn_lambda": jnp.log(u) - jnp.log1p(-u),
        "rnn_w_out": nrm((N_RNN_LAYERS, D_RNN, D), D_RNN ** -0.5),
        "rnn_b_out": nrm((N_RNN_LAYERS, D), 0.02),
        "moe_w_router": nrm((DEPTH, D, N_EXPERTS), D ** -0.5),
        "moe_b_router": nrm((DEPTH, N_EXPERTS), 0.01),
        "moe_w_gate": nrm((DEPTH, N_EXPERTS, D, D_EXPERT), D ** -0.5),
        "moe_b_gate": nrm((DEPTH, N_EXPERTS, D_EXPERT), 0.02),
        "moe_w_up": nrm((DEPTH, N_EXPERTS, D, D_EXPERT), D ** -0.5),
        "moe_b_up": nrm((DEPTH, N_EXPERTS, D_EXPERT), 0.02),
        "moe_w_down": nrm((DEPTH, N_EXPERTS, D_EXPERT, D), D_EXPERT ** -0.5),
        "moe_b_down": nrm((DEPTH, N_EXPERTS, D), 0.02),
        "final_norm": gain((D,)),
    }


def reference(x_prompt, x_sample, cache_k, cache_v, state_rglru, c, c_ctx,
              norm_mix, norm_ffn, w_mod, b_mod,
              attn_w_in, attn_b_in, attn_w_out, attn_b_out, attn_sink,
              rnn_w_in, rnn_b_in, rnn_conv_w, rnn_conv_b, rnn_w_a, rnn_b_a, rnn_w_x, rnn_b_x,
              rnn_lambda, rnn_w_out, rnn_b_out,
              moe_w_router, moe_b_router, moe_w_gate, moe_b_gate, moe_w_up, moe_b_up,
              moe_w_down, moe_b_down, final_norm):
    def moe_params(l):
        return (moe_w_router[l], moe_b_router[l], moe_w_gate[l], moe_b_gate[l],
                moe_w_up[l], moe_b_up[l], moe_w_down[l], moe_b_down[l])

    def rnn_params(j):
        return (rnn_w_in[j], rnn_b_in[j], rnn_conv_w[j], rnn_conv_b[j], rnn_w_a[j], rnn_b_a[j],
                rnn_w_x[j], rnn_b_x[j], rnn_lambda[j], rnn_w_out[j], rnn_b_out[j])

    x = x_prompt
    new_k, new_v, new_s = [], [], []
    for l in range(DEPTH):
        j = l // 2
        sh1, sc1, g1, sh2, sc2, g2 = _modulation(c_ctx[None, :], w_mod[l], b_mod[l])
        h = _rmsnorm(x, norm_mix[l]) * (1.0 + sc1) + sh1
        if l % 2 == 0:
            y, k, v = _even_mixer_context(h, attn_w_in[j], attn_b_in[j], attn_w_out[j], attn_b_out[j], attn_sink[j])
            new_k.append(k)
            new_v.append(v)
        else:
            h0 = jnp.zeros((x.shape[0], 2, D_RNN), jnp.float32)
            y, st = _rnn_mixer(h, h0, *rnn_params(j), return_state=True)
            new_s.append(st)
        x = x + g1 * y
        h = _rmsnorm(x, norm_ffn[l]) * (1.0 + sc2) + sh2
        x = x + g2 * _moe(h, *moe_params(l))
    y_prompt = _rmsnorm(x, final_norm)
    new_cache_k = jnp.stack(new_k, axis=1)
    new_cache_v = jnp.stack(new_v, axis=1)
    new_state_rglru = jnp.stack(new_s, axis=1)

    x = x_sample
    for l in range(DEPTH):
        j = l // 2
        sh1, sc1, g1, sh2, sc2, g2 = _modulation(c, w_mod[l], b_mod[l])
        h = _rmsnorm(x, norm_mix[l]) * (1.0 + sc1) + sh1
        if l % 2 == 0:
            y = _even_mixer_latent(h, cache_k[:, j], cache_v[:, j], attn_w_in[j], attn_b_in[j],
                                   attn_w_out[j], attn_b_out[j], attn_sink[j])
        else:
            y = _rnn_mixer(h, state_rglru[:, j], *rnn_params(j), return_state=False)
        x = x + g1 * y
        h = _rmsnorm(x, norm_ffn[l]) * (1.0 + sc2) + sh2
        x = x + g2 * _moe(h, *moe_params(l))
    y_sample = _rmsnorm(x, final_norm)

    return (y_prompt, y_sample, new_cache_k, new_cache_v, new_state_rglru)
```

```python
import functools
import math

import numpy as np
import jax
import jax.numpy as jnp
from jax import lax
from jax.experimental import pallas as pl
from jax.experimental.pallas import tpu as pltpu

D_MODEL = 2048
BATCH = 32
SEQ = 256
DEPTH = 2
DEC_BATCH = 2
DEC_SEQ = 1024
PAST_LEN = 256
GRID_W = 64
N_HEADS = 16
N_KV_HEADS = 2
HEAD_DIM = 64
WINDOW = 128
ROPE_THETA = 10000.0
N_FOURIER_GROUPS = 4
N_RNN_BLOCKS = 8
CONV_WIDTH = 4
CONV_LEFT = 2
RG_C = 8.0
N_EXPERTS = 32
TOP_K = 4
SWIGLU_LIMIT = 7.0
SWIGLU_ALPHA = 1.702
N_MOD = 6
EPS = 1e-6

LANES = 128
SUBLANES = 8
VMEM_LIMIT_BYTES = 56 * 1024 * 1024

MOE_TILE_ROWS = 1024
MOE_SUB_ROWS = 256
MOE_HIDDEN_BLOCK = 256
ROW_TILE = 512
PROJ_OUT_COLS = 512
PROJ_IN_ATTN_COLS = 768
PROJ_IN_RNN_COLS = 1024

F32 = jnp.float32
BF16 = jnp.bfloat16


def _params(semantics):
    return pltpu.CompilerParams(dimension_semantics=semantics, vmem_limit_bytes=VMEM_LIMIT_BYTES)


def _segment_of_tile(i, tile_rows):
    n_ctx_tiles = (BATCH * SEQ) // tile_rows
    tiles_per_latent = DEC_SEQ // tile_rows
    return jnp.where(i < n_ctx_tiles, 0, 1 + (i - n_ctx_tiles) // tiles_per_latent)


def _rms_modulate(x, g, scale, shift):
    ms = jnp.mean(x * x, axis=-1, keepdims=True)
    return (x * lax.rsqrt(ms + EPS) * g) * (1.0 + scale) + shift


def _modulation_kernel(c_ref, w_ref, b_ref, o_ref):
    c = c_ref[...]
    s = (c * jax.nn.sigmoid(c)).astype(BF16)
    o_ref[0] = jnp.dot(s, w_ref[0].astype(BF16), preferred_element_type=F32) + b_ref[0]


def _modulation(cond, w_mod, b_mod):
    d = D_MODEL
    tn = 512
    n_out = N_MOD * d
    return pl.pallas_call(
        _modulation_kernel,
        out_shape=jax.ShapeDtypeStruct((DEPTH, SUBLANES, n_out), F32),
        grid=(DEPTH, n_out // tn),
        in_specs=[
            pl.BlockSpec((SUBLANES, d), lambda l, j: (0, 0)),
            pl.BlockSpec((1, d, tn), lambda l, j: (l, 0, j)),
            pl.BlockSpec((1, 1, tn), lambda l, j: (l, 0, j)),
        ],
        out_specs=pl.BlockSpec((1, SUBLANES, tn), lambda l, j: (l, 0, j)),
        compiler_params=_params(("arbitrary", "arbitrary")),
        name="modulation",
    )(cond, w_mod, b_mod.reshape(DEPTH, 1, n_out))


def _proj_in_kernel(x_ref, g_ref, sc_ref, sh_ref, w_ref, b_ref, o_ref, h_ref):
    @pl.when(pl.program_id(1) == 0)
    def _():
        h_ref[...] = _rms_modulate(x_ref[...], g_ref[...], sc_ref[0], sh_ref[0]).astype(BF16)

    o_ref[...] = jnp.dot(h_ref[...], w_ref[...].astype(BF16), preferred_element_type=F32) + b_ref[...]


def _proj_in(x, g, scale, shift, w, b, tn):
    n, d = x.shape
    n_out = w.shape[1]
    tm = ROW_TILE
    seg = lambda i, j: (_segment_of_tile(i, tm), 0, 0)
    return pl.pallas_call(
        _proj_in_kernel,
        out_shape=jax.ShapeDtypeStruct((n, n_out), F32),
        grid=(n // tm, n_out // tn),
        in_specs=[
            pl.BlockSpec((tm, d), lambda i, j: (i, 0)),
            pl.BlockSpec((1, d), lambda i, j: (0, 0)),
            pl.BlockSpec((1, 1, d), seg),
            pl.BlockSpec((1, 1, d), seg),
            pl.BlockSpec((d, tn), lambda i, j: (0, j)),
            pl.BlockSpec((1, tn), lambda i, j: (0, j)),
        ],
        out_specs=pl.BlockSpec((tm, tn), lambda i, j: (i, j)),
        scratch_shapes=[pltpu.VMEM((tm, d), BF16)],
        compiler_params=_params(("arbitrary", "arbitrary")),
        name="proj_in",
    )(x, g.reshape(1, d), scale, shift, w, b.reshape(1, n_out))


def _proj_out_kernel(a_ref, w_ref, b_ref, gate_ref, res_ref, o_ref):
    y = jnp.dot(a_ref[...], w_ref[...].astype(BF16), preferred_element_type=F32) + b_ref[...]
    o_ref[...] = res_ref[...] + gate_ref[0] * y


def _proj_out(a, w, b, gate, res):
    n, k = a.shape
    d = w.shape[1]
    tm, tn = ROW_TILE, PROJ_OUT_COLS
    return pl.pallas_call(
        _proj_out_kernel,
        out_shape=jax.ShapeDtypeStruct((n, d), F32),
        grid=(n // tm, d // tn),
        in_specs=[
            pl.BlockSpec((tm, k), lambda i, j: (i, 0)),
            pl.BlockSpec((k, tn), lambda i, j: (0, j)),
            pl.BlockSpec((1, tn), lambda i, j: (0, j)),
            pl.BlockSpec((1, 1, tn), lambda i, j: (_segment_of_tile(i, tm), 0, j)),
            pl.BlockSpec((tm, tn), lambda i, j: (i, j)),
        ],
        out_specs=pl.BlockSpec((tm, tn), lambda i, j: (i, j)),
        compiler_params=_params(("arbitrary", "arbitrary")),
        name="proj_out",
    )(a, w, b.reshape(1, d), gate, res)


def _dot_nt(a, b):
    return lax.dot_general(a, b, (((1,), (1,)), ((), ())), preferred_element_type=F32)


def _head_pair_operands(k, v, group):
    lane = lax.broadcasted_iota(jnp.int32, k.shape, 1)
    low = lane < HEAD_DIM
    k_sw = pltpu.roll(k, HEAD_DIM, 1)
    v_sw = pltpu.roll(v, HEAD_DIM, 1)
    if group == 0:
        kd = jnp.where(low, k, k_sw)
        vd = jnp.where(low, v, v_sw)
    else:
        kd = jnp.where(low, k_sw, k)
        vd = jnp.where(low, v_sw, v)
    v_lo = jnp.where(low, vd, 0.0).astype(BF16)
    v_hi = jnp.where(low, 0.0, vd).astype(BF16)
    return kd.astype(BF16), v_lo, v_hi


def _split_pair(q2):
    lane = lax.broadcasted_iota(jnp.int32, q2.shape, 1)
    low = lane < HEAD_DIM
    qs = q2 * (HEAD_DIM ** -0.5)
    return jnp.where(low, qs, 0.0).astype(BF16), jnp.where(low, 0.0, qs).astype(BF16)


def _softmax_pv(scores, values, sink):
    m = jnp.full((scores[0].shape[0], 1), sink, F32)
    for s in scores:
        m = jnp.maximum(m, jnp.max(s, axis=-1, keepdims=True))
    den = jnp.exp(sink - m)
    ps = []
    for s in scores:
        p = jnp.exp(s - m)
        den = den + jnp.sum(p, axis=-1, keepdims=True)
        ps.append(p)
    out = None
    for p, v in zip(ps, values):
        o = jnp.dot((p / den).astype(BF16), v, preferred_element_type=F32)
        out = o if out is None else out + o
    return out


def _dft_matrices(t):
    idx = np.arange(t)
    ang = 2.0 * np.pi * ((idx[:, None] * idx[None, :]) % t) / t
    m = np.concatenate([np.cos(ang), np.sin(ang)], axis=0) / math.sqrt(t)
    return jnp.asarray(m, dtype=BF16)


def _dft_channel_matrix(c):
    idx = np.arange(c)
    ang = 2.0 * np.pi * ((idx[:, None] * idx[None, :]) % c) / c
    m = np.concatenate([np.cos(ang), -np.sin(ang)], axis=0) / math.sqrt(c)
    return jnp.asarray(m, dtype=BF16)


def _fourier_group(f_g, ts_ref, cs_ref):
    t = f_g.shape[0]
    ab = jnp.dot(ts_ref[...], f_g.astype(BF16), preferred_element_type=F32)
    lhs = jnp.concatenate([ab[:t], ab[t:]], axis=1).astype(BF16)
    return jnp.dot(lhs, cs_ref[...], preferred_element_type=F32)


def _ctx_mixer_kernel(sink_ref, p_ref, ts_ref, cs_ref, o_ref):
    attn_w = N_HEADS * HEAD_DIM
    kv_w = N_KV_HEADS * HEAD_DIM
    pair_w = 2 * HEAD_DIM
    group_heads = N_HEADS // N_KV_HEADS
    k = p_ref[:, attn_w:attn_w + kv_w]
    v = p_ref[:, attn_w + kv_w:attn_w + 2 * kv_w]
    for g in range(N_KV_HEADS):
        kd, v_lo, v_hi = _head_pair_operands(k, v, g)
        for i in range(group_heads // 2):
            pair = g * (group_heads // 2) + i
            q_lo, q_hi = _split_pair(p_ref[:, pair * pair_w:(pair + 1) * pair_w])
            o = _softmax_pv([_dot_nt(q_lo, kd)], [v_lo], sink_ref[2 * pair])
            o = o + _softmax_pv([_dot_nt(q_hi, kd)], [v_hi], sink_ref[2 * pair + 1])
            o_ref[:, pair * pair_w:(pair + 1) * pair_w] = o.astype(o_ref.dtype)
    f0 = attn_w + 2 * kv_w
    fg = (D_MODEL - attn_w) // N_FOURIER_GROUPS
    for g in range(N_FOURIER_GROUPS):
        z = _fourier_group(p_ref[:, f0 + g * fg:f0 + (g + 1) * fg], ts_ref, cs_ref)
        o_ref[:, attn_w + g * fg:attn_w + (g + 1) * fg] = z.astype(o_ref.dtype)


def _ctx_mixer(p, sink):
    n, width = BATCH * SEQ, p.shape[1]
    fg = (D_MODEL - N_HEADS * HEAD_DIM) // N_FOURIER_GROUPS
    return pl.pallas_call(
        _ctx_mixer_kernel,
        out_shape=jax.ShapeDtypeStruct((n, D_MODEL), BF16),
        grid=(n // SEQ,),
        in_specs=[
            pl.BlockSpec(memory_space=pltpu.SMEM),
            pl.BlockSpec((SEQ, width), lambda b: (b, 0)),
            pl.BlockSpec((2 * SEQ, SEQ), lambda b: (0, 0)),
            pl.BlockSpec((2 * fg, fg), lambda b: (0, 0)),
        ],
        out_specs=pl.BlockSpec((SEQ, D_MODEL), lambda b: (b, 0)),
        compiler_params=_params(("arbitrary",)),
        name="ctx_mixer",
    )(sink, p, _dft_matrices(SEQ), _dft_channel_matrix(fg))


def _rope_tables():
    rows = DEC_SEQ // GRID_W
    row = np.repeat(np.arange(rows, dtype=np.float32), GRID_W)
    col = np.tile(np.arange(GRID_W, dtype=np.float32), rows)
    n_freq = HEAD_DIM // 4
    inv = jnp.asarray(ROPE_THETA, F32) ** (-jnp.arange(n_freq, dtype=F32) / n_freq)
    ang = jnp.concatenate([row[:, None] * inv, col[:, None] * inv], axis=-1)
    cos = jnp.repeat(jnp.cos(ang), 2, axis=-1)
    sin = jnp.repeat(jnp.sin(ang), 2, axis=-1)
    sign = jnp.tile(jnp.asarray([-1.0, 1.0], F32), HEAD_DIM // 2)
    return jnp.tile(cos, (1, 2)), jnp.tile(sin * sign, (1, 2))


def _rope(x, cos, sin_signed):
    lane = lax.broadcasted_iota(jnp.int32, x.shape, 1)
    width = x.shape[1]
    partner = jnp.where(lane % 2 == 0, pltpu.roll(x, width - 1, 1), pltpu.roll(x, 1, 1))
    return x * cos + partner * sin_signed


def _lat_mixer_kernel(sink_ref, p_ref, ck_ref, cv_ref, cos_ref, sin_ref, ts_ref, cs_ref, o_ref, q_scr, k_scr):
    attn_w = N_HEADS * HEAD_DIM
    kv_w = N_KV_HEADS * HEAD_DIM
    pair_w = 2 * HEAD_DIM
    group_heads = N_HEADS // N_KV_HEADS
    q_rows = 256
    cos = cos_ref[...]
    sin = sin_ref[...]
    for pair in range(N_HEADS // 2):
        q_scr[:, pair * pair_w:(pair + 1) * pair_w] = _rope(p_ref[:, pair * pair_w:(pair + 1) * pair_w], cos, sin)
    k_scr[...] = _rope(p_ref[:, attn_w:attn_w + kv_w], cos, sin)
    v = p_ref[:, attn_w + kv_w:attn_w + 2 * kv_w]
    ck = ck_ref[0]
    cv = cv_ref[0]
    k = k_scr[...]
    ops = []
    for g in range(N_KV_HEADS):
        ops.append(_head_pair_operands(k, v, g) + _head_pair_operands(ck, cv, g))

    def chunk(c, carry):
        r0 = pl.multiple_of(c * q_rows, q_rows)
        qi = r0 + lax.broadcasted_iota(jnp.int32, (q_rows, DEC_SEQ), 0)
        kj = lax.broadcasted_iota(jnp.int32, (q_rows, DEC_SEQ), 1)
        valid = jnp.abs(qi - kj) <= WINDOW
        for g in range(N_KV_HEADS):
            kd, v_lo, v_hi, ckd, cv_lo, cv_hi = ops[g]
            for i in range(group_heads // 2):
                pair = g * (group_heads // 2) + i
                q_lo, q_hi = _split_pair(q_scr[pl.ds(r0, q_rows), pair * pair_w:(pair + 1) * pair_w])
                s_lo = jnp.where(valid, _dot_nt(q_lo, kd), -jnp.inf)
                o = _softmax_pv([_dot_nt(q_lo, ckd), s_lo], [cv_lo, v_lo], sink_ref[2 * pair])
                s_hi = jnp.where(valid, _dot_nt(q_hi, kd), -jnp.inf)
                o = o + _softmax_pv([_dot_nt(q_hi, ckd), s_hi], [cv_hi, v_hi], sink_ref[2 * pair + 1])
                o_ref[pl.ds(r0, q_rows), pair * pair_w:(pair + 1) * pair_w] = o.astype(o_ref.dtype)
        return carry

    lax.fori_loop(0, DEC_SEQ // q_rows, chunk, 0)
    f0 = attn_w + 2 * kv_w
    fg = (D_MODEL - attn_w) // N_FOURIER_GROUPS
    for g in range(N_FOURIER_GROUPS):
        z = _fourier_group(p_ref[:, f0 + g * fg:f0 + (g + 1) * fg], ts_ref, cs_ref)
        o_ref[:, attn_w + g * fg:attn_w + (g + 1) * fg] = z.astype(o_ref.dtype)


def _lat_mixer(p, cache_k, cache_v, sink):
    n, width = DEC_BATCH * DEC_SEQ, p.shape[1]
    first = (BATCH * SEQ) // DEC_SEQ
    kv_w = N_KV_HEADS * HEAD_DIM
    attn_w = N_HEADS * HEAD_DIM
    fg = (D_MODEL - attn_w) // N_FOURIER_GROUPS
    cos, sin = _rope_tables()
    return pl.pallas_call(
        _lat_mixer_kernel,
        out_shape=jax.ShapeDtypeStruct((n, D_MODEL), BF16),
        grid=(n // DEC_SEQ,),
        in_specs=[
            pl.BlockSpec(memory_space=pltpu.SMEM),
            pl.BlockSpec((DEC_SEQ, width), lambda b: (first + b, 0)),
            pl.BlockSpec((1, PAST_LEN, kv_w), lambda b: (b, 0, 0)),
            pl.BlockSpec((1, PAST_LEN, kv_w), lambda b: (b, 0, 0)),
            pl.BlockSpec((DEC_SEQ, 2 * HEAD_DIM), lambda b: (0, 0)),
            pl.BlockSpec((DEC_SEQ, 2 * HEAD_DIM), lambda b: (0, 0)),
            pl.BlockSpec((2 * DEC_SEQ, DEC_SEQ), lambda b: (0, 0)),
            pl.BlockSpec((2 * fg, fg), lambda b: (0, 0)),
        ],
        out_specs=pl.BlockSpec((DEC_SEQ, D_MODEL), lambda b: (b, 0)),
        scratch_shapes=[pltpu.VMEM((DEC_SEQ, attn_w), F32), pltpu.VMEM((DEC_SEQ, kv_w), F32)],
        compiler_params=_params(("arbitrary",)),
        name="lat_mixer",
    )(sink, p, cache_k, cache_v, cos, sin, _dft_matrices(DEC_SEQ), _dft_channel_matrix(fg))


def _rnn_core_kernel(xr_ref, gr_ref, cw_ref, cb_ref, wa_ref, ba_ref, wx_ref, bx_ref, lam_ref, h0_ref,
                     y_ref, st_ref, a_scr, b_scr, h_scr, *, n_seq, seq_len):
    rows = n_seq * seq_len
    n_lb = xr_ref.shape[1] // LANES
    xr = xr_ref[...]
    t_idx = lax.broadcasted_iota(jnp.int32, (rows, 1), 0) % seq_len
    xc = jnp.broadcast_to(cb_ref[...], xr.shape)
    for tap in range(CONV_WIDTH):
        off = tap - CONV_LEFT
        shifted = xr if off == 0 else pltpu.roll(xr, (-off) % rows, 0)
        valid = (t_idx + off >= 0) & (t_idx + off < seq_len)
        xc = xc + jnp.where(valid, shifted, 0.0) * cw_ref[tap:tap + 1, :]
    xcb = xc.astype(BF16)
    for d in range(2):
        r = jax.nn.sigmoid(jnp.dot(xcb, wa_ref[d, 0].astype(BF16), preferred_element_type=F32) + ba_ref[d:d + 1, :])
        gi = jax.nn.sigmoid(jnp.dot(xcb, wx_ref[d, 0].astype(BF16), preferred_element_type=F32) + bx_ref[d:d + 1, :])
        neg_lam = -lam_ref[d:d + 1, :]
        softplus = jnp.maximum(neg_lam, 0.0) + jnp.log1p(jnp.exp(-jnp.abs(neg_lam)))
        log_a = -RG_C * r * softplus
        a = jnp.exp(log_a)
        th = jnp.tanh(log_a)
        b = jnp.sqrt(-2.0 * th / (1.0 - th)) * (gi * xc)
        for lb in range(n_lb):
            a_scr[d * n_lb + lb] = a[:, lb * LANES:(lb + 1) * LANES]
            b_scr[d * n_lb + lb] = b[:, lb * LANES:(lb + 1) * LANES]

    def time_rows(t):
        return pl.ds(t, n_seq, stride=seq_len)

    def lane_block(ref, k, lb):
        return ref[:, k, lb * LANES:(lb + 1) * LANES]

    def fwd(t, hs):
        out = []
        for lb in range(n_lb):
            h = a_scr[lb, time_rows(t), :] * hs[lb] + b_scr[lb, time_rows(t), :]
            h_scr[lb, time_rows(t), :] = h
            out.append(h)
        return tuple(out)

    hs = lax.fori_loop(0, seq_len, fwd, tuple(lane_block(h0_ref, 0, lb) for lb in range(n_lb)))
    for lb in range(n_lb):
        st_ref[:, 0, lb * LANES:(lb + 1) * LANES] = hs[lb]

    def bwd(i, hs):
        t = seq_len - 1 - i
        out = []
        for lb in range(n_lb):
            h = a_scr[n_lb + lb, time_rows(t), :] * hs[lb] + b_scr[n_lb + lb, time_rows(t), :]
            h_scr[lb, time_rows(t), :] = h_scr[lb, time_rows(t), :] + h
            out.append(h)
        return tuple(out)

    hs = lax.fori_loop(0, seq_len, bwd, tuple(lane_block(h0_ref, 1, lb) for lb in range(n_lb)))
    for lb in range(n_lb):
        st_ref[:, 1, lb * LANES:(lb + 1) * LANES] = hs[lb]
    for lb in range(n_lb):
        gr = gr_ref[:, lb * LANES:(lb + 1) * LANES]
        gelu = 0.5 * gr * (1.0 + jnp.tanh(math.sqrt(2.0 / math.pi) * (gr + 0.044715 * (gr * gr * gr))))
        y_ref[:, lb * LANES:(lb + 1) * LANES] = (gelu * h_scr[lb]).astype(y_ref.dtype)


def _rnn_core(p, first_row, n, h0, conv_w, conv_b, w_a, b_a, w_x, b_x, lam, n_seq, seq_len):
    d_rnn = p.shape[1] // 2
    cb = d_rnn // N_RNN_BLOCKS
    rows = n_seq * seq_len
    n_batch = n // seq_len
    first = first_row // rows
    kern = functools.partial(_rnn_core_kernel, n_seq=n_seq, seq_len=seq_len)
    return pl.pallas_call(
        kern,
        out_shape=(jax.ShapeDtypeStruct((n, d_rnn), BF16), jax.ShapeDtypeStruct((n_batch, 2, d_rnn), F32)),
        grid=(n // rows, N_RNN_BLOCKS),
        in_specs=[
            pl.BlockSpec((rows, cb), lambda i, c: (first + i, c)),
            pl.BlockSpec((rows, cb), lambda i, c: (first + i, N_RNN_BLOCKS + c)),
            pl.BlockSpec((CONV_WIDTH, cb), lambda i, c: (0, c)),
            pl.BlockSpec((1, cb), lambda i, c: (0, c)),
            pl.BlockSpec((2, 1, cb, cb), lambda i, c: (0, c, 0, 0)),
            pl.BlockSpec((2, cb), lambda i, c: (0, c)),
            pl.BlockSpec((2, 1, cb, cb), lambda i, c: (0, c, 0, 0)),
            pl.BlockSpec((2, cb), lambda i, c: (0, c)),
            pl.BlockSpec((2, cb), lambda i, c: (0, c)),
            pl.BlockSpec((n_seq, 2, cb), lambda i, c: (i, 0, c)),
        ],
        out_specs=(
            pl.BlockSpec((rows, cb), lambda i, c: (i, c)),
            pl.BlockSpec((n_seq, 2, cb), lambda i, c: (i, 0, c)),
        ),
        scratch_shapes=[pltpu.VMEM((2 * cb // LANES, rows, LANES), F32), pltpu.VMEM((2 * cb // LANES, rows, LANES), F32),
                        pltpu.VMEM((cb // LANES, rows, LANES), F32)],
        compiler_params=_params(("arbitrary", "arbitrary")),
        name="rnn_core",
    )(p, p, conv_w, conv_b.reshape(1, d_rnn), w_a, b_a, w_x, b_x, lam, h0)


def _router_kernel(x_ref, g_ref, sc_ref, sh_ref, w_ref, b_ref, h_ref, idx_ref, gate_ref, rank_ref, cnt_ref, run_ref):
    tm = x_ref.shape[0]

    @pl.when(pl.program_id(0) == 0)
    def _():
        run_ref[...] = jnp.zeros_like(run_ref)

    h = _rms_modulate(x_ref[...], g_ref[...], sc_ref[0], sh_ref[0])
    h_ref[...] = h.astype(h_ref.dtype)
    logits = jnp.dot(h, w_ref[...], preferred_element_type=F32, precision=lax.Precision.HIGHEST) + b_ref[...]
    lane = lax.broadcasted_iota(jnp.int32, logits.shape, 1).astype(F32)
    col = lax.broadcasted_iota(jnp.int32, (tm, TOP_K), 1)
    chosen = jnp.zeros(logits.shape, F32)
    top_v, top_i, hits = [], [], []
    work = logits
    for _ in range(TOP_K):
        m = jnp.max(work, axis=-1, keepdims=True)
        first = jnp.min(jnp.where(work == m, lane, float(N_EXPERTS)), axis=-1, keepdims=True)
        hit = lane == first
        work = jnp.where(hit, -jnp.inf, work)
        chosen = jnp.where(hit, 1.0, chosen)
        top_v.append(m)
        top_i.append(first)
        hits.append(hit)
    exps = [jnp.exp(v - top_v[0]) for v in top_v]
    den = exps[0]
    for e in exps[1:]:
        den = den + e
    ri = lax.broadcasted_iota(jnp.int32, (tm, tm), 0)
    ci = lax.broadcasted_iota(jnp.int32, (tm, tm), 1)
    before = (ci < ri).astype(BF16)
    rank_all = jnp.dot(before, chosen.astype(BF16), preferred_element_type=F32) + run_ref[...]
    idx_out = jnp.zeros((tm, TOP_K), F32)
    gate_out = jnp.zeros((tm, TOP_K), F32)
    rank_out = jnp.zeros((tm, TOP_K), F32)
    for k in range(TOP_K):
        rk = jnp.sum(jnp.where(hits[k], rank_all, 0.0), axis=-1, keepdims=True)
        idx_out = jnp.where(col == k, top_i[k], idx_out)
        gate_out = jnp.where(col == k, exps[k] / den, gate_out)
        rank_out = jnp.where(col == k, rk, rank_out)
    idx_ref[...] = idx_out.astype(jnp.int32)
    gate_ref[...] = gate_out
    rank_ref[...] = rank_out.astype(jnp.int32)
    run_ref[...] = run_ref[...] + jnp.sum(chosen, axis=0, keepdims=True)
    cnt_ref[...] = run_ref[...]


def _router(x, g, scale, shift, w, b):
    n, d = x.shape
    e = w.shape[1]
    tm = ROW_TILE
    seg = lambda i: (_segment_of_tile(i, tm), 0, 0)
    small = pl.BlockSpec((tm, TOP_K), lambda i: (i, 0))
    return pl.pallas_call(
        _router_kernel,
        out_shape=(
            jax.ShapeDtypeStruct((n, d), BF16),
            jax.ShapeDtypeStruct((n, TOP_K), jnp.int32),
            jax.ShapeDtypeStruct((n, TOP_K), F32),
            jax.ShapeDtypeStruct((n, TOP_K), jnp.int32),
            jax.ShapeDtypeStruct((1, e), F32),
        ),
        grid=(n // tm,),
        in_specs=[
            pl.BlockSpec((tm, d), lambda i: (i, 0)),
            pl.BlockSpec((1, d), lambda i: (0, 0)),
            pl.BlockSpec((1, 1, d), seg),
            pl.BlockSpec((1, 1, d), seg),
            pl.BlockSpec((d, e), lambda i: (0, 0)),
            pl.BlockSpec((1, e), lambda i: (0, 0)),
        ],
        out_specs=(pl.BlockSpec((tm, d), lambda i: (i, 0)), small, small, small, pl.BlockSpec((1, e), lambda i: (0, 0))),
        scratch_shapes=[pltpu.VMEM((1, e), F32)],
        compiler_params=_params(("arbitrary",)),
        name="router",
    )(x, g.reshape(1, d), scale, shift, w, b.reshape(1, e))


def _moe_kernel(te_ref, tv_ref, ti_ref, xs_ref, wg_ref, wu_ref, wd_ref, bg_ref, bu_ref, bd_ref, o_ref,
                wg_scr, wu_scr, wd_scr):
    t = pl.program_id(0)
    j = pl.program_id(1)
    n_valid = tv_ref[t]
    n_sub_total = o_ref.shape[0] // MOE_SUB_ROWS

    @pl.when(n_valid > 0)
    def _():
        wg_scr[...] = wg_ref[0].astype(BF16)
        wu_scr[...] = wu_ref[0].astype(BF16)
        wd_scr[...] = wd_ref[0].astype(BF16)
        n_sub = (n_valid + MOE_SUB_ROWS - 1) // MOE_SUB_ROWS

        def body(r, carry):
            rows = pl.ds(pl.multiple_of(r * MOE_SUB_ROWS, MOE_SUB_ROWS), MOE_SUB_ROWS)
            x = xs_ref[rows, :]
            g = jnp.dot(x, wg_scr[...], preferred_element_type=F32) + bg_ref[0]
            u = jnp.dot(x, wu_scr[...], preferred_element_type=F32) + bu_ref[0]
            g = jnp.minimum(g, SWIGLU_LIMIT)
            u = jnp.clip(u, -SWIGLU_LIMIT, SWIGLU_LIMIT)
            act = (u + 1.0) * (g * jax.nn.sigmoid(SWIGLU_ALPHA * g))
            y = jnp.dot(act.astype(BF16), wd_scr[...], preferred_element_type=F32)

            @pl.when(j == 0)
            def _():
                o_ref[rows, :] = y + bd_ref[0]

            @pl.when(j > 0)
            def _():
                o_ref[rows, :] = o_ref[rows, :] + y

            return carry

        lax.fori_loop(0, n_sub, body, 0)

        @pl.when(j == 0)
        def _():
            def clear(r, carry):
                rows = pl.ds(pl.multiple_of(r * MOE_SUB_ROWS, MOE_SUB_ROWS), MOE_SUB_ROWS)
                o_ref[rows, :] = jnp.zeros((MOE_SUB_ROWS, o_ref.shape[1]), o_ref.dtype)
                return carry

            lax.fori_loop(n_sub, n_sub_total, clear, 0)


def _moe_experts(xs, tile_expert, tile_valid, tile_index, w_gate, b_gate, w_up, b_up, w_down, b_down):
    p_rows, d = xs.shape
    e, _, hdim = w_gate.shape
    tm, th = MOE_TILE_ROWS, MOE_HIDDEN_BLOCK
    n_tiles = p_rows // tm
    n_j = hdim // th

    def jj(t, j, tv):
        return jnp.where(tv[t] > 0, j, n_j - 1)

    grid_spec = pltpu.PrefetchScalarGridSpec(
        num_scalar_prefetch=3,
        grid=(n_tiles, n_j),
        in_specs=[
            pl.BlockSpec((tm, d), lambda t, j, te, tv, ti: (ti[t], 0)),
            pl.BlockSpec((1, d, th), lambda t, j, te, tv, ti: (te[t], 0, jj(t, j, tv))),
            pl.BlockSpec((1, d, th), lambda t, j, te, tv, ti: (te[t], 0, jj(t, j, tv))),
            pl.BlockSpec((1, th, d), lambda t, j, te, tv, ti: (te[t], jj(t, j, tv), 0)),
            pl.BlockSpec((1, 1, th), lambda t, j, te, tv, ti: (te[t], 0, jj(t, j, tv))),
            pl.BlockSpec((1, 1, th), lambda t, j, te, tv, ti: (te[t], 0, jj(t, j, tv))),
            pl.BlockSpec((1, 1, d), lambda t, j, te, tv, ti: (te[t], 0, 0)),
        ],
        out_specs=pl.BlockSpec((tm, d), lambda t, j, te, tv, ti: (ti[t], 0)),
        scratch_shapes=[pltpu.VMEM((d, th), BF16), pltpu.VMEM((d, th), BF16), pltpu.VMEM((th, d), BF16)],
    )
    return pl.pallas_call(
        _moe_kernel,
        out_shape=jax.ShapeDtypeStruct((p_rows, d), F32),
        grid_spec=grid_spec,
        compiler_params=_params(("arbitrary", "arbitrary")),
        name="moe_experts",
    )(tile_expert, tile_valid, tile_index, xs, w_gate, w_up, w_down,
      b_gate.reshape(e, 1, hdim), b_up.reshape(e, 1, hdim), b_down.reshape(e, 1, d))


def _moe_layer(x, g, scale, shift, gate2, w_router, b_router, w_gate, b_gate, w_up, b_up, w_down, b_down):
    n, d = x.shape
    e = w_router.shape[1]
    tm = MOE_TILE_ROWS
    h, top_i, gates, rank, counts = _router(x, g, scale, shift, w_router, b_router)
    counts = counts[0].astype(jnp.int32)
    padded = ((counts + tm - 1) // tm) * tm
    ends = jnp.cumsum(padded)
    starts = ends - padded
    n_tiles = (n * TOP_K) // tm + e
    p_rows = n_tiles * tm
    tile_start = jnp.arange(n_tiles, dtype=jnp.int32) * tm
    n_used = ends[-1] // tm
    tile_expert = jnp.minimum(jnp.searchsorted(ends, tile_start, side="right"), e - 1).astype(jnp.int32)
    tile_valid = jnp.clip(counts[tile_expert] - (tile_start - starts[tile_expert]), 0, tm)
    tile_valid = jnp.where(tile_start < ends[-1], tile_valid, 0).astype(jnp.int32)
    last = jnp.maximum(n_used - 1, 0)
    tile_index = jnp.minimum(jnp.arange(n_tiles, dtype=jnp.int32), last).astype(jnp.int32)
    tile_expert = jnp.where(tile_start < ends[-1], tile_expert, tile_expert[last]).astype(jnp.int32)
    pos = starts[top_i] + rank
    token = jnp.broadcast_to(jnp.arange(n, dtype=jnp.int32)[:, None], (n, TOP_K))
    src = jnp.zeros((p_rows,), jnp.int32).at[pos.reshape(-1)].set(token.reshape(-1))
    xs = jnp.take(h, src, axis=0)
    ys = _moe_experts(xs, tile_expert, tile_valid, tile_index, w_gate, b_gate, w_up, b_up, w_down, b_down)
    picked = jnp.take(ys, pos.reshape(-1), axis=0).reshape(n, TOP_K, d)
    moe = jnp.sum(picked * gates[:, :, None], axis=1)
    seg = jnp.concatenate([jnp.zeros((BATCH * SEQ,), jnp.int32),
                           1 + jnp.repeat(jnp.arange(DEC_BATCH, dtype=jnp.int32), DEC_SEQ)])
    return x + gate2[seg, 0] * moe


def _final_norm_kernel(x_ref, g_ref, o_ref):
    x = x_ref[...]
    ms = jnp.mean(x * x, axis=-1, keepdims=True)
    o_ref[...] = x * lax.rsqrt(ms + EPS) * g_ref[...]


def _final_norm(x, g):
    n, d = x.shape
    tm = ROW_TILE
    return pl.pallas_call(
        _final_norm_kernel,
        out_shape=jax.ShapeDtypeStruct((n, d), F32),
        grid=(n // tm,),
        in_specs=[pl.BlockSpec((tm, d), lambda i: (i, 0)), pl.BlockSpec((1, d), lambda i: (0, 0))],
        out_specs=pl.BlockSpec((tm, d), lambda i: (i, 0)),
        compiler_params=_params(("arbitrary",)),
        name="final_norm",
    )(x, g.reshape(1, d))


def kernel(x_prompt, x_sample, cache_k, cache_v, state_rglru, c, c_ctx, norm_mix, norm_ffn, w_mod, b_mod,
           attn_w_in, attn_b_in, attn_w_out, attn_b_out, attn_sink,
           rnn_w_in, rnn_b_in, rnn_conv_w, rnn_conv_b, rnn_w_a, rnn_b_a, rnn_w_x, rnn_b_x,
           rnn_lambda, rnn_w_out, rnn_b_out,
           moe_w_router, moe_b_router, moe_w_gate, moe_b_gate, moe_w_up, moe_b_up,
           moe_w_down, moe_b_down, final_norm):
    d = D_MODEL
    n_ctx = BATCH * SEQ
    attn_w = N_HEADS * HEAD_DIM
    kv_w = N_KV_HEADS * HEAD_DIM
    x = jnp.concatenate([x_prompt.reshape(n_ctx, d), x_sample.reshape(DEC_BATCH * DEC_SEQ, d)], axis=0)
    cond = jnp.concatenate([c_ctx[None, :], c, jnp.zeros((SUBLANES - 1 - DEC_BATCH, d), F32)], axis=0)
    mods = _modulation(cond, w_mod, b_mod)

    def mod(l, k):
        return mods[l, :, k * d:(k + 1) * d].reshape(SUBLANES, 1, d)

    new_k, new_v, new_s = [], [], []
    for l in range(DEPTH):
        j = l // 2
        sh1, sc1, g1, sh2, sc2, g2 = [mod(l, k) for k in range(N_MOD)]
        if l % 2 == 0:
            p = _proj_in(x, norm_mix[l], sc1, sh1, attn_w_in[j], attn_b_in[j], tn=PROJ_IN_ATTN_COLS)
            new_k.append(p[:n_ctx, attn_w:attn_w + kv_w].reshape(BATCH, SEQ, N_KV_HEADS, HEAD_DIM))
            new_v.append(p[:n_ctx, attn_w + kv_w:attn_w + 2 * kv_w].reshape(BATCH, SEQ, N_KV_HEADS, HEAD_DIM))
            mix_ctx = _ctx_mixer(p, attn_sink[j])
            mix_lat = _lat_mixer(p, cache_k[:, j].reshape(DEC_BATCH, PAST_LEN, kv_w),
                                 cache_v[:, j].reshape(DEC_BATCH, PAST_LEN, kv_w), attn_sink[j])
            mix = jnp.concatenate([mix_ctx, mix_lat], axis=0)
            x = _proj_out(mix, attn_w_out[j], attn_b_out[j], g1, x)
        else:
            p = _proj_in(x, norm_mix[l], sc1, sh1, rnn_w_in[j], rnn_b_in[j], tn=PROJ_IN_RNN_COLS)
            args = (rnn_conv_w[j], rnn_conv_b[j], rnn_w_a[j], rnn_b_a[j], rnn_w_x[j], rnn_b_x[j], rnn_lambda[j])
            h0_ctx = jnp.zeros((BATCH, 2, d), F32)
            mix_ctx, st = _rnn_core(p, 0, n_ctx, h0_ctx, *args, n_seq=SUBLANES, seq_len=SEQ)
            mix_lat, _ = _rnn_core(p, n_ctx, DEC_BATCH * DEC_SEQ, state_rglru[:, j], *args,
                                   n_seq=DEC_BATCH, seq_len=DEC_SEQ)
            new_s.append(st)
            mix = jnp.concatenate([mix_ctx, mix_lat], axis=0)
            x = _proj_out(mix, rnn_w_out[j], rnn_b_out[j], g1, x)
        x = _moe_layer(x, norm_ffn[l], sc2, sh2, g2, moe_w_router[l], moe_b_router[l],
                       moe_w_gate[l], moe_b_gate[l], moe_w_up[l], moe_b_up[l], moe_w_down[l], moe_b_down[l])
    y = _final_norm(x, final_norm)
    y_prompt = y[:n_ctx].reshape(BATCH, SEQ, d)
    y_sample = y[n_ctx:].reshape(DEC_BATCH, DEC_SEQ, d)
    return (y_prompt, y_sample, jnp.stack(new_k, axis=1), jnp.stack(new_v, axis=1), jnp.stack(new_s, axis=1))
```

```python
import functools
import math

import numpy as np
import jax
import jax.numpy as jnp
from jax import lax
from jax.experimental import pallas as pl
from jax.experimental.pallas import tpu as pltpu

D_MODEL = 2048
BATCH = 32
SEQ = 256
DEPTH = 2
DEC_BATCH = 2
DEC_SEQ = 1024
PAST_LEN = 256
GRID_W = 64
N_HEADS = 16
N_KV_HEADS = 2
HEAD_DIM = 64
WINDOW = 128
ROPE_THETA = 10000.0
N_FOURIER_GROUPS = 4
N_RNN_BLOCKS = 8
CONV_WIDTH = 4
CONV_LEFT = 2
RG_C = 8.0
N_EXPERTS = 32
TOP_K = 4
SWIGLU_LIMIT = 7.0
SWIGLU_ALPHA = 1.702
N_MOD = 6
EPS = 1e-6

LANES = 128
SUBLANES = 8
VMEM_LIMIT_BYTES = 56 * 1024 * 1024

MOE_TILE_ROWS = 1024
MOE_SUB_ROWS = 256
MOE_HIDDEN_BLOCK = 256
ROW_TILE = 512
COMBINE_ROWS = 128
PROJ_OUT_COLS = 512
PROJ_IN_ATTN_COLS = 768
PROJ_IN_RNN_COLS = 1024

F32 = jnp.float32
BF16 = jnp.bfloat16


def _params(semantics):
    return pltpu.CompilerParams(dimension_semantics=semantics, vmem_limit_bytes=VMEM_LIMIT_BYTES)


def _segment_of_tile(i, tile_rows):
    n_ctx_tiles = (BATCH * SEQ) // tile_rows
    tiles_per_latent = DEC_SEQ // tile_rows
    return jnp.where(i < n_ctx_tiles, 0, 1 + (i - n_ctx_tiles) // tiles_per_latent)


def _rms_modulate(x, g, scale, shift):
    ms = jnp.mean(x * x, axis=-1, keepdims=True)
    return (x * lax.rsqrt(ms + EPS) * g) * (1.0 + scale) + shift


def _modulation_kernel(c_ref, w_ref, b_ref, o_ref):
    c = c_ref[...]
    s = (c * jax.nn.sigmoid(c)).astype(BF16)
    o_ref[0] = jnp.dot(s, w_ref[0].astype(BF16), preferred_element_type=F32) + b_ref[0]


def _modulation(cond, w_mod, b_mod):
    d = D_MODEL
    tn = 512
    n_out = N_MOD * d
    return pl.pallas_call(
        _modulation_kernel,
        out_shape=jax.ShapeDtypeStruct((DEPTH, SUBLANES, n_out), F32),
        grid=(DEPTH, n_out // tn),
        in_specs=[
            pl.BlockSpec((SUBLANES, d), lambda l, j: (0, 0)),
            pl.BlockSpec((1, d, tn), lambda l, j: (l, 0, j)),
            pl.BlockSpec((1, 1, tn), lambda l, j: (l, 0, j)),
        ],
        out_specs=pl.BlockSpec((1, SUBLANES, tn), lambda l, j: (l, 0, j)),
        compiler_params=_params(("arbitrary", "arbitrary")),
        name="modulation",
    )(cond, w_mod, b_mod.reshape(DEPTH, 1, n_out))


def _proj_in_kernel(x_ref, g_ref, sc_ref, sh_ref, w_ref, b_ref, o_ref, h_ref):
    @pl.when(pl.program_id(1) == 0)
    def _():
        h_ref[...] = _rms_modulate(x_ref[...], g_ref[...], sc_ref[0], sh_ref[0]).astype(BF16)

    o_ref[...] = jnp.dot(h_ref[...], w_ref[...].astype(BF16), preferred_element_type=F32) + b_ref[...]


def _proj_in(x, g, scale, shift, w, b, tn):
    n, d = x.shape
    n_out = w.shape[1]
    tm = ROW_TILE
    seg = lambda i, j: (_segment_of_tile(i, tm), 0, 0)
    return pl.pallas_call(
        _proj_in_kernel,
        out_shape=jax.ShapeDtypeStruct((n, n_out), F32),
        grid=(n // tm, n_out // tn),
        in_specs=[
            pl.BlockSpec((tm, d), lambda i, j: (i, 0)),
            pl.BlockSpec((1, d), lambda i, j: (0, 0)),
            pl.BlockSpec((1, 1, d), seg),
            pl.BlockSpec((1, 1, d), seg),
            pl.BlockSpec((d, tn), lambda i, j: (0, j)),
            pl.BlockSpec((1, tn), lambda i, j: (0, j)),
        ],
        out_specs=pl.BlockSpec((tm, tn), lambda i, j: (i, j)),
        scratch_shapes=[pltpu.VMEM((tm, d), BF16)],
        compiler_params=_params(("arbitrary", "arbitrary")),
        name="proj_in",
    )(x, g.reshape(1, d), scale, shift, w, b.reshape(1, n_out))


def _proj_out_kernel(a_ref, w_ref, b_ref, gate_ref, res_ref, o_ref):
    y = jnp.dot(a_ref[...], w_ref[...].astype(BF16), preferred_element_type=F32) + b_ref[...]
    o_ref[...] = res_ref[...] + gate_ref[0] * y


def _proj_out(a, w, b, gate, res):
    n, k = a.shape
    d = w.shape[1]
    tm, tn = ROW_TILE, PROJ_OUT_COLS
    return pl.pallas_call(
        _proj_out_kernel,
        out_shape=jax.ShapeDtypeStruct((n, d), F32),
        grid=(n // tm, d // tn),
        in_specs=[
            pl.BlockSpec((tm, k), lambda i, j: (i, 0)),
            pl.BlockSpec((k, tn), lambda i, j: (0, j)),
            pl.BlockSpec((1, tn), lambda i, j: (0, j)),
            pl.BlockSpec((1, 1, tn), lambda i, j: (_segment_of_tile(i, tm), 0, j)),
            pl.BlockSpec((tm, tn), lambda i, j: (i, j)),
        ],
        out_specs=pl.BlockSpec((tm, tn), lambda i, j: (i, j)),
        compiler_params=_params(("arbitrary", "arbitrary")),
        name="proj_out",
    )(a, w, b.reshape(1, d), gate, res)


def _dot_nt(a, b):
    return lax.dot_general(a, b, (((1,), (1,)), ((), ())), preferred_element_type=F32)


def _head_pair_operands(k, v, group):
    lane = lax.broadcasted_iota(jnp.int32, k.shape, 1)
    low = lane < HEAD_DIM
    k_sw = pltpu.roll(k, HEAD_DIM, 1)
    v_sw = pltpu.roll(v, HEAD_DIM, 1)
    if group == 0:
        kd = jnp.where(low, k, k_sw)
        vd = jnp.where(low, v, v_sw)
    else:
        kd = jnp.where(low, k_sw, k)
        vd = jnp.where(low, v_sw, v)
    v_lo = jnp.where(low, vd, 0.0).astype(BF16)
    v_hi = jnp.where(low, 0.0, vd).astype(BF16)
    return kd.astype(BF16), v_lo, v_hi


def _split_pair(q2):
    lane = lax.broadcasted_iota(jnp.int32, q2.shape, 1)
    low = lane < HEAD_DIM
    qs = q2 * (HEAD_DIM ** -0.5)
    return jnp.where(low, qs, 0.0).astype(BF16), jnp.where(low, 0.0, qs).astype(BF16)


def _softmax_pv(scores, values, sink):
    m = jnp.full((scores[0].shape[0], 1), sink, F32)
    for s in scores:
        m = jnp.maximum(m, jnp.max(s, axis=-1, keepdims=True))
    den = jnp.exp(sink - m)
    ps = []
    for s in scores:
        p = jnp.exp(s - m)
        den = den + jnp.sum(p, axis=-1, keepdims=True)
        ps.append(p)
    out = None
    for p, v in zip(ps, values):
        o = jnp.dot((p / den).astype(BF16), v, preferred_element_type=F32)
        out = o if out is None else out + o
    return out


def _dft_matrices(t):
    idx = np.arange(t)
    ang = 2.0 * np.pi * ((idx[:, None] * idx[None, :]) % t) / t
    m = np.concatenate([np.cos(ang), np.sin(ang)], axis=0) / math.sqrt(t)
    return jnp.asarray(m, dtype=BF16)


def _dft_channel_matrix(c):
    idx = np.arange(c)
    ang = 2.0 * np.pi * ((idx[:, None] * idx[None, :]) % c) / c
    m = np.concatenate([np.cos(ang), -np.sin(ang)], axis=0) / math.sqrt(c)
    return jnp.asarray(m, dtype=BF16)


def _fourier_group(f_g, ts_ref, cs_ref):
    t = f_g.shape[0]
    ab = jnp.dot(ts_ref[...], f_g.astype(BF16), preferred_element_type=F32)
    lhs = jnp.concatenate([ab[:t], ab[t:]], axis=1).astype(BF16)
    return jnp.dot(lhs, cs_ref[...], preferred_element_type=F32)


def _ctx_mixer_kernel(sink_ref, p_ref, ts_ref, cs_ref, o_ref):
    attn_w = N_HEADS * HEAD_DIM
    kv_w = N_KV_HEADS * HEAD_DIM
    pair_w = 2 * HEAD_DIM
    group_heads = N_HEADS // N_KV_HEADS
    k = p_ref[:, attn_w:attn_w + kv_w]
    v = p_ref[:, attn_w + kv_w:attn_w + 2 * kv_w]
    for g in range(N_KV_HEADS):
        kd, v_lo, v_hi = _head_pair_operands(k, v, g)
        for i in range(group_heads // 2):
            pair = g * (group_heads // 2) + i
            q_lo, q_hi = _split_pair(p_ref[:, pair * pair_w:(pair + 1) * pair_w])
            o = _softmax_pv([_dot_nt(q_lo, kd)], [v_lo], sink_ref[2 * pair])
            o = o + _softmax_pv([_dot_nt(q_hi, kd)], [v_hi], sink_ref[2 * pair + 1])
            o_ref[:, pair * pair_w:(pair + 1) * pair_w] = o.astype(o_ref.dtype)
    f0 = attn_w + 2 * kv_w
    fg = (D_MODEL - attn_w) // N_FOURIER_GROUPS
    for g in range(N_FOURIER_GROUPS):
        z = _fourier_group(p_ref[:, f0 + g * fg:f0 + (g + 1) * fg], ts_ref, cs_ref)
        o_ref[:, attn_w + g * fg:attn_w + (g + 1) * fg] = z.astype(o_ref.dtype)


def _ctx_mixer(p, sink):
    n, width = BATCH * SEQ, p.shape[1]
    fg = (D_MODEL - N_HEADS * HEAD_DIM) // N_FOURIER_GROUPS
    return pl.pallas_call(
        _ctx_mixer_kernel,
        out_shape=jax.ShapeDtypeStruct((n, D_MODEL), BF16),
        grid=(n // SEQ,),
        in_specs=[
            pl.BlockSpec(memory_space=pltpu.SMEM),
            pl.BlockSpec((SEQ, width), lambda b: (b, 0)),
            pl.BlockSpec((2 * SEQ, SEQ), lambda b: (0, 0)),
            pl.BlockSpec((2 * fg, fg), lambda b: (0, 0)),
        ],
        out_specs=pl.BlockSpec((SEQ, D_MODEL), lambda b: (b, 0)),
        compiler_params=_params(("arbitrary",)),
        name="ctx_mixer",
    )(sink, p, _dft_matrices(SEQ), _dft_channel_matrix(fg))


def _rope_tables():
    rows = DEC_SEQ // GRID_W
    row = np.repeat(np.arange(rows, dtype=np.float32), GRID_W)
    col = np.tile(np.arange(GRID_W, dtype=np.float32), rows)
    n_freq = HEAD_DIM // 4
    inv = jnp.asarray(ROPE_THETA, F32) ** (-jnp.arange(n_freq, dtype=F32) / n_freq)
    ang = jnp.concatenate([row[:, None] * inv, col[:, None] * inv], axis=-1)
    cos = jnp.repeat(jnp.cos(ang), 2, axis=-1)
    sin = jnp.repeat(jnp.sin(ang), 2, axis=-1)
    sign = jnp.tile(jnp.asarray([-1.0, 1.0], F32), HEAD_DIM // 2)
    return jnp.tile(cos, (1, 2)), jnp.tile(sin * sign, (1, 2))


def _rope(x, cos, sin_signed):
    lane = lax.broadcasted_iota(jnp.int32, x.shape, 1)
    width = x.shape[1]
    partner = jnp.where(lane % 2 == 0, pltpu.roll(x, width - 1, 1), pltpu.roll(x, 1, 1))
    return x * cos + partner * sin_signed


def _lat_mixer_kernel(sink_ref, p_ref, ck_ref, cv_ref, cos_ref, sin_ref, ts_ref, cs_ref, o_ref, q_scr, k_scr):
    attn_w = N_HEADS * HEAD_DIM
    kv_w = N_KV_HEADS * HEAD_DIM
    pair_w = 2 * HEAD_DIM
    group_heads = N_HEADS // N_KV_HEADS
    q_rows = 256
    cos = cos_ref[...]
    sin = sin_ref[...]
    for pair in range(N_HEADS // 2):
        q_scr[:, pair * pair_w:(pair + 1) * pair_w] = _rope(p_ref[:, pair * pair_w:(pair + 1) * pair_w], cos, sin)
    k_scr[...] = _rope(p_ref[:, attn_w:attn_w + kv_w], cos, sin)
    v = p_ref[:, attn_w + kv_w:attn_w + 2 * kv_w]
    ck = ck_ref[0]
    cv = cv_ref[0]
    k = k_scr[...]
    ops = []
    for g in range(N_KV_HEADS):
        ops.append(_head_pair_operands(k, v, g) + _head_pair_operands(ck, cv, g))

    def chunk(c, carry):
        r0 = pl.multiple_of(c * q_rows, q_rows)
        qi = r0 + lax.broadcasted_iota(jnp.int32, (q_rows, DEC_SEQ), 0)
        kj = lax.broadcasted_iota(jnp.int32, (q_rows, DEC_SEQ), 1)
        valid = jnp.abs(qi - kj) <= WINDOW
        for g in range(N_KV_HEADS):
            kd, v_lo, v_hi, ckd, cv_lo, cv_hi = ops[g]
            for i in range(group_heads // 2):
                pair = g * (group_heads // 2) + i
                q_lo, q_hi = _split_pair(q_scr[pl.ds(r0, q_rows), pair * pair_w:(pair + 1) * pair_w])
                s_lo = jnp.where(valid, _dot_nt(q_lo, kd), -jnp.inf)
                o = _softmax_pv([_dot_nt(q_lo, ckd), s_lo], [cv_lo, v_lo], sink_ref[2 * pair])
                s_hi = jnp.where(valid, _dot_nt(q_hi, kd), -jnp.inf)
                o = o + _softmax_pv([_dot_nt(q_hi, ckd), s_hi], [cv_hi, v_hi], sink_ref[2 * pair + 1])
                o_ref[pl.ds(r0, q_rows), pair * pair_w:(pair + 1) * pair_w] = o.astype(o_ref.dtype)
        return carry

    lax.fori_loop(0, DEC_SEQ // q_rows, chunk, 0)
    f0 = attn_w + 2 * kv_w
    fg = (D_MODEL - attn_w) // N_FOURIER_GROUPS
    for g in range(N_FOURIER_GROUPS):
        z = _fourier_group(p_ref[:, f0 + g * fg:f0 + (g + 1) * fg], ts_ref, cs_ref)
        o_ref[:, attn_w + g * fg:attn_w + (g + 1) * fg] = z.astype(o_ref.dtype)


def _lat_mixer(p, cache_k, cache_v, sink):
    n, width = DEC_BATCH * DEC_SEQ, p.shape[1]
    first = (BATCH * SEQ) // DEC_SEQ
    kv_w = N_KV_HEADS * HEAD_DIM
    attn_w = N_HEADS * HEAD_DIM
    fg = (D_MODEL - attn_w) // N_FOURIER_GROUPS
    cos, sin = _rope_tables()
    return pl.pallas_call(
        _lat_mixer_kernel,
        out_shape=jax.ShapeDtypeStruct((n, D_MODEL), BF16),
        grid=(n // DEC_SEQ,),
        in_specs=[
            pl.BlockSpec(memory_space=pltpu.SMEM),
            pl.BlockSpec((DEC_SEQ, width), lambda b: (first + b, 0)),
            pl.BlockSpec((1, PAST_LEN, kv_w), lambda b: (b, 0, 0)),
            pl.BlockSpec((1, PAST_LEN, kv_w), lambda b: (b, 0, 0)),
            pl.BlockSpec((DEC_SEQ, 2 * HEAD_DIM), lambda b: (0, 0)),
            pl.BlockSpec((DEC_SEQ, 2 * HEAD_DIM), lambda b: (0, 0)),
            pl.BlockSpec((2 * DEC_SEQ, DEC_SEQ), lambda b: (0, 0)),
            pl.BlockSpec((2 * fg, fg), lambda b: (0, 0)),
        ],
        out_specs=pl.BlockSpec((DEC_SEQ, D_MODEL), lambda b: (b, 0)),
        scratch_shapes=[pltpu.VMEM((DEC_SEQ, attn_w), F32), pltpu.VMEM((DEC_SEQ, kv_w), F32)],
        compiler_params=_params(("arbitrary",)),
        name="lat_mixer",
    )(sink, p, cache_k, cache_v, cos, sin, _dft_matrices(DEC_SEQ), _dft_channel_matrix(fg))


def _rnn_core_kernel(xr_ref, gr_ref, cw_ref, cb_ref, wa_ref, ba_ref, wx_ref, bx_ref, lam_ref, h0_ref,
                     y_ref, st_ref, a_scr, b_scr, h_scr, *, n_seq, seq_len):
    rows = n_seq * seq_len
    n_lb = xr_ref.shape[1] // LANES
    xr = xr_ref[...]
    t_idx = lax.broadcasted_iota(jnp.int32, (rows, 1), 0) % seq_len
    xc = jnp.broadcast_to(cb_ref[...], xr.shape)
    for tap in range(CONV_WIDTH):
        off = tap - CONV_LEFT
        shifted = xr if off == 0 else pltpu.roll(xr, (-off) % rows, 0)
        valid = (t_idx + off >= 0) & (t_idx + off < seq_len)
        xc = xc + jnp.where(valid, shifted, 0.0) * cw_ref[tap:tap + 1, :]
    xcb = xc.astype(BF16)
    for d in range(2):
        r = jax.nn.sigmoid(jnp.dot(xcb, wa_ref[d, 0].astype(BF16), preferred_element_type=F32) + ba_ref[d:d + 1, :])
        gi = jax.nn.sigmoid(jnp.dot(xcb, wx_ref[d, 0].astype(BF16), preferred_element_type=F32) + bx_ref[d:d + 1, :])
        neg_lam = -lam_ref[d:d + 1, :]
        softplus = jnp.maximum(neg_lam, 0.0) + jnp.log1p(jnp.exp(-jnp.abs(neg_lam)))
        log_a = -RG_C * r * softplus
        a = jnp.exp(log_a)
        th = jnp.tanh(log_a)
        b = jnp.sqrt(-2.0 * th / (1.0 - th)) * (gi * xc)
        for lb in range(n_lb):
            a_scr[d * n_lb + lb] = a[:, lb * LANES:(lb + 1) * LANES]
            b_scr[d * n_lb + lb] = b[:, lb * LANES:(lb + 1) * LANES]

    def time_rows(t):
        return pl.ds(t, n_seq, stride=seq_len)

    def lane_block(ref, k, lb):
        return ref[:, k, lb * LANES:(lb + 1) * LANES]

    def fwd(t, hs):
        out = []
        for lb in range(n_lb):
            h = a_scr[lb, time_rows(t), :] * hs[lb] + b_scr[lb, time_rows(t), :]
            h_scr[lb, time_rows(t), :] = h
            out.append(h)
        return tuple(out)

    hs = lax.fori_loop(0, seq_len, fwd, tuple(lane_block(h0_ref, 0, lb) for lb in range(n_lb)))
    for lb in range(n_lb):
        st_ref[:, 0, lb * LANES:(lb + 1) * LANES] = hs[lb]

    def bwd(i, hs):
        t = seq_len - 1 - i
        out = []
        for lb in range(n_lb):
            h = a_scr[n_lb + lb, time_rows(t), :] * hs[lb] + b_scr[n_lb + lb, time_rows(t), :]
            h_scr[lb, time_rows(t), :] = h_scr[lb, time_rows(t), :] + h
            out.append(h)
        return tuple(out)

    hs = lax.fori_loop(0, seq_len, bwd, tuple(lane_block(h0_ref, 1, lb) for lb in range(n_lb)))
    for lb in range(n_lb):
        st_ref[:, 1, lb * LANES:(lb + 1) * LANES] = hs[lb]
    for lb in range(n_lb):
        gr = gr_ref[:, lb * LANES:(lb + 1) * LANES]
        gelu = 0.5 * gr * (1.0 + jnp.tanh(math.sqrt(2.0 / math.pi) * (gr + 0.044715 * (gr * gr * gr))))
        y_ref[:, lb * LANES:(lb + 1) * LANES] = (gelu * h_scr[lb]).astype(y_ref.dtype)


def _rnn_core(p, first_row, n, h0, conv_w, conv_b, w_a, b_a, w_x, b_x, lam, n_seq, seq_len):
    d_rnn = p.shape[1] // 2
    cb = d_rnn // N_RNN_BLOCKS
    rows = n_seq * seq_len
    n_batch = n // seq_len
    first = first_row // rows
    kern = functools.partial(_rnn_core_kernel, n_seq=n_seq, seq_len=seq_len)
    return pl.pallas_call(
        kern,
        out_shape=(jax.ShapeDtypeStruct((n, d_rnn), BF16), jax.ShapeDtypeStruct((n_batch, 2, d_rnn), F32)),
        grid=(n // rows, N_RNN_BLOCKS),
        in_specs=[
            pl.BlockSpec((rows, cb), lambda i, c: (first + i, c)),
            pl.BlockSpec((rows, cb), lambda i, c: (first + i, N_RNN_BLOCKS + c)),
            pl.BlockSpec((CONV_WIDTH, cb), lambda i, c: (0, c)),
            pl.BlockSpec((1, cb), lambda i, c: (0, c)),
            pl.BlockSpec((2, 1, cb, cb), lambda i, c: (0, c, 0, 0)),
            pl.BlockSpec((2, cb), lambda i, c: (0, c)),
            pl.BlockSpec((2, 1, cb, cb), lambda i, c: (0, c, 0, 0)),
            pl.BlockSpec((2, cb), lambda i, c: (0, c)),
            pl.BlockSpec((2, cb), lambda i, c: (0, c)),
            pl.BlockSpec((n_seq, 2, cb), lambda i, c: (i, 0, c)),
        ],
        out_specs=(
            pl.BlockSpec((rows, cb), lambda i, c: (i, c)),
            pl.BlockSpec((n_seq, 2, cb), lambda i, c: (i, 0, c)),
        ),
        scratch_shapes=[pltpu.VMEM((2 * cb // LANES, rows, LANES), F32), pltpu.VMEM((2 * cb // LANES, rows, LANES), F32),
                        pltpu.VMEM((cb // LANES, rows, LANES), F32)],
        compiler_params=_params(("arbitrary", "arbitrary")),
        name="rnn_core",
    )(p, p, conv_w, conv_b.reshape(1, d_rnn), w_a, b_a, w_x, b_x, lam, h0)


def _pack_bf16_pairs(h):
    half = h.shape[1] // 2
    bits = lax.bitcast_convert_type(h.astype(BF16).astype(F32), jnp.uint32)
    return (bits[:, :half] >> 16) | (bits[:, half:] & jnp.uint32(0xFFFF0000))


def _unpack_bf16_pairs(w):
    lo = lax.bitcast_convert_type(w << 16, F32)
    hi = lax.bitcast_convert_type(w & jnp.uint32(0xFFFF0000), F32)
    return jnp.concatenate([lo, hi], axis=1).astype(BF16)


def _router_kernel(x_ref, g_ref, sc_ref, sh_ref, w_ref, b_ref, h_ref, idx_ref, gate_ref, rank_ref, cnt_ref, run_ref):
    tm = x_ref.shape[0]

    @pl.when(pl.program_id(0) == 0)
    def _():
        run_ref[...] = jnp.zeros_like(run_ref)

    h = _rms_modulate(x_ref[...], g_ref[...], sc_ref[0], sh_ref[0])
    h_ref[...] = _pack_bf16_pairs(h)
    logits =jnp.dot(h, w_ref[...], preferred_element_type=F32, precision=lax.Precision.HIGHEST) + b_ref[...]
    lane = lax.broadcasted_iota(jnp.int32, logits.shape, 1).astype(F32)
    col = lax.broadcasted_iota(jnp.int32, (tm, TOP_K), 1)
    chosen = jnp.zeros(logits.shape, F32)
    top_v, top_i, hits = [], [], []
    work = logits
    for _ in range(TOP_K):
        m = jnp.max(work, axis=-1, keepdims=True)
        first = jnp.min(jnp.where(work == m, lane, float(N_EXPERTS)), axis=-1, keepdims=True)
        hit = lane == first
        work = jnp.where(hit, -jnp.inf, work)
        chosen = jnp.where(hit, 1.0, chosen)
        top_v.append(m)
        top_i.append(first)
        hits.append(hit)
    exps = [jnp.exp(v - top_v[0]) for v in top_v]
    den = exps[0]
    for e in exps[1:]:
        den = den + e
    ri = lax.broadcasted_iota(jnp.int32, (tm, tm), 0)
    ci = lax.broadcasted_iota(jnp.int32, (tm, tm), 1)
    before = (ci < ri).astype(BF16)
    rank_all = jnp.dot(before, chosen.astype(BF16), preferred_element_type=F32) + run_ref[...]
    idx_out = jnp.zeros((tm, TOP_K), F32)
    gate_out = jnp.zeros((tm, TOP_K), F32)
    rank_out = jnp.zeros((tm, TOP_K), F32)
    for k in range(TOP_K):
        rk = jnp.sum(jnp.where(hits[k], rank_all, 0.0), axis=-1, keepdims=True)
        idx_out = jnp.where(col == k, top_i[k], idx_out)
        gate_out = jnp.where(col == k, exps[k] / den, gate_out)
        rank_out = jnp.where(col == k, rk, rank_out)
    idx_ref[...] = idx_out.astype(jnp.int32)
    gate_ref[...] = gate_out
    rank_ref[...] = rank_out.astype(jnp.int32)
    run_ref[...] = run_ref[...] + jnp.sum(chosen, axis=0, keepdims=True)
    cnt_ref[...] = run_ref[...]


def _router(x, g, scale, shift, w, b):
    n, d = x.shape
    e = w.shape[1]
    tm = ROW_TILE
    seg = lambda i: (_segment_of_tile(i, tm), 0, 0)
    small = pl.BlockSpec((tm, TOP_K), lambda i: (i, 0))
    return pl.pallas_call(
        _router_kernel,
        out_shape=(
            jax.ShapeDtypeStruct((n, d // 2), jnp.uint32),
            jax.ShapeDtypeStruct((n, TOP_K), jnp.int32),
            jax.ShapeDtypeStruct((n, TOP_K), F32),
            jax.ShapeDtypeStruct((n, TOP_K), jnp.int32),
            jax.ShapeDtypeStruct((1, e), F32),
        ),
        grid=(n // tm,),
        in_specs=[
            pl.BlockSpec((tm, d), lambda i: (i, 0)),
            pl.BlockSpec((1, d), lambda i: (0, 0)),
            pl.BlockSpec((1, 1, d), seg),
            pl.BlockSpec((1, 1, d), seg),
            pl.BlockSpec((d, e), lambda i: (0, 0)),
            pl.BlockSpec((1, e), lambda i: (0, 0)),
        ],
        out_specs=(pl.BlockSpec((tm, d // 2), lambda i: (i, 0)), small, small, small,
                   pl.BlockSpec((1, e), lambda i: (0, 0))),
        scratch_shapes=[pltpu.VMEM((1, e), F32)],
        compiler_params=_params(("arbitrary",)),
        name="router",
    )(x, g.reshape(1, d), scale, shift, w, b.reshape(1, e))


def _moe_kernel(te_ref, tv_ref, src_ref, h_hbm, wg_ref, wu_ref, wd_ref, bg_ref, bu_ref, bd_ref, o_ref,
                gather_buf, gather_sem, xs_scr, wg_scr, wu_scr, wd_scr):
    t = pl.program_id(0)
    j = pl.program_id(1)
    tile_rows = o_ref.shape[0]
    n_sub_total = tile_rows // MOE_SUB_ROWS

    def sub_blocks(tile):
        return (tv_ref[tile] + MOE_SUB_ROWS - 1) // MOE_SUB_ROWS

    def row_copy(token, slot, row):
        return pltpu.make_async_copy(h_hbm.at[pl.ds(token, 1), :], gather_buf.at[slot, pl.ds(row, 1), :],
                                     gather_sem.at[slot])

    def start_gather(tile, slot):
        def issue(r, carry):
            row_copy(src_ref[tile * tile_rows + r], slot, r).start()
            return carry

        lax.fori_loop(0, sub_blocks(tile) * MOE_SUB_ROWS, issue, 0)

    def wait_gather(tile, slot):
        def wait_sub(s, carry):
            pltpu.make_async_copy(h_hbm.at[pl.ds(0, MOE_SUB_ROWS), :],
                                  gather_buf.at[slot, pl.ds(0, MOE_SUB_ROWS), :], gather_sem.at[slot]).wait()
            return carry

        lax.fori_loop(0, sub_blocks(tile), wait_sub, 0)

    n_valid = tv_ref[t]
    n_sub = sub_blocks(t)
    slot = t % 2

    @pl.when((j == 0) & (n_valid > 0))
    def _():
        @pl.when(t == 0)
        def _():
            start_gather(0, 0)

        @pl.when(t + 1 < pl.num_programs(0))
        def _():
            start_gather(t + 1, 1 - slot)

        wait_gather(t, slot)

        def unpack(r, carry):
            rows = pl.ds(pl.multiple_of(r * MOE_SUB_ROWS, MOE_SUB_ROWS), MOE_SUB_ROWS)
            xs_scr[rows, :] = _unpack_bf16_pairs(gather_buf[slot, rows, :])
            return carry

        lax.fori_loop(0, n_sub, unpack, 0)

    @pl.when(n_valid > 0)
    def _():
        wg_scr[...] = wg_ref[0, 0].astype(BF16)
        wu_scr[...] = wu_ref[0, 0].astype(BF16)
        wd_scr[...] = wd_ref[0, 0].astype(BF16)

        def body(r, carry):
            rows = pl.ds(pl.multiple_of(r * MOE_SUB_ROWS, MOE_SUB_ROWS), MOE_SUB_ROWS)
            x = xs_scr[rows, :]
            g = jnp.dot(x, wg_scr[...], preferred_element_type=F32) + bg_ref[0]
            u = jnp.dot(x, wu_scr[...], preferred_element_type=F32) + bu_ref[0]
            g = jnp.minimum(g, SWIGLU_LIMIT)
            u = jnp.clip(u, -SWIGLU_LIMIT, SWIGLU_LIMIT)
            act = (u + 1.0) * (g * jax.nn.sigmoid(SWIGLU_ALPHA * g))
            y = jnp.dot(act.astype(BF16), wd_scr[...], preferred_element_type=F32)

            @pl.when(j == 0)
            def _():
                o_ref[rows, :] = y + bd_ref[0]

            @pl.when(j > 0)
            def _():
                o_ref[rows, :] = o_ref[rows, :] + y

            return carry

        lax.fori_loop(0, n_sub, body, 0)

    @pl.when(j == 0)
    def _():
        def clear(r, carry):
            rows = pl.ds(pl.multiple_of(r * MOE_SUB_ROWS, MOE_SUB_ROWS), MOE_SUB_ROWS)
            o_ref[rows, :] = jnp.zeros((MOE_SUB_ROWS, o_ref.shape[1]), o_ref.dtype)
            return carry

        lax.fori_loop(n_sub, n_sub_total, clear, 0)


def _moe_experts(h_packed, src, tile_expert, tile_valid, layer, w_gate, b_gate, w_up, b_up, w_down, b_down):
    p_rows = src.shape[0]
    _, e, d, hdim = w_gate.shape
    tm, th = MOE_TILE_ROWS, MOE_HIDDEN_BLOCK
    n_tiles = p_rows // tm
    n_j = hdim // th

    def jj(t, j, tv):
        return jnp.where(tv[t] > 0, j, n_j - 1)

    grid_spec = pltpu.PrefetchScalarGridSpec(
        num_scalar_prefetch=3,
        grid=(n_tiles, n_j),
        in_specs=[
            pl.BlockSpec(memory_space=pl.ANY),
            pl.BlockSpec((1, 1, d, th), lambda t, j, te, tv, src: (layer, te[t], 0, jj(t, j, tv))),
            pl.BlockSpec((1, 1, d, th), lambda t, j, te, tv, src: (layer, te[t], 0, jj(t, j, tv))),
            pl.BlockSpec((1, 1, th, d), lambda t, j, te, tv, src: (layer, te[t], jj(t, j, tv), 0)),
            pl.BlockSpec((1, 1, th), lambda t, j, te, tv, src: (te[t], 0, jj(t, j, tv))),
            pl.BlockSpec((1, 1, th), lambda t, j, te, tv, src: (te[t], 0, jj(t, j, tv))),
            pl.BlockSpec((1, 1, d), lambda t, j, te, tv, src: (te[t], 0, 0)),
        ],
        out_specs=pl.BlockSpec((tm, d), lambda t, j, te, tv, src: (t, 0)),
        scratch_shapes=[
            pltpu.VMEM((2, tm, d // 2), jnp.uint32),
            pltpu.SemaphoreType.DMA((2,)),
            pltpu.VMEM((tm, d), BF16),
            pltpu.VMEM((d, th), BF16),
            pltpu.VMEM((d, th), BF16),
            pltpu.VMEM((th, d), BF16),
        ],
    )
    return pl.pallas_call(
        _moe_kernel,
        out_shape=jax.ShapeDtypeStruct((p_rows, d), F32),
        grid_spec=grid_spec,
        compiler_params=_params(("arbitrary", "arbitrary")),
        name="moe_experts",
    )(tile_expert, tile_valid, src, h_packed, w_gate, w_up, w_down,
      b_gate[layer].reshape(e, 1, hdim), b_up[layer].reshape(e, 1, hdim), b_down[layer].reshape(e, 1, d))


def _combine_kernel(pos_ref, ys_hbm, gates_ref, x_ref, gate2_ref, norm_ref, o_ref, rows_buf, rows_sem, *, final_norm):
    i = pl.program_id(0)
    tm = x_ref.shape[0]
    slot = i % 2

    def start_gather(tile, slot):
        def issue(r, carry):
            for k in range(TOP_K):
                row = pos_ref[(tile * tm + r) * TOP_K + k]
                pltpu.make_async_copy(ys_hbm.at[pl.ds(row, 1), :], rows_buf.at[slot, pl.ds(k * tm + r, 1), :],
                                      rows_sem.at[slot]).start()
            return carry

        lax.fori_loop(0, tm, issue, 0)

    @pl.when(i == 0)
    def _():
        start_gather(0, 0)

    @pl.when(i + 1 < pl.num_programs(0))
    def _():
        start_gather(i + 1, 1 - slot)

    pltpu.make_async_copy(ys_hbm.at[pl.ds(0, TOP_K * tm), :], rows_buf.at[slot], rows_sem.at[slot]).wait()
    gates = gates_ref[...]
    moe = gates[:, 0:1] * rows_buf[slot, pl.ds(0, tm), :]
    for k in range(1, TOP_K):
        moe = moe + gates[:, k:k + 1] * rows_buf[slot, pl.ds(k * tm, tm), :]
    x = x_ref[...] + gate2_ref[0] * moe
    if final_norm:
        ms = jnp.mean(x * x, axis=-1, keepdims=True)
        x = x * lax.rsqrt(ms + EPS) * norm_ref[...]
    o_ref[...] = x


def _combine(ys, pos, gates, x, gate2, norm_g, final_norm):
    n, d = x.shape
    tm = COMBINE_ROWS
    grid_spec = pltpu.PrefetchScalarGridSpec(
        num_scalar_prefetch=1,
        grid=(n // tm,),
        in_specs=[
            pl.BlockSpec(memory_space=pl.ANY),
            pl.BlockSpec((tm, TOP_K), lambda i, pos: (i, 0)),
            pl.BlockSpec((tm, d), lambda i, pos: (i, 0)),
            pl.BlockSpec((1, 1, d), lambda i, pos: (_segment_of_tile(i, tm), 0, 0)),
            pl.BlockSpec((1, d), lambda i, pos: (0, 0)),
        ],
        out_specs=pl.BlockSpec((tm, d), lambda i, pos: (i, 0)),
        scratch_shapes=[pltpu.VMEM((2, TOP_K * tm, d), F32), pltpu.SemaphoreType.DMA((2,))],
    )
    return pl.pallas_call(
        functools.partial(_combine_kernel, final_norm=final_norm),
        out_shape=jax.ShapeDtypeStruct((n, d), F32),
        grid_spec=grid_spec,
        compiler_params=_params(("arbitrary",)),
        name="moe_combine",
    )(pos.reshape(-1), ys, gates, x, gate2, norm_g.reshape(1, d))


def _moe_layer(x, g, scale, shift, gate2, layer, w_router, b_router, w_gate, b_gate, w_up, b_up, w_down, b_down,
               norm_g, final_norm):
    n, d = x.shape
    e = w_router.shape[-1]
    tm = MOE_TILE_ROWS
    h, top_i, gates, rank, counts = _router(x, g, scale, shift, w_router[layer], b_router[layer])
    counts = counts[0].astype(jnp.int32)
    padded = ((counts + tm - 1) // tm) * tm
    ends = jnp.cumsum(padded)
    starts = ends - padded
    n_tiles = (n * TOP_K) // tm + e
    p_rows = n_tiles * tm
    tile_start = jnp.arange(n_tiles, dtype=jnp.int32) * tm
    n_used = ends[-1] // tm
    tile_expert = jnp.minimum(jnp.searchsorted(ends, tile_start, side="right"), e - 1).astype(jnp.int32)
    tile_valid = jnp.clip(counts[tile_expert] - (tile_start - starts[tile_expert]), 0, tm)
    tile_valid = jnp.where(tile_start < ends[-1], tile_valid, 0).astype(jnp.int32)
    last = jnp.maximum(n_used - 1, 0)
    tile_expert =jnp.where(tile_start < ends[-1], tile_expert, tile_expert[last]).astype(jnp.int32)
    pos = starts[top_i] + rank
    token = jnp.broadcast_to(jnp.arange(n, dtype=jnp.int32)[:, None], (n, TOP_K))
    src = jnp.zeros((p_rows,), jnp.int32).at[pos.reshape(-1)].set(token.reshape(-1))
    ys = _moe_experts(h, src, tile_expert, tile_valid, layer, w_gate, b_gate, w_up, b_up, w_down, b_down)
    return _combine(ys, pos, gates, x, gate2, norm_g, final_norm)


def kernel(x_prompt, x_sample, cache_k, cache_v, state_rglru, c, c_ctx, norm_mix, norm_ffn, w_mod, b_mod,
           attn_w_in, attn_b_in, attn_w_out, attn_b_out, attn_sink,
           rnn_w_in, rnn_b_in, rnn_conv_w, rnn_conv_b, rnn_w_a, rnn_b_a, rnn_w_x, rnn_b_x,
           rnn_lambda, rnn_w_out, rnn_b_out,
           moe_w_router, moe_b_router, moe_w_gate, moe_b_gate, moe_w_up, moe_b_up,
           moe_w_down, moe_b_down, final_norm):
    d = D_MODEL
    n_ctx = BATCH * SEQ
    attn_w = N_HEADS * HEAD_DIM
    kv_w = N_KV_HEADS * HEAD_DIM
    x = jnp.concatenate([x_prompt.reshape(n_ctx, d), x_sample.reshape(DEC_BATCH * DEC_SEQ, d)], axis=0)
    cond = jnp.concatenate([c_ctx[None, :], c, jnp.zeros((SUBLANES - 1 - DEC_BATCH, d), F32)], axis=0)
    mods = _modulation(cond, w_mod, b_mod)

    def mod(l, k):
        return mods[l, :, k * d:(k + 1) * d].reshape(SUBLANES, 1, d)

    new_k, new_v, new_s = [], [], []
    for l in range(DEPTH):
        j = l // 2
        sh1, sc1, g1, sh2, sc2, g2 = [mod(l, k) for k in range(N_MOD)]
        if l % 2 == 0:
            p = _proj_in(x, norm_mix[l], sc1, sh1, attn_w_in[j], attn_b_in[j], tn=PROJ_IN_ATTN_COLS)
            new_k.append(p[:n_ctx, attn_w:attn_w + kv_w].reshape(BATCH, SEQ, N_KV_HEADS, HEAD_DIM))
            new_v.append(p[:n_ctx, attn_w + kv_w:attn_w + 2 * kv_w].reshape(BATCH, SEQ, N_KV_HEADS, HEAD_DIM))
            mix_ctx = _ctx_mixer(p, attn_sink[j])
            mix_lat = _lat_mixer(p, cache_k[:, j].reshape(DEC_BATCH, PAST_LEN, kv_w),
                                 cache_v[:, j].reshape(DEC_BATCH, PAST_LEN, kv_w), attn_sink[j])
            mix = jnp.concatenate([mix_ctx, mix_lat], axis=0)
            x = _proj_out(mix, attn_w_out[j], attn_b_out[j], g1, x)
        else:
            p = _proj_in(x, norm_mix[l], sc1, sh1, rnn_w_in[j], rnn_b_in[j], tn=PROJ_IN_RNN_COLS)
            args = (rnn_conv_w[j], rnn_conv_b[j], rnn_w_a[j], rnn_b_a[j], rnn_w_x[j], rnn_b_x[j], rnn_lambda[j])
            h0_ctx = jnp.zeros((BATCH, 2, d), F32)
            mix_ctx, st = _rnn_core(p, 0, n_ctx, h0_ctx, *args, n_seq=SUBLANES, seq_len=SEQ)
            mix_lat, _ = _rnn_core(p, n_ctx, DEC_BATCH * DEC_SEQ, state_rglru[:, j], *args,
                                   n_seq=DEC_BATCH, seq_len=DEC_SEQ)
            new_s.append(st)
            mix = jnp.concatenate([mix_ctx, mix_lat], axis=0)
            x = _proj_out(mix, rnn_w_out[j], rnn_b_out[j], g1, x)
        x = _moe_layer(x, norm_ffn[l], sc2, sh2, g2, l, moe_w_router, moe_b_router,
                       moe_w_gate, moe_b_gate, moe_w_up, moe_b_up, moe_w_down, moe_b_down,
                       final_norm, final_norm=(l == DEPTH - 1))
    y = x
    y_prompt = y[:n_ctx].reshape(BATCH, SEQ, d)
    y_sample = y[n_ctx:].reshape(DEC_BATCH, DEC_SEQ, d)
    return (y_prompt, y_sample, jnp.stack(new_k, axis=1), jnp.stack(new_v, axis=1), jnp.stack(new_s, axis=1))
```

```python
import functools
import math

import numpy as np
import jax
import jax.numpy as jnp
from jax import lax
from jax.experimental import pallas as pl
from jax.experimental.pallas import tpu as pltpu

D_MODEL = 2048
BATCH = 32
SEQ = 256
DEPTH = 2
DEC_BATCH = 2
DEC_SEQ = 1024
PAST_LEN = 256
GRID_W = 64
N_HEADS = 16
N_KV_HEADS = 2
HEAD_DIM = 64
WINDOW = 128
ROPE_THETA = 10000.0
N_FOURIER_GROUPS = 4
N_RNN_BLOCKS = 8
CONV_WIDTH = 4
CONV_LEFT = 2
RG_C = 8.0
N_EXPERTS = 32
TOP_K = 4
SWIGLU_LIMIT = 7.0
SWIGLU_ALPHA = 1.702
N_MOD = 6
EPS = 1e-6

LANES = 128
SUBLANES = 8
VMEM_LIMIT_BYTES = 56 * 1024 * 1024

MOE_TILE_ROWS = 1536
MOE_SUB_ROWS = 256
MOE_HIDDEN_BLOCK = 256
MOE_OUT_BLOCK = 256
GATHER_UNROLL = 8
ROW_TILE = 512
COMBINE_ROWS = 128
PROJ_OUT_COLS = 512
PROJ_IN_ATTN_COLS = 768
PROJ_IN_RNN_COLS = 1024

F32 = jnp.float32
BF16 = jnp.bfloat16


def _params(semantics):
    return pltpu.CompilerParams(dimension_semantics=semantics, vmem_limit_bytes=VMEM_LIMIT_BYTES)


def _segment_of_tile(i, tile_rows):
    n_ctx_tiles = (BATCH * SEQ) // tile_rows
    tiles_per_latent = DEC_SEQ // tile_rows
    return jnp.where(i < n_ctx_tiles, 0, 1 + (i - n_ctx_tiles) // tiles_per_latent)


def _rms_modulate(x, g, scale, shift):
    ms = jnp.mean(x * x, axis=-1, keepdims=True)
    return (x * lax.rsqrt(ms + EPS) * g) * (1.0 + scale) + shift


def _modulation_kernel(c_ref, w_ref, b_ref, o_ref):
    c = c_ref[...]
    s = (c * jax.nn.sigmoid(c)).astype(BF16)
    o_ref[0] = jnp.dot(s, w_ref[0].astype(BF16), preferred_element_type=F32) + b_ref[0]


def _modulation(cond, w_mod, b_mod):
    d = D_MODEL
    tn = 512
    n_out = N_MOD * d
    return pl.pallas_call(
        _modulation_kernel,
        out_shape=jax.ShapeDtypeStruct((DEPTH, SUBLANES, n_out), F32),
        grid=(DEPTH, n_out // tn),
        in_specs=[
            pl.BlockSpec((SUBLANES, d), lambda l, j: (0, 0)),
            pl.BlockSpec((1, d, tn), lambda l, j: (l, 0, j)),
            pl.BlockSpec((1, 1, tn), lambda l, j: (l, 0, j)),
        ],
        out_specs=pl.BlockSpec((1, SUBLANES, tn), lambda l, j: (l, 0, j)),
        compiler_params=_params(("arbitrary", "arbitrary")),
        name="modulation",
    )(cond, w_mod, b_mod.reshape(DEPTH, 1, n_out))


def _proj_in_kernel(x_ref, g_ref, sc_ref, sh_ref, w_ref, b_ref, o_ref, h_ref):
    @pl.when(pl.program_id(1) == 0)
    def _():
        h_ref[...] = _rms_modulate(x_ref[...], g_ref[...], sc_ref[0], sh_ref[0]).astype(BF16)

    o_ref[...] = jnp.dot(h_ref[...], w_ref[...].astype(BF16), preferred_element_type=F32) + b_ref[...]


def _proj_in(x, g, scale, shift, w, b, tn):
    n, d = x.shape
    n_out = w.shape[1]
    tm = ROW_TILE
    seg = lambda i, j: (_segment_of_tile(i, tm), 0, 0)
    return pl.pallas_call(
        _proj_in_kernel,
        out_shape=jax.ShapeDtypeStruct((n, n_out), F32),
        grid=(n // tm, n_out // tn),
        in_specs=[
            pl.BlockSpec((tm, d), lambda i, j: (i, 0)),
            pl.BlockSpec((1, d), lambda i, j: (0, 0)),
            pl.BlockSpec((1, 1, d), seg),
            pl.BlockSpec((1, 1, d), seg),
            pl.BlockSpec((d, tn), lambda i, j: (0, j)),
            pl.BlockSpec((1, tn), lambda i, j: (0, j)),
        ],
        out_specs=pl.BlockSpec((tm, tn), lambda i, j: (i, j)),
        scratch_shapes=[pltpu.VMEM((tm, d), BF16)],
        compiler_params=_params(("arbitrary", "arbitrary")),
        name="proj_in",
    )(x, g.reshape(1, d), scale, shift, w, b.reshape(1, n_out))


def _proj_out_kernel(a_ref, w_ref, b_ref, gate_ref, res_ref, o_ref):
    y = jnp.dot(a_ref[...], w_ref[...].astype(BF16), preferred_element_type=F32) + b_ref[...]
    o_ref[...] = res_ref[...] + gate_ref[0] * y


def _proj_out(a, w, b, gate, res):
    n, k = a.shape
    d = w.shape[1]
    tm, tn = ROW_TILE, PROJ_OUT_COLS
    return pl.pallas_call(
        _proj_out_kernel,
        out_shape=jax.ShapeDtypeStruct((n, d), F32),
        grid=(n // tm, d // tn),
        in_specs=[
            pl.BlockSpec((tm, k), lambda i, j: (i, 0)),
            pl.BlockSpec((k, tn), lambda i, j: (0, j)),
            pl.BlockSpec((1, tn), lambda i, j: (0, j)),
            pl.BlockSpec((1, 1, tn), lambda i, j: (_segment_of_tile(i, tm), 0, j)),
            pl.BlockSpec((tm, tn), lambda i, j: (i, j)),
        ],
        out_specs=pl.BlockSpec((tm, tn), lambda i, j: (i, j)),
        compiler_params=_params(("arbitrary", "arbitrary")),
        name="proj_out",
    )(a, w, b.reshape(1, d), gate, res)


def _dot_nt(a, b):
    return lax.dot_general(a, b, (((1,), (1,)), ((), ())), preferred_element_type=F32)


def _head_pair_operands(k, v, group):
    lane = lax.broadcasted_iota(jnp.int32, k.shape, 1)
    low = lane < HEAD_DIM
    k_sw = pltpu.roll(k, HEAD_DIM, 1)
    v_sw = pltpu.roll(v, HEAD_DIM, 1)
    if group == 0:
        kd = jnp.where(low, k, k_sw)
        vd = jnp.where(low, v, v_sw)
    else:
        kd = jnp.where(low, k_sw, k)
        vd = jnp.where(low, v_sw, v)
    v_lo = jnp.where(low, vd, 0.0).astype(BF16)
    v_hi = jnp.where(low, 0.0, vd).astype(BF16)
    return kd.astype(BF16), v_lo, v_hi


def _split_pair(q2):
    lane = lax.broadcasted_iota(jnp.int32, q2.shape, 1)
    low = lane < HEAD_DIM
    qs = q2 * (HEAD_DIM ** -0.5)
    return jnp.where(low, qs, 0.0).astype(BF16), jnp.where(low, 0.0, qs).astype(BF16)


def _softmax_pv(scores, values, sink):
    m = jnp.full((scores[0].shape[0], 1), sink, F32)
    for s in scores:
        m = jnp.maximum(m, jnp.max(s, axis=-1, keepdims=True))
    den = jnp.exp(sink - m)
    ps = []
    for s in scores:
        p = jnp.exp(s - m)
        den = den + jnp.sum(p, axis=-1, keepdims=True)
        ps.append(p)
    out = None
    for p, v in zip(ps, values):
        o = jnp.dot((p / den).astype(BF16), v, preferred_element_type=F32)
        out = o if out is None else out + o
    return out


def _dft_matrices(t):
    idx = np.arange(t)
    ang = 2.0 * np.pi * ((idx[:, None] * idx[None, :]) % t) / t
    m = np.concatenate([np.cos(ang), np.sin(ang)], axis=0) / math.sqrt(t)
    return jnp.asarray(m, dtype=BF16)


def _dft_channel_matrix(c):
    idx = np.arange(c)
    ang = 2.0 * np.pi * ((idx[:, None] * idx[None, :]) % c) / c
    m = np.concatenate([np.cos(ang), -np.sin(ang)], axis=0) / math.sqrt(c)
    return jnp.asarray(m, dtype=BF16)


def _fourier_group(f_g, ts_ref, cs_ref):
    t = f_g.shape[0]
    ab = jnp.dot(ts_ref[...], f_g.astype(BF16), preferred_element_type=F32)
    lhs = jnp.concatenate([ab[:t], ab[t:]], axis=1).astype(BF16)
    return jnp.dot(lhs, cs_ref[...], preferred_element_type=F32)


def _ctx_mixer_kernel(sink_ref, p_ref, ts_ref, cs_ref, o_ref):
    attn_w = N_HEADS * HEAD_DIM
    kv_w = N_KV_HEADS * HEAD_DIM
    pair_w = 2 * HEAD_DIM
    group_heads = N_HEADS // N_KV_HEADS
    k = p_ref[:, attn_w:attn_w + kv_w]
    v = p_ref[:, attn_w + kv_w:attn_w + 2 * kv_w]
    for g in range(N_KV_HEADS):
        kd, v_lo, v_hi = _head_pair_operands(k, v, g)
        for i in range(group_heads // 2):
            pair = g * (group_heads // 2) + i
            q_lo, q_hi = _split_pair(p_ref[:, pair * pair_w:(pair + 1) * pair_w])
            o = _softmax_pv([_dot_nt(q_lo, kd)], [v_lo], sink_ref[2 * pair])
            o = o + _softmax_pv([_dot_nt(q_hi, kd)], [v_hi], sink_ref[2 * pair + 1])
            o_ref[:, pair * pair_w:(pair + 1) * pair_w] = o.astype(o_ref.dtype)
    f0 = attn_w + 2 * kv_w
    fg = (D_MODEL - attn_w) // N_FOURIER_GROUPS
    for g in range(N_FOURIER_GROUPS):
        z = _fourier_group(p_ref[:, f0 + g * fg:f0 + (g + 1) * fg], ts_ref, cs_ref)
        o_ref[:, attn_w + g * fg:attn_w + (g + 1) * fg] = z.astype(o_ref.dtype)


def _ctx_mixer(p, sink):
    n, width = BATCH * SEQ, p.shape[1]
    fg = (D_MODEL - N_HEADS * HEAD_DIM) // N_FOURIER_GROUPS
    return pl.pallas_call(
        _ctx_mixer_kernel,
        out_shape=jax.ShapeDtypeStruct((n, D_MODEL), BF16),
        grid=(n // SEQ,),
        in_specs=[
            pl.BlockSpec(memory_space=pltpu.SMEM),
            pl.BlockSpec((SEQ, width), lambda b: (b, 0)),
            pl.BlockSpec((2 * SEQ, SEQ), lambda b: (0, 0)),
            pl.BlockSpec((2 * fg, fg), lambda b: (0, 0)),
        ],
        out_specs=pl.BlockSpec((SEQ, D_MODEL), lambda b: (b, 0)),
        compiler_params=_params(("arbitrary",)),
        name="ctx_mixer",
    )(sink, p, _dft_matrices(SEQ), _dft_channel_matrix(fg))


def _rope_tables():
    rows = DEC_SEQ // GRID_W
    row = np.repeat(np.arange(rows, dtype=np.float32), GRID_W)
    col = np.tile(np.arange(GRID_W, dtype=np.float32), rows)
    n_freq = HEAD_DIM // 4
    inv = jnp.asarray(ROPE_THETA, F32) ** (-jnp.arange(n_freq, dtype=F32) / n_freq)
    ang = jnp.concatenate([row[:, None] * inv, col[:, None] * inv], axis=-1)
    cos = jnp.repeat(jnp.cos(ang), 2, axis=-1)
    sin = jnp.repeat(jnp.sin(ang), 2, axis=-1)
    sign = jnp.tile(jnp.asarray([-1.0, 1.0], F32), HEAD_DIM // 2)
    return jnp.tile(cos, (1, 2)), jnp.tile(sin * sign, (1, 2))


def _rope(x, cos, sin_signed):
    lane = lax.broadcasted_iota(jnp.int32, x.shape, 1)
    width = x.shape[1]
    partner = jnp.where(lane % 2 == 0, pltpu.roll(x, width - 1, 1), pltpu.roll(x, 1, 1))
    return x * cos + partner * sin_signed


def _lat_mixer_kernel(sink_ref, p_ref, ck_ref, cv_ref, cos_ref, sin_ref, ts_ref, cs_ref, o_ref, q_scr, k_scr):
    attn_w = N_HEADS * HEAD_DIM
    kv_w = N_KV_HEADS * HEAD_DIM
    pair_w = 2 * HEAD_DIM
    group_heads = N_HEADS // N_KV_HEADS
    q_rows = 256
    cos = cos_ref[...]
    sin = sin_ref[...]
    for pair in range(N_HEADS // 2):
        q_scr[:, pair * pair_w:(pair + 1) * pair_w] = _rope(p_ref[:, pair * pair_w:(pair + 1) * pair_w], cos, sin)
    k_scr[...] = _rope(p_ref[:, attn_w:attn_w + kv_w], cos, sin)
    v = p_ref[:, attn_w + kv_w:attn_w + 2 * kv_w]
    ck = ck_ref[0]
    cv = cv_ref[0]
    k = k_scr[...]
    ops = []
    for g in range(N_KV_HEADS):
        ops.append(_head_pair_operands(k, v, g) + _head_pair_operands(ck, cv, g))

    def chunk(c, carry):
        r0 = pl.multiple_of(c * q_rows, q_rows)
        qi = r0 + lax.broadcasted_iota(jnp.int32, (q_rows, DEC_SEQ), 0)
        kj = lax.broadcasted_iota(jnp.int32, (q_rows, DEC_SEQ), 1)
        valid = jnp.abs(qi - kj) <= WINDOW
        for g in range(N_KV_HEADS):
            kd, v_lo, v_hi, ckd, cv_lo, cv_hi = ops[g]
            for i in range(group_heads // 2):
                pair = g * (group_heads // 2) + i
                q_lo, q_hi = _split_pair(q_scr[pl.ds(r0, q_rows), pair * pair_w:(pair + 1) * pair_w])
                s_lo = jnp.where(valid, _dot_nt(q_lo, kd), -jnp.inf)
                o = _softmax_pv([_dot_nt(q_lo, ckd), s_lo], [cv_lo, v_lo], sink_ref[2 * pair])
                s_hi = jnp.where(valid, _dot_nt(q_hi, kd), -jnp.inf)
                o = o + _softmax_pv([_dot_nt(q_hi, ckd), s_hi], [cv_hi, v_hi], sink_ref[2 * pair + 1])
                o_ref[pl.ds(r0, q_rows), pair * pair_w:(pair + 1) * pair_w] = o.astype(o_ref.dtype)
        return carry

    lax.fori_loop(0, DEC_SEQ // q_rows, chunk, 0)
    f0 = attn_w + 2 * kv_w
    fg = (D_MODEL - attn_w) // N_FOURIER_GROUPS
    for g in range(N_FOURIER_GROUPS):
        z = _fourier_group(p_ref[:, f0 + g * fg:f0 + (g + 1) * fg], ts_ref, cs_ref)
        o_ref[:, attn_w + g * fg:attn_w + (g + 1) * fg] = z.astype(o_ref.dtype)


def _lat_mixer(p, cache_k, cache_v, sink):
    n, width = DEC_BATCH * DEC_SEQ, p.shape[1]
    first = (BATCH * SEQ) // DEC_SEQ
    kv_w = N_KV_HEADS * HEAD_DIM
    attn_w = N_HEADS * HEAD_DIM
    fg = (D_MODEL - attn_w) // N_FOURIER_GROUPS
    cos, sin = _rope_tables()
    return pl.pallas_call(
        _lat_mixer_kernel,
        out_shape=jax.ShapeDtypeStruct((n, D_MODEL), BF16),
        grid=(n // DEC_SEQ,),
        in_specs=[
            pl.BlockSpec(memory_space=pltpu.SMEM),
            pl.BlockSpec((DEC_SEQ, width), lambda b: (first + b, 0)),
            pl.BlockSpec((1, PAST_LEN, kv_w), lambda b: (b, 0, 0)),
            pl.BlockSpec((1, PAST_LEN, kv_w), lambda b: (b, 0, 0)),
            pl.BlockSpec((DEC_SEQ, 2 * HEAD_DIM), lambda b: (0, 0)),
            pl.BlockSpec((DEC_SEQ, 2 * HEAD_DIM), lambda b: (0, 0)),
            pl.BlockSpec((2 * DEC_SEQ, DEC_SEQ), lambda b: (0, 0)),
            pl.BlockSpec((2 * fg, fg), lambda b: (0, 0)),
        ],
        out_specs=pl.BlockSpec((DEC_SEQ, D_MODEL), lambda b: (b, 0)),
        scratch_shapes=[pltpu.VMEM((DEC_SEQ, attn_w), F32), pltpu.VMEM((DEC_SEQ, kv_w), F32)],
        compiler_params=_params(("arbitrary",)),
        name="lat_mixer",
    )(sink, p, cache_k, cache_v, cos, sin, _dft_matrices(DEC_SEQ), _dft_channel_matrix(fg))


def _rnn_core_kernel(xr_ref, gr_ref, cw_ref, cb_ref, wa_ref, ba_ref, wx_ref, bx_ref, lam_ref, h0_ref,
                     y_ref, st_ref, a_scr, b_scr, h_scr, *, n_seq, seq_len):
    rows = n_seq * seq_len
    n_lb = xr_ref.shape[1] // LANES
    xr = xr_ref[...]
    t_idx = lax.broadcasted_iota(jnp.int32, (rows, 1), 0) % seq_len
    xc = jnp.broadcast_to(cb_ref[...], xr.shape)
    for tap in range(CONV_WIDTH):
        off = tap - CONV_LEFT
        shifted = xr if off == 0 else pltpu.roll(xr, (-off) % rows, 0)
        valid = (t_idx + off >= 0) & (t_idx + off < seq_len)
        xc = xc + jnp.where(valid, shifted, 0.0) * cw_ref[tap:tap + 1, :]
    xcb = xc.astype(BF16)
    for d in range(2):
        r = jax.nn.sigmoid(jnp.dot(xcb, wa_ref[d, 0].astype(BF16), preferred_element_type=F32) + ba_ref[d:d + 1, :])
        gi = jax.nn.sigmoid(jnp.dot(xcb, wx_ref[d, 0].astype(BF16), preferred_element_type=F32) + bx_ref[d:d + 1, :])
        neg_lam = -lam_ref[d:d + 1, :]
        softplus = jnp.maximum(neg_lam, 0.0) + jnp.log1p(jnp.exp(-jnp.abs(neg_lam)))
        log_a = -RG_C * r * softplus
        a = jnp.exp(log_a)
        th = jnp.tanh(log_a)
        b = jnp.sqrt(-2.0 * th / (1.0 - th)) * (gi * xc)
        for lb in range(n_lb):
            a_scr[d * n_lb + lb] = a[:, lb * LANES:(lb + 1) * LANES]
            b_scr[d * n_lb + lb] = b[:, lb * LANES:(lb + 1) * LANES]

    def time_rows(t):
        return pl.ds(t, n_seq, stride=seq_len)

    def lane_block(ref, k, lb):
        return ref[:, k, lb * LANES:(lb + 1) * LANES]

    def fwd(t, hs):
        out = []
        for lb in range(n_lb):
            h = a_scr[lb, time_rows(t), :] * hs[lb] + b_scr[lb, time_rows(t), :]
            h_scr[lb, time_rows(t), :] = h
            out.append(h)
        return tuple(out)

    hs = lax.fori_loop(0, seq_len, fwd, tuple(lane_block(h0_ref, 0, lb) for lb in range(n_lb)))
    for lb in range(n_lb):
        st_ref[:, 0, lb * LANES:(lb + 1) * LANES] = hs[lb]

    def bwd(i, hs):
        t = seq_len - 1 - i
        out = []
        for lb in range(n_lb):
            h = a_scr[n_lb + lb, time_rows(t), :] * hs[lb] + b_scr[n_lb + lb, time_rows(t), :]
            h_scr[lb, time_rows(t), :] = h_scr[lb, time_rows(t), :] + h
            out.append(h)
        return tuple(out)

    hs = lax.fori_loop(0, seq_len, bwd, tuple(lane_block(h0_ref, 1, lb) for lb in range(n_lb)))
    for lb in range(n_lb):
        st_ref[:, 1, lb * LANES:(lb + 1) * LANES] = hs[lb]
    for lb in range(n_lb):
        gr = gr_ref[:, lb * LANES:(lb + 1) * LANES]
        gelu = 0.5 * gr * (1.0 + jnp.tanh(math.sqrt(2.0 / math.pi) * (gr + 0.044715 * (gr * gr * gr))))
        y_ref[:, lb * LANES:(lb + 1) * LANES] = (gelu * h_scr[lb]).astype(y_ref.dtype)


def _rnn_core(p, first_row, n, h0, conv_w, conv_b, w_a, b_a, w_x, b_x, lam, n_seq, seq_len):
    d_rnn = p.shape[1] // 2
    cb = d_rnn // N_RNN_BLOCKS
    rows = n_seq * seq_len
    n_batch = n // seq_len
    first = first_row // rows
    kern = functools.partial(_rnn_core_kernel, n_seq=n_seq, seq_len=seq_len)
    return pl.pallas_call(
        kern,
        out_shape=(jax.ShapeDtypeStruct((n, d_rnn), BF16), jax.ShapeDtypeStruct((n_batch, 2, d_rnn), F32)),
        grid=(n // rows, N_RNN_BLOCKS),
        in_specs=[
            pl.BlockSpec((rows, cb), lambda i, c: (first + i, c)),
            pl.BlockSpec((rows, cb), lambda i, c: (first + i, N_RNN_BLOCKS + c)),
            pl.BlockSpec((CONV_WIDTH, cb), lambda i, c: (0, c)),
            pl.BlockSpec((1, cb), lambda i, c: (0, c)),
            pl.BlockSpec((2, 1, cb, cb), lambda i, c: (0, c, 0, 0)),
            pl.BlockSpec((2, cb), lambda i, c: (0, c)),
            pl.BlockSpec((2, 1, cb, cb), lambda i, c: (0, c, 0, 0)),
            pl.BlockSpec((2, cb), lambda i, c: (0, c)),
            pl.BlockSpec((2, cb), lambda i, c: (0, c)),
            pl.BlockSpec((n_seq, 2, cb), lambda i, c: (i, 0, c)),
        ],
        out_specs=(
            pl.BlockSpec((rows, cb), lambda i, c: (i, c)),
            pl.BlockSpec((n_seq, 2, cb), lambda i, c: (i, 0, c)),
        ),
        scratch_shapes=[pltpu.VMEM((2 * cb // LANES, rows, LANES), F32), pltpu.VMEM((2 * cb // LANES, rows, LANES), F32),
                        pltpu.VMEM((cb // LANES, rows, LANES), F32)],
        compiler_params=_params(("arbitrary", "arbitrary")),
        name="rnn_core",
    )(p, p, conv_w, conv_b.reshape(1, d_rnn), w_a, b_a, w_x, b_x, lam, h0)


def _pack_bf16_pairs(h):
    half = h.shape[1] // 2
    bits = lax.bitcast_convert_type(h.astype(BF16).astype(F32), jnp.uint32)
    return (bits[:, :half] >> 16) | (bits[:, half:] & jnp.uint32(0xFFFF0000))


def _unpack_bf16_pairs(w):
    lo = lax.bitcast_convert_type(w << 16, F32)
    hi = lax.bitcast_convert_type(w & jnp.uint32(0xFFFF0000), F32)
    return jnp.concatenate([lo, hi], axis=1).astype(BF16)


def _router_kernel(x_ref, g_ref, sc_ref, sh_ref, w_ref, b_ref, h_ref, idx_ref, gate_ref, rank_ref, cnt_ref, run_ref):
    tm = x_ref.shape[0]

    @pl.when(pl.program_id(0) == 0)
    def _():
        run_ref[...] = jnp.zeros_like(run_ref)

    h = _rms_modulate(x_ref[...], g_ref[...], sc_ref[0], sh_ref[0])
    h_ref[...] = _pack_bf16_pairs(h)
    logits =jnp.dot(h, w_ref[...], preferred_element_type=F32, precision=lax.Precision.HIGHEST) + b_ref[...]
    lane = lax.broadcasted_iota(jnp.int32, logits.shape, 1).astype(F32)
    col = lax.broadcasted_iota(jnp.int32, (tm, TOP_K), 1)
    chosen = jnp.zeros(logits.shape, F32)
    top_v, top_i, hits = [], [], []
    work = logits
    for _ in range(TOP_K):
        m = jnp.max(work, axis=-1, keepdims=True)
        first = jnp.min(jnp.where(work == m, lane, float(N_EXPERTS)), axis=-1, keepdims=True)
        hit = lane == first
        work = jnp.where(hit, -jnp.inf, work)
        chosen = jnp.where(hit, 1.0, chosen)
        top_v.append(m)
        top_i.append(first)
        hits.append(hit)
    exps = [jnp.exp(v - top_v[0]) for v in top_v]
    den = exps[0]
    for e in exps[1:]:
        den = den + e
    ri = lax.broadcasted_iota(jnp.int32, (tm, tm), 0)
    ci = lax.broadcasted_iota(jnp.int32, (tm, tm), 1)
    before = (ci < ri).astype(BF16)
    rank_all = jnp.dot(before, chosen.astype(BF16), preferred_element_type=F32) + run_ref[...]
    idx_out = jnp.zeros((tm, TOP_K), F32)
    gate_out = jnp.zeros((tm, TOP_K), F32)
    rank_out = jnp.zeros((tm, TOP_K), F32)
    for k in range(TOP_K):
        rk = jnp.sum(jnp.where(hits[k], rank_all, 0.0), axis=-1, keepdims=True)
        idx_out = jnp.where(col == k, top_i[k], idx_out)
        gate_out = jnp.where(col == k, exps[k] / den, gate_out)
        rank_out = jnp.where(col == k, rk, rank_out)
    idx_ref[...] = idx_out.astype(jnp.int32)
    gate_ref[...] = gate_out
    rank_ref[...] = rank_out.astype(jnp.int32)
    run_ref[...] = run_ref[...] + jnp.sum(chosen, axis=0, keepdims=True)
    cnt_ref[...] = run_ref[...]


def _router(x, g, scale, shift, w, b):
    n, d = x.shape
    e = w.shape[1]
    tm = ROW_TILE
    seg = lambda i: (_segment_of_tile(i, tm), 0, 0)
    small = pl.BlockSpec((tm, TOP_K), lambda i: (i, 0))
    return pl.pallas_call(
        _router_kernel,
        out_shape=(
            jax.ShapeDtypeStruct((n, d // 2), jnp.uint32),
            jax.ShapeDtypeStruct((n, TOP_K), jnp.int32),
            jax.ShapeDtypeStruct((n, TOP_K), F32),
            jax.ShapeDtypeStruct((n, TOP_K), jnp.int32),
            jax.ShapeDtypeStruct((1, e), F32),
        ),
        grid=(n // tm,),
        in_specs=[
            pl.BlockSpec((tm, d), lambda i: (i, 0)),
            pl.BlockSpec((1, d), lambda i: (0, 0)),
            pl.BlockSpec((1, 1, d), seg),
            pl.BlockSpec((1, 1, d), seg),
            pl.BlockSpec((d, e), lambda i: (0, 0)),
            pl.BlockSpec((1, e), lambda i: (0, 0)),
        ],
        out_specs=(pl.BlockSpec((tm, d // 2), lambda i: (i, 0)), small, small, small,
                   pl.BlockSpec((1, e), lambda i: (0, 0))),
        scratch_shapes=[pltpu.VMEM((1, e), F32)],
        compiler_params=_params(("arbitrary",)),
        name="router",
    )(x, g.reshape(1, d), scale, shift, w, b.reshape(1, e))


def _moe_kernel(te_ref, tv_ref, src_ref, h_hbm, wg_ref, wu_ref, wd_ref, bg_ref, bu_ref, bd_ref, o_ref,
                gather_buf, gather_sem, xs_scr, act_scr, wg_scr, wu_scr, wd_scr):
    t = pl.program_id(0)
    s = pl.program_id(1)
    n_j = act_scr.shape[0]
    tile_rows = xs_scr.shape[0]
    n_sub_total = tile_rows // MOE_SUB_ROWS

    def sub_blocks(tile):
        return (tv_ref[tile] + MOE_SUB_ROWS - 1) // MOE_SUB_ROWS

    def sub_rows(r):
        return pl.ds(pl.multiple_of(r * MOE_SUB_ROWS, MOE_SUB_ROWS), MOE_SUB_ROWS)

    def start_gather(tile):
        def issue(i, carry):
            for q in range(GATHER_UNROLL):
                r = i * GATHER_UNROLL + q
                token = src_ref[tile * tile_rows + r]
                pltpu.make_async_copy(h_hbm.at[pl.ds(token, 1), :], gather_buf.at[pl.ds(r, 1), :], gather_sem).start()
            return carry

        lax.fori_loop(0, sub_blocks(tile) * (MOE_SUB_ROWS // GATHER_UNROLL), issue, 0)

    def wait_gather(tile):
        def wait_sub(r, carry):
            pltpu.make_async_copy(h_hbm.at[pl.ds(0, MOE_SUB_ROWS), :], gather_buf.at[sub_rows(r), :], gather_sem).wait()
            return carry

        lax.fori_loop(0, sub_blocks(tile), wait_sub, 0)

    def for_each_sub_block(n_sub, one):
        def pair(i, carry):
            one(2 * i)
            one(2 * i + 1)
            return carry

        lax.fori_loop(0, n_sub // 2, pair, 0)

        @pl.when(n_sub % 2 == 1)
        def _():
            one(n_sub - 1)

    n_valid = tv_ref[t]
    n_sub = sub_blocks(t)

    @pl.when((s == 0) & (n_valid > 0))
    def _():
        @pl.when(t == 0)
        def _():
            start_gather(0)

        wait_gather(t)

        def unpack(r, carry):
            xs_scr[sub_rows(r), :] = _unpack_bf16_pairs(gather_buf[sub_rows(r), :])
            return carry

        lax.fori_loop(0, n_sub, unpack, 0)

        @pl.when(t + 1 < pl.num_programs(0))
        def _():
            start_gather(t + 1)

    @pl.when((s < n_j) & (n_valid > 0))
    def _():
        wg_scr[...] = wg_ref[0, 0].astype(BF16)
        wu_scr[...] = wu_ref[0, 0].astype(BF16)

        def gate_up(r):
            x = xs_scr[sub_rows(r), :]
            g = jnp.dot(x, wg_scr[...], preferred_element_type=F32) + bg_ref[0]
            u = jnp.dot(x, wu_scr[...], preferred_element_type=F32) + bu_ref[0]
            g = jnp.minimum(g, SWIGLU_LIMIT)
            u = jnp.clip(u, -SWIGLU_LIMIT, SWIGLU_LIMIT)
            act = (u + 1.0) * (g * jax.nn.sigmoid(SWIGLU_ALPHA * g))
            act_scr[s, sub_rows(r), :] = act.astype(BF16)

        for_each_sub_block(n_sub, gate_up)

    @pl.when(s >= n_j)
    def _():
        @pl.when(n_valid > 0)
        def _():
            wd_scr[...] = wd_ref[0, 0].astype(BF16)

            def down(r):
                act = jnp.concatenate([act_scr[jb, sub_rows(r), :] for jb in range(n_j)], axis=1)
                o_ref[sub_rows(r), :] = jnp.dot(act, wd_scr[...], preferred_element_type=F32) + bd_ref[0]

            for_each_sub_block(n_sub, down)

        def clear(r, carry):
            o_ref[sub_rows(r), :] = jnp.zeros((MOE_SUB_ROWS, o_ref.shape[1]), o_ref.dtype)
            return carry

        lax.fori_loop(n_sub, n_sub_total, clear, 0)


def _moe_experts(h_packed, src, tile_expert, tile_valid, layer, w_gate, b_gate, w_up, b_up, w_down, b_down):
    p_rows = src.shape[0]
    _, e, d, hdim = w_gate.shape
    tm, th, tn = MOE_TILE_ROWS, MOE_HIDDEN_BLOCK, MOE_OUT_BLOCK
    n_tiles = p_rows // tm
    n_j = hdim // th
    n_out = d // tn

    def hidden_block(t, s, tv):
        return jnp.where(tv[t] > 0, jnp.minimum(s, n_j - 1), n_j - 1)

    def out_block(t, s, tv):
        return jnp.where(tv[t] > 0, jnp.maximum(s - n_j, 0), n_out - 1)

    grid_spec = pltpu.PrefetchScalarGridSpec(
        num_scalar_prefetch=3,
        grid=(n_tiles, n_j + n_out),
        in_specs=[
            pl.BlockSpec(memory_space=pl.ANY),
            pl.BlockSpec((1, 1, d, th), lambda t, s, te, tv, src: (layer, te[t], 0, hidden_block(t, s, tv))),
            pl.BlockSpec((1, 1, d, th), lambda t, s, te, tv, src: (layer, te[t], 0, hidden_block(t, s, tv))),
            pl.BlockSpec((1, 1, hdim, tn), lambda t, s, te, tv, src: (layer, te[t], 0, out_block(t, s, tv))),
            pl.BlockSpec((1, 1, th), lambda t, s, te, tv, src: (te[t], 0, hidden_block(t, s, tv))),
            pl.BlockSpec((1, 1, th), lambda t, s, te, tv, src: (te[t], 0, hidden_block(t, s, tv))),
            pl.BlockSpec((1, 1, tn), lambda t, s, te, tv, src: (te[t], 0, out_block(t, s, tv))),
        ],
        out_specs=pl.BlockSpec((tm, tn), lambda t, s, te, tv, src: (t, jnp.maximum(s - n_j, 0))),
        scratch_shapes=[
            pltpu.VMEM((tm, d // 2), jnp.uint32),
            pltpu.SemaphoreType.DMA(()),
            pltpu.VMEM((tm, d), BF16),
            pltpu.VMEM((n_j, tm, th), BF16),
            pltpu.VMEM((d, th), BF16),
            pltpu.VMEM((d, th), BF16),
            pltpu.VMEM((hdim, tn), BF16),
        ],
    )
    return pl.pallas_call(
        _moe_kernel,
        out_shape=jax.ShapeDtypeStruct((p_rows, d), F32),
        grid_spec=grid_spec,
        compiler_params=_params(("arbitrary", "arbitrary")),
        name="moe_experts",
    )(tile_expert, tile_valid, src, h_packed, w_gate, w_up, w_down,
      b_gate[layer].reshape(e, 1, hdim), b_up[layer].reshape(e, 1, hdim), b_down[layer].reshape(e, 1, d))


def _combine_kernel(pos_ref, ys_hbm, gates_ref, x_ref, gate2_ref, norm_ref, o_ref, rows_buf, rows_sem, *, final_norm):
    i = pl.program_id(0)
    tm = x_ref.shape[0]
    slot = i % 2

    def start_gather(tile, slot):
        def issue(i, carry):
            for q in range(GATHER_UNROLL // TOP_K):
                r = i * (GATHER_UNROLL // TOP_K) + q
                for k in range(TOP_K):
                    row = pos_ref[(tile * tm + r) * TOP_K + k]
                    pltpu.make_async_copy(ys_hbm.at[pl.ds(row, 1), :], rows_buf.at[slot, pl.ds(k * tm + r, 1), :],
                                          rows_sem.at[slot]).start()
            return carry

        lax.fori_loop(0, tm // (GATHER_UNROLL // TOP_K), issue, 0)

    @pl.when(i == 0)
    def _():
        start_gather(0, 0)

    @pl.when(i + 1 < pl.num_programs(0))
    def _():
        start_gather(i + 1, 1 - slot)

    pltpu.make_async_copy(ys_hbm.at[pl.ds(0, TOP_K * tm), :], rows_buf.at[slot], rows_sem.at[slot]).wait()
    gates = gates_ref[...]
    moe = gates[:, 0:1] * rows_buf[slot, pl.ds(0, tm), :]
    for k in range(1, TOP_K):
        moe = moe + gates[:, k:k + 1] * rows_buf[slot, pl.ds(k * tm, tm), :]
    x = x_ref[...] + gate2_ref[0] * moe
    if final_norm:
        ms = jnp.mean(x * x, axis=-1, keepdims=True)
        x = x * lax.rsqrt(ms + EPS) * norm_ref[...]
    o_ref[...] = x


def _combine(ys, pos, gates, x, gate2, norm_g, final_norm):
    n, d = x.shape
    tm = COMBINE_ROWS
    grid_spec = pltpu.PrefetchScalarGridSpec(
        num_scalar_prefetch=1,
        grid=(n // tm,),
        in_specs=[
            pl.BlockSpec(memory_space=pl.ANY),
            pl.BlockSpec((tm, TOP_K), lambda i, pos: (i, 0)),
            pl.BlockSpec((tm, d), lambda i, pos: (i, 0)),
            pl.BlockSpec((1, 1, d), lambda i, pos: (_segment_of_tile(i, tm), 0, 0)),
            pl.BlockSpec((1, d), lambda i, pos: (0, 0)),
        ],
        out_specs=pl.BlockSpec((tm, d), lambda i, pos: (i, 0)),
        scratch_shapes=[pltpu.VMEM((2, TOP_K * tm, d), F32), pltpu.SemaphoreType.DMA((2,))],
    )
    return pl.pallas_call(
        functools.partial(_combine_kernel, final_norm=final_norm),
        out_shape=jax.ShapeDtypeStruct((n, d), F32),
        grid_spec=grid_spec,
        compiler_params=_params(("arbitrary",)),
        name="moe_combine",
    )(pos.reshape(-1), ys, gates, x, gate2, norm_g.reshape(1, d))


def _moe_layer(x, g, scale, shift, gate2, layer, w_router, b_router, w_gate, b_gate, w_up, b_up, w_down, b_down,
               norm_g, final_norm):
    n, d = x.shape
    e = w_router.shape[-1]
    tm = MOE_TILE_ROWS
    h, top_i, gates, rank, counts = _router(x, g, scale, shift, w_router[layer], b_router[layer])
    counts = counts[0].astype(jnp.int32)
    padded = ((counts + tm - 1) // tm) * tm
    ends = jnp.cumsum(padded)
    starts = ends - padded
    n_tiles = -(-(n * TOP_K) // tm) + e
    p_rows = n_tiles * tm
    tile_start = jnp.arange(n_tiles, dtype=jnp.int32) * tm
    n_used = ends[-1] // tm
    tile_expert = jnp.minimum(jnp.searchsorted(ends, tile_start, side="right"), e - 1).astype(jnp.int32)
    tile_valid = jnp.clip(counts[tile_expert] - (tile_start - starts[tile_expert]), 0, tm)
    tile_valid = jnp.where(tile_start < ends[-1], tile_valid, 0).astype(jnp.int32)
    last = jnp.maximum(n_used - 1, 0)
    tile_expert = jnp.where(tile_start < ends[-1], tile_expert, tile_expert[last]).astype(jnp.int32)
    pos = starts[top_i] + rank
    token = jnp.broadcast_to(jnp.arange(n, dtype=jnp.int32)[:, None], (n, TOP_K))
    src = jnp.zeros((p_rows,), jnp.int32).at[pos.reshape(-1)].set(token.reshape(-1))
    ys = _moe_experts(h, src, tile_expert, tile_valid, layer, w_gate, b_gate, w_up, b_up, w_down, b_down)
    return _combine(ys, pos, gates, x, gate2, norm_g, final_norm)


def kernel(x_prompt, x_sample, cache_k, cache_v, state_rglru, c, c_ctx, norm_mix, norm_ffn, w_mod, b_mod,
           attn_w_in, attn_b_in, attn_w_out, attn_b_out, attn_sink,
           rnn_w_in, rnn_b_in, rnn_conv_w, rnn_conv_b, rnn_w_a, rnn_b_a, rnn_w_x, rnn_b_x,
           rnn_lambda, rnn_w_out, rnn_b_out,
           moe_w_router, moe_b_router, moe_w_gate, moe_b_gate, moe_w_up, moe_b_up,
           moe_w_down, moe_b_down, final_norm):
    d = D_MODEL
    n_ctx = BATCH * SEQ
    attn_w = N_HEADS * HEAD_DIM
    kv_w = N_KV_HEADS * HEAD_DIM
    x = jnp.concatenate([x_prompt.reshape(n_ctx, d), x_sample.reshape(DEC_BATCH * DEC_SEQ, d)], axis=0)
    cond = jnp.concatenate([c_ctx[None, :], c, jnp.zeros((SUBLANES - 1 - DEC_BATCH, d), F32)], axis=0)
    mods = _modulation(cond, w_mod, b_mod)

    def mod(l, k):
        return mods[l, :, k * d:(k + 1) * d].reshape(SUBLANES, 1, d)

    new_k, new_v, new_s = [], [], []
    for l in range(DEPTH):
        j = l // 2
        sh1, sc1, g1, sh2, sc2, g2 = [mod(l, k) for k in range(N_MOD)]
        if l % 2 == 0:
            p = _proj_in(x, norm_mix[l], sc1, sh1, attn_w_in[j], attn_b_in[j], tn=PROJ_IN_ATTN_COLS)
            new_k.append(p[:n_ctx, attn_w:attn_w + kv_w].reshape(BATCH, SEQ, N_KV_HEADS, HEAD_DIM))
            new_v.append(p[:n_ctx, attn_w + kv_w:attn_w + 2 * kv_w].reshape(BATCH, SEQ, N_KV_HEADS, HEAD_DIM))
            mix_ctx = _ctx_mixer(p, attn_sink[j])
            mix_lat = _lat_mixer(p, cache_k[:, j].reshape(DEC_BATCH, PAST_LEN, kv_w),
                                 cache_v[:, j].reshape(DEC_BATCH, PAST_LEN, kv_w), attn_sink[j])
            mix = jnp.concatenate([mix_ctx, mix_lat], axis=0)
            x = _proj_out(mix, attn_w_out[j], attn_b_out[j], g1, x)
        else:
            p = _proj_in(x, norm_mix[l], sc1, sh1, rnn_w_in[j], rnn_b_in[j], tn=PROJ_IN_RNN_COLS)
            args = (rnn_conv_w[j], rnn_conv_b[j], rnn_w_a[j], rnn_b_a[j], rnn_w_x[j], rnn_b_x[j], rnn_lambda[j])
            h0_ctx = jnp.zeros((BATCH, 2, d), F32)
            mix_ctx, st = _rnn_core(p, 0, n_ctx, h0_ctx, *args, n_seq=SUBLANES, seq_len=SEQ)
            mix_lat, _ = _rnn_core(p, n_ctx, DEC_BATCH * DEC_SEQ, state_rglru[:, j], *args,
                                   n_seq=DEC_BATCH, seq_len=DEC_SEQ)
            new_s.append(st)
            mix = jnp.concatenate([mix_ctx, mix_lat], axis=0)
            x = _proj_out(mix, rnn_w_out[j], rnn_b_out[j], g1, x)
        x = _moe_layer(x, norm_ffn[l], sc2, sh2, g2, l, moe_w_router, moe_b_router,
                       moe_w_gate, moe_b_gate, moe_w_up, moe_b_up, moe_w_down, moe_b_down,
                       final_norm, final_norm=(l == DEPTH - 1))
    y = x
    y_prompt = y[:n_ctx].reshape(BATCH, SEQ, d)
    y_sample = y[n_ctx:].reshape(DEC_BATCH, DEC_SEQ, d)
    return (y_prompt, y_sample, jnp.stack(new_k, axis=1), jnp.stack(new_v, axis=1), jnp.stack(new_s, axis=1))
```

```python
import functools
import math

import numpy as np
import jax
import jax.numpy as jnp
from jax import lax
from jax.experimental import pallas as pl
from jax.experimental.pallas import tpu as pltpu

D_MODEL = 2048
BATCH = 32
SEQ = 256
DEPTH = 2
DEC_BATCH = 2
DEC_SEQ = 1024
PAST_LEN = 256
GRID_W = 64
N_HEADS = 16
N_KV_HEADS = 2
HEAD_DIM = 64
WINDOW = 128
ROPE_THETA = 10000.0
N_FOURIER_GROUPS = 4
N_RNN_BLOCKS = 8
CONV_WIDTH = 4
CONV_LEFT = 2
RG_C = 8.0
N_EXPERTS = 32
TOP_K = 4
SWIGLU_LIMIT = 7.0
SWIGLU_ALPHA = 1.702
N_MOD = 6
EPS = 1e-6

LANES = 128
SUBLANES = 8
VMEM_LIMIT_BYTES = 56 * 1024 * 1024

MOE_TILE_ROWS = 1536
MOE_SUB_ROWS = 256
MOE_HIDDEN_BLOCK = 256
MOE_OUT_BLOCK = 256
GATHER_UNROLL = 8
SCAN_ROW_PAD = 8
SCAN_UNROLL = 4
ROW_TILE = 512
PROJ_IN_ROWS = 1024
COMBINE_ROWS = 128
PROJ_OUT_COLS = 512
PROJ_IN_ATTN_COLS = 768
PROJ_IN_RNN_COLS = 1024

F32 = jnp.float32
BF16 = jnp.bfloat16


def _params(semantics):
    return pltpu.CompilerParams(dimension_semantics=semantics, vmem_limit_bytes=VMEM_LIMIT_BYTES)


def _segment_of_tile(i, tile_rows):
    n_ctx_tiles = (BATCH * SEQ) // tile_rows
    tiles_per_latent = DEC_SEQ // tile_rows
    return jnp.where(i < n_ctx_tiles, 0, 1 + (i - n_ctx_tiles) // tiles_per_latent)


def _rms_modulate(x, g, scale, shift):
    ms = jnp.mean(x * x, axis=-1, keepdims=True)
    return (x * lax.rsqrt(ms + EPS) * g) * (1.0 + scale) + shift


def _modulation_kernel(c_ref, w_ref, b_ref, o_ref):
    c = c_ref[...]
    s = (c * jax.nn.sigmoid(c)).astype(BF16)
    o_ref[0] = jnp.dot(s, w_ref[0].astype(BF16), preferred_element_type=F32) + b_ref[0]


def _modulation(cond, w_mod, b_mod):
    d = D_MODEL
    tn = 512
    n_out = N_MOD * d
    return pl.pallas_call(
        _modulation_kernel,
        out_shape=jax.ShapeDtypeStruct((DEPTH, SUBLANES, n_out), F32),
        grid=(DEPTH, n_out // tn),
        in_specs=[
            pl.BlockSpec((SUBLANES, d), lambda l, j: (0, 0)),
            pl.BlockSpec((1, d, tn), lambda l, j: (l, 0, j)),
            pl.BlockSpec((1, 1, tn), lambda l, j: (l, 0, j)),
        ],
        out_specs=pl.BlockSpec((1, SUBLANES, tn), lambda l, j: (l, 0, j)),
        compiler_params=_params(("arbitrary", "arbitrary")),
        name="modulation",
    )(cond, w_mod, b_mod.reshape(DEPTH, 1, n_out))


def _proj_in_kernel(x_ref, g_ref, sc_ref, sh_ref, w_ref, b_ref, o_ref, h_ref):
    @pl.when(pl.program_id(1) == 0)
    def _():
        h_ref[...] = _rms_modulate(x_ref[...], g_ref[...], sc_ref[0], sh_ref[0]).astype(BF16)

    o_ref[...] = jnp.dot(h_ref[...], w_ref[...].astype(BF16), preferred_element_type=F32) + b_ref[...]


def _proj_in(x, g, scale, shift, w, b, tn):
    n, d = x.shape
    n_out = w.shape[1]
    tm = PROJ_IN_ROWS
    seg = lambda i, j: (_segment_of_tile(i, tm), 0, 0)
    return pl.pallas_call(
        _proj_in_kernel,
        out_shape=jax.ShapeDtypeStruct((n, n_out), F32),
        grid=(n // tm, n_out // tn),
        in_specs=[
            pl.BlockSpec((tm, d), lambda i, j: (i, 0)),
            pl.BlockSpec((1, d), lambda i, j: (0, 0)),
            pl.BlockSpec((1, 1, d), seg),
            pl.BlockSpec((1, 1, d), seg),
            pl.BlockSpec((d, tn), lambda i, j: (0, j)),
            pl.BlockSpec((1, tn), lambda i, j: (0, j)),
        ],
        out_specs=pl.BlockSpec((tm, tn), lambda i, j: (i, j)),
        scratch_shapes=[pltpu.VMEM((tm, d), BF16)],
        compiler_params=_params(("arbitrary", "arbitrary")),
        name="proj_in",
    )(x, g.reshape(1, d), scale, shift, w, b.reshape(1, n_out))


def _proj_out_kernel(a_ref, w_ref, b_ref, gate_ref, res_ref, o_ref):
    y = jnp.dot(a_ref[...], w_ref[...].astype(BF16), preferred_element_type=F32) + b_ref[...]
    o_ref[...] = res_ref[...] + gate_ref[0] * y


def _proj_out(a, w, b, gate, res):
    n, k = a.shape
    d = w.shape[1]
    tm, tn = ROW_TILE, PROJ_OUT_COLS
    return pl.pallas_call(
        _proj_out_kernel,
        out_shape=jax.ShapeDtypeStruct((n, d), F32),
        grid=(n // tm, d // tn),
        in_specs=[
            pl.BlockSpec((tm, k), lambda i, j: (i, 0)),
            pl.BlockSpec((k, tn), lambda i, j: (0, j)),
            pl.BlockSpec((1, tn), lambda i, j: (0, j)),
            pl.BlockSpec((1, 1, tn), lambda i, j: (_segment_of_tile(i, tm), 0, j)),
            pl.BlockSpec((tm, tn), lambda i, j: (i, j)),
        ],
        out_specs=pl.BlockSpec((tm, tn), lambda i, j: (i, j)),
        compiler_params=_params(("arbitrary", "arbitrary")),
        name="proj_out",
    )(a, w, b.reshape(1, d), gate, res)


def _dot_nt(a, b):
    return lax.dot_general(a, b, (((1,), (1,)), ((), ())), preferred_element_type=F32)


def _head_pair_operands(k, v, group):
    lane = lax.broadcasted_iota(jnp.int32, k.shape, 1)
    low = lane < HEAD_DIM
    k_sw = pltpu.roll(k, HEAD_DIM, 1)
    v_sw = pltpu.roll(v, HEAD_DIM, 1)
    if group == 0:
        kd = jnp.where(low, k, k_sw)
        vd = jnp.where(low, v, v_sw)
    else:
        kd = jnp.where(low, k_sw, k)
        vd = jnp.where(low, v_sw, v)
    v_lo = jnp.where(low, vd, 0.0).astype(BF16)
    v_hi = jnp.where(low, 0.0, vd).astype(BF16)
    return kd.astype(BF16), v_lo, v_hi


def _split_pair(q2):
    lane = lax.broadcasted_iota(jnp.int32, q2.shape, 1)
    low = lane < HEAD_DIM
    qs = q2 * (HEAD_DIM ** -0.5)
    return jnp.where(low, qs, 0.0).astype(BF16), jnp.where(low, 0.0, qs).astype(BF16)


def _softmax_pv(scores, values, sink):
    m = jnp.full((scores[0].shape[0], 1), sink, F32)
    for s in scores:
        m = jnp.maximum(m, jnp.max(s, axis=-1, keepdims=True))
    den = jnp.exp(sink - m)
    ps = []
    for s in scores:
        p = jnp.exp(s - m)
        den = den + jnp.sum(p, axis=-1, keepdims=True)
        ps.append(p)
    out = None
    for p, v in zip(ps, values):
        o = jnp.dot((p / den).astype(BF16), v, preferred_element_type=F32)
        out = o if out is None else out + o
    return out


def _dft_matrices(t):
    idx = np.arange(t)
    ang = 2.0 * np.pi * ((idx[:, None] * idx[None, :]) % t) / t
    m = np.concatenate([np.cos(ang), np.sin(ang)], axis=0) / math.sqrt(t)
    return jnp.asarray(m, dtype=BF16)


def _dft_channel_matrix(c):
    idx = np.arange(c)
    ang = 2.0 * np.pi * ((idx[:, None] * idx[None, :]) % c) / c
    m = np.concatenate([np.cos(ang), -np.sin(ang)], axis=0) / math.sqrt(c)
    return jnp.asarray(m, dtype=BF16)


def _fourier_group(f_g, ts_ref, cs_ref):
    t = f_g.shape[0]
    ab = jnp.dot(ts_ref[...], f_g.astype(BF16), preferred_element_type=F32)
    lhs = jnp.concatenate([ab[:t], ab[t:]], axis=1).astype(BF16)
    return jnp.dot(lhs, cs_ref[...], preferred_element_type=F32)


def _ctx_mixer_kernel(sink_ref, p_ref, ts_ref, cs_ref, o_ref):
    attn_w = N_HEADS * HEAD_DIM
    kv_w = N_KV_HEADS * HEAD_DIM
    pair_w = 2 * HEAD_DIM
    group_heads = N_HEADS // N_KV_HEADS
    k = p_ref[:, attn_w:attn_w + kv_w]
    v = p_ref[:, attn_w + kv_w:attn_w + 2 * kv_w]
    for g in range(N_KV_HEADS):
        kd, v_lo, v_hi = _head_pair_operands(k, v, g)
        for i in range(group_heads // 2):
            pair = g * (group_heads // 2) + i
            q_lo, q_hi = _split_pair(p_ref[:, pair * pair_w:(pair + 1) * pair_w])
            o = _softmax_pv([_dot_nt(q_lo, kd)], [v_lo], sink_ref[2 * pair])
            o = o + _softmax_pv([_dot_nt(q_hi, kd)], [v_hi], sink_ref[2 * pair + 1])
            o_ref[:, pair * pair_w:(pair + 1) * pair_w] = o.astype(o_ref.dtype)
    f0 = attn_w + 2 * kv_w
    fg = (D_MODEL - attn_w) // N_FOURIER_GROUPS
    for g in range(N_FOURIER_GROUPS):
        z = _fourier_group(p_ref[:, f0 + g * fg:f0 + (g + 1) * fg], ts_ref, cs_ref)
        o_ref[:, attn_w + g * fg:attn_w + (g + 1) * fg] = z.astype(o_ref.dtype)


def _ctx_mixer(p, sink):
    n, width = BATCH * SEQ, p.shape[1]
    fg = (D_MODEL - N_HEADS * HEAD_DIM) // N_FOURIER_GROUPS
    return pl.pallas_call(
        _ctx_mixer_kernel,
        out_shape=jax.ShapeDtypeStruct((n, D_MODEL), BF16),
        grid=(n // SEQ,),
        in_specs=[
            pl.BlockSpec(memory_space=pltpu.SMEM),
            pl.BlockSpec((SEQ, width), lambda b: (b, 0)),
            pl.BlockSpec((2 * SEQ, SEQ), lambda b: (0, 0)),
            pl.BlockSpec((2 * fg, fg), lambda b: (0, 0)),
        ],
        out_specs=pl.BlockSpec((SEQ, D_MODEL), lambda b: (b, 0)),
        compiler_params=_params(("arbitrary",)),
        name="ctx_mixer",
    )(sink, p, _dft_matrices(SEQ), _dft_channel_matrix(fg))


def _rope_tables():
    rows = DEC_SEQ // GRID_W
    row = np.repeat(np.arange(rows, dtype=np.float32), GRID_W)
    col = np.tile(np.arange(GRID_W, dtype=np.float32), rows)
    n_freq = HEAD_DIM // 4
    inv = jnp.asarray(ROPE_THETA, F32) ** (-jnp.arange(n_freq, dtype=F32) / n_freq)
    ang = jnp.concatenate([row[:, None] * inv, col[:, None] * inv], axis=-1)
    cos = jnp.repeat(jnp.cos(ang), 2, axis=-1)
    sin = jnp.repeat(jnp.sin(ang), 2, axis=-1)
    sign = jnp.tile(jnp.asarray([-1.0, 1.0], F32), HEAD_DIM // 2)
    return jnp.tile(cos, (1, 2)), jnp.tile(sin * sign, (1, 2))


def _rope(x, cos, sin_signed):
    lane = lax.broadcasted_iota(jnp.int32, x.shape, 1)
    width = x.shape[1]
    partner = jnp.where(lane % 2 == 0, pltpu.roll(x, width - 1, 1), pltpu.roll(x, 1, 1))
    return x * cos + partner * sin_signed


def _lat_mixer_kernel(sink_ref, p_ref, ck_ref, cv_ref, cos_ref, sin_ref, ts_ref, cs_ref, o_ref, q_scr, k_scr):
    attn_w = N_HEADS * HEAD_DIM
    kv_w = N_KV_HEADS * HEAD_DIM
    pair_w = 2 * HEAD_DIM
    group_heads = N_HEADS // N_KV_HEADS
    q_rows = 256
    cos = cos_ref[...]
    sin = sin_ref[...]
    for pair in range(N_HEADS // 2):
        q_scr[:, pair * pair_w:(pair + 1) * pair_w] = _rope(p_ref[:, pair * pair_w:(pair + 1) * pair_w], cos, sin)
    k_scr[...] = _rope(p_ref[:, attn_w:attn_w + kv_w], cos, sin)
    v = p_ref[:, attn_w + kv_w:attn_w + 2 * kv_w]
    ck = ck_ref[0]
    cv = cv_ref[0]
    k = k_scr[...]
    ops = []
    for g in range(N_KV_HEADS):
        ops.append(_head_pair_operands(k, v, g) + _head_pair_operands(ck, cv, g))

    def chunk(c, carry):
        r0 = pl.multiple_of(c * q_rows, q_rows)
        qi = r0 + lax.broadcasted_iota(jnp.int32, (q_rows, DEC_SEQ), 0)
        kj = lax.broadcasted_iota(jnp.int32, (q_rows, DEC_SEQ), 1)
        valid = jnp.abs(qi - kj) <= WINDOW
        for g in range(N_KV_HEADS):
            kd, v_lo, v_hi, ckd, cv_lo, cv_hi = ops[g]
            for i in range(group_heads // 2):
                pair = g * (group_heads // 2) + i
                q_lo, q_hi = _split_pair(q_scr[pl.ds(r0, q_rows), pair * pair_w:(pair + 1) * pair_w])
                s_lo = jnp.where(valid, _dot_nt(q_lo, kd), -jnp.inf)
                o = _softmax_pv([_dot_nt(q_lo, ckd), s_lo], [cv_lo, v_lo], sink_ref[2 * pair])
                s_hi = jnp.where(valid, _dot_nt(q_hi, kd), -jnp.inf)
                o = o + _softmax_pv([_dot_nt(q_hi, ckd), s_hi], [cv_hi, v_hi], sink_ref[2 * pair + 1])
                o_ref[pl.ds(r0, q_rows), pair * pair_w:(pair + 1) * pair_w] = o.astype(o_ref.dtype)
        return carry

    lax.fori_loop(0, DEC_SEQ // q_rows, chunk, 0)
    f0 = attn_w + 2 * kv_w
    fg = (D_MODEL - attn_w) // N_FOURIER_GROUPS
    for g in range(N_FOURIER_GROUPS):
        z = _fourier_group(p_ref[:, f0 + g * fg:f0 + (g + 1) * fg], ts_ref, cs_ref)
        o_ref[:, attn_w + g * fg:attn_w + (g + 1) * fg] = z.astype(o_ref.dtype)


def _lat_mixer(p, cache_k, cache_v, sink):
    n, width = DEC_BATCH * DEC_SEQ, p.shape[1]
    first = (BATCH * SEQ) // DEC_SEQ
    kv_w = N_KV_HEADS * HEAD_DIM
    attn_w = N_HEADS * HEAD_DIM
    fg = (D_MODEL - attn_w) // N_FOURIER_GROUPS
    cos, sin = _rope_tables()
    return pl.pallas_call(
        _lat_mixer_kernel,
        out_shape=jax.ShapeDtypeStruct((n, D_MODEL), BF16),
        grid=(n // DEC_SEQ,),
        in_specs=[
            pl.BlockSpec(memory_space=pltpu.SMEM),
            pl.BlockSpec((DEC_SEQ, width), lambda b: (first + b, 0)),
            pl.BlockSpec((1, PAST_LEN, kv_w), lambda b: (b, 0, 0)),
            pl.BlockSpec((1, PAST_LEN, kv_w), lambda b: (b, 0, 0)),
            pl.BlockSpec((DEC_SEQ, 2 * HEAD_DIM), lambda b: (0, 0)),
            pl.BlockSpec((DEC_SEQ, 2 * HEAD_DIM), lambda b: (0, 0)),
            pl.BlockSpec((2 * DEC_SEQ, DEC_SEQ), lambda b: (0, 0)),
            pl.BlockSpec((2 * fg, fg), lambda b: (0, 0)),
        ],
        out_specs=pl.BlockSpec((DEC_SEQ, D_MODEL), lambda b: (b, 0)),
        scratch_shapes=[pltpu.VMEM((DEC_SEQ, attn_w), F32), pltpu.VMEM((DEC_SEQ, kv_w), F32)],
        compiler_params=_params(("arbitrary",)),
        name="lat_mixer",
    )(sink, p, cache_k, cache_v, cos, sin, _dft_matrices(DEC_SEQ), _dft_channel_matrix(fg))


def _rnn_core_kernel(xr_ref, gr_ref, cw_ref, cb_ref, wa_ref, ba_ref, wx_ref, bx_ref, lam_ref, h0_ref,
                     y_ref, st_ref, a_scr, b_scr, h_scr, *, n_seq, seq_len):
    rows = n_seq * seq_len
    n_lb = xr_ref.shape[1] // LANES
    xr = xr_ref[...]
    t_idx = lax.broadcasted_iota(jnp.int32, (rows, 1), 0) % seq_len
    xc = jnp.broadcast_to(cb_ref[...], xr.shape)
    for tap in range(CONV_WIDTH):
        off = tap - CONV_LEFT
        shifted = xr if off == 0 else pltpu.roll(xr, (-off) % rows, 0)
        valid = (t_idx + off >= 0) & (t_idx + off < seq_len)
        xc = xc + jnp.where(valid, shifted, 0.0) * cw_ref[tap:tap + 1, :]
    xcb = xc.astype(BF16)
    pitch = seq_len + SCAN_ROW_PAD

    def sigmoid(z):
        return 0.5 * jnp.tanh(0.5 * z) + 0.5

    def seq_rows(s):
        return pl.ds(s * pitch, seq_len)

    for d in range(2):
        r = sigmoid(jnp.dot(xcb, wa_ref[d, 0].astype(BF16), preferred_element_type=F32) + ba_ref[d:d + 1, :])
        gi = sigmoid(jnp.dot(xcb, wx_ref[d, 0].astype(BF16), preferred_element_type=F32) + bx_ref[d:d + 1, :])
        neg_lam = -lam_ref[d:d + 1, :]
        softplus = jnp.maximum(neg_lam, 0.0) + jnp.log1p(jnp.exp(-jnp.abs(neg_lam)))
        a = jnp.exp(-RG_C * r * softplus)
        b = jnp.sqrt(1.0 - a * a) * (gi * xc)
        for lb in range(n_lb):
            for s in range(n_seq):
                a_scr[d * n_lb + lb, seq_rows(s), :] = a[s * seq_len:(s + 1) * seq_len, lb * LANES:(lb + 1) * LANES]
                b_scr[d * n_lb + lb, seq_rows(s), :] = b[s * seq_len:(s + 1) * seq_len, lb * LANES:(lb + 1) * LANES]

    def time_rows(t):
        return pl.ds(t, n_seq, stride=pitch)

    def lane_block(ref, k, lb):
        return ref[:, k, lb * LANES:(lb + 1) * LANES]

    def fwd(t, hs):
        out = []
        for lb in range(n_lb):
            h = a_scr[lb, time_rows(t), :] * hs[lb] + b_scr[lb, time_rows(t), :]
            h_scr[lb, time_rows(t), :] = h
            out.append(h)
        return tuple(out)

    hs = lax.fori_loop(0, seq_len, fwd, tuple(lane_block(h0_ref, 0, lb) for lb in range(n_lb)), unroll=SCAN_UNROLL)
    for lb in range(n_lb):
        st_ref[:, 0, lb * LANES:(lb + 1) * LANES] = hs[lb]

    def bwd(i, hs):
        t = seq_len - 1 - i
        out = []
        for lb in range(n_lb):
            h = a_scr[n_lb + lb, time_rows(t), :] * hs[lb] + b_scr[n_lb + lb, time_rows(t), :]
            h_scr[lb, time_rows(t), :] = h_scr[lb, time_rows(t), :] + h
            out.append(h)
        return tuple(out)

    hs = lax.fori_loop(0, seq_len, bwd, tuple(lane_block(h0_ref, 1, lb) for lb in range(n_lb)), unroll=SCAN_UNROLL)
    for lb in range(n_lb):
        st_ref[:, 1, lb * LANES:(lb + 1) * LANES] = hs[lb]
    for lb in range(n_lb):
        for s in range(n_seq):
            gr = gr_ref[s * seq_len:(s + 1) * seq_len, lb * LANES:(lb + 1) * LANES]
            gelu = 0.5 * gr * (1.0 + jnp.tanh(math.sqrt(2.0 / math.pi) * (gr + 0.044715 * (gr * gr * gr))))
            y_ref[s * seq_len:(s + 1) * seq_len, lb * LANES:(lb + 1) * LANES] = (
                gelu * h_scr[lb, seq_rows(s), :]).astype(y_ref.dtype)


def _rnn_core(p, first_row, n, h0, conv_w, conv_b, w_a, b_a, w_x, b_x, lam, n_seq, seq_len):
    d_rnn = p.shape[1] // 2
    cb = d_rnn // N_RNN_BLOCKS
    rows = n_seq * seq_len
    n_batch = n // seq_len
    first = first_row // rows
    scan_rows = n_seq * (seq_len + SCAN_ROW_PAD)
    kern = functools.partial(_rnn_core_kernel, n_seq=n_seq, seq_len=seq_len)
    return pl.pallas_call(
        kern,
        out_shape=(jax.ShapeDtypeStruct((n, d_rnn), BF16), jax.ShapeDtypeStruct((n_batch, 2, d_rnn), F32)),
        grid=(n // rows, N_RNN_BLOCKS),
        in_specs=[
            pl.BlockSpec((rows, cb), lambda i, c: (first + i, c)),
            pl.BlockSpec((rows, cb), lambda i, c: (first + i, N_RNN_BLOCKS + c)),
            pl.BlockSpec((CONV_WIDTH, cb), lambda i, c: (0, c)),
            pl.BlockSpec((1, cb), lambda i, c: (0, c)),
            pl.BlockSpec((2, 1, cb, cb), lambda i, c: (0, c, 0, 0)),
            pl.BlockSpec((2, cb), lambda i, c: (0, c)),
            pl.BlockSpec((2, 1, cb, cb), lambda i, c: (0, c, 0, 0)),
            pl.BlockSpec((2, cb), lambda i, c: (0, c)),
            pl.BlockSpec((2, cb), lambda i, c: (0, c)),
            pl.BlockSpec((n_seq, 2, cb), lambda i, c: (i, 0, c)),
        ],
        out_specs=(
            pl.BlockSpec((rows, cb), lambda i, c: (i, c)),
            pl.BlockSpec((n_seq, 2, cb), lambda i, c: (i, 0, c)),
        ),
        scratch_shapes=[pltpu.VMEM((2 * cb // LANES, scan_rows, LANES), F32),
                        pltpu.VMEM((2 * cb // LANES, scan_rows, LANES), F32),
                        pltpu.VMEM((cb // LANES, scan_rows, LANES), F32)],
        compiler_params=_params(("arbitrary", "arbitrary")),
        name="rnn_core",
    )(p, p, conv_w, conv_b.reshape(1, d_rnn), w_a, b_a, w_x, b_x, lam, h0)


def _pack_bf16_pairs(h):
    half = h.shape[1] // 2
    bits = lax.bitcast_convert_type(h.astype(BF16).astype(F32), jnp.uint32)
    return (bits[:, :half] >> 16) | (bits[:, half:] & jnp.uint32(0xFFFF0000))


def _unpack_bf16_pairs(w):
    lo = lax.bitcast_convert_type(w << 16, F32)
    hi = lax.bitcast_convert_type(w & jnp.uint32(0xFFFF0000), F32)
    return jnp.concatenate([lo, hi], axis=1).astype(BF16)


def _router_kernel(x_ref, g_ref, sc_ref, sh_ref, w_ref, b_ref, h_ref, idx_ref, gate_ref, rank_ref, cnt_ref, run_ref):
    tm = x_ref.shape[0]

    @pl.when(pl.program_id(0) == 0)
    def _():
        run_ref[...] = jnp.zeros_like(run_ref)

    h = _rms_modulate(x_ref[...], g_ref[...], sc_ref[0], sh_ref[0])
    h_ref[...] = _pack_bf16_pairs(h)
    logits =jnp.dot(h, w_ref[...], preferred_element_type=F32, precision=lax.Precision.HIGHEST) + b_ref[...]
    lane = lax.broadcasted_iota(jnp.int32, logits.shape, 1).astype(F32)
    col = lax.broadcasted_iota(jnp.int32, (tm, TOP_K), 1)
    chosen = jnp.zeros(logits.shape, F32)
    top_v, top_i, hits = [], [], []
    work = logits
    for _ in range(TOP_K):
        m = jnp.max(work, axis=-1, keepdims=True)
        first = jnp.min(jnp.where(work == m, lane, float(N_EXPERTS)), axis=-1, keepdims=True)
        hit = lane == first
        work = jnp.where(hit, -jnp.inf, work)
        chosen = jnp.where(hit, 1.0, chosen)
        top_v.append(m)
        top_i.append(first)
        hits.append(hit)
    exps = [jnp.exp(v - top_v[0]) for v in top_v]
    den = exps[0]
    for e in exps[1:]:
        den = den + e
    ri = lax.broadcasted_iota(jnp.int32, (tm, tm), 0)
    ci = lax.broadcasted_iota(jnp.int32, (tm, tm), 1)
    before = (ci < ri).astype(BF16)
    rank_all = jnp.dot(before, chosen.astype(BF16), preferred_element_type=F32) + run_ref[...]
    idx_out = jnp.zeros((tm, TOP_K), F32)
    gate_out = jnp.zeros((tm, TOP_K), F32)
    rank_out = jnp.zeros((tm, TOP_K), F32)
    for k in range(TOP_K):
        rk = jnp.sum(jnp.where(hits[k], rank_all, 0.0), axis=-1, keepdims=True)
        idx_out = jnp.where(col == k, top_i[k], idx_out)
        gate_out = jnp.where(col == k, exps[k] / den, gate_out)
        rank_out = jnp.where(col == k, rk, rank_out)
    idx_ref[...] = idx_out.astype(jnp.int32)
    gate_ref[...] = gate_out
    rank_ref[...] = rank_out.astype(jnp.int32)
    run_ref[...] = run_ref[...] + jnp.sum(chosen, axis=0, keepdims=True)
    cnt_ref[...] = run_ref[...]


def _router(x, g, scale, shift, w, b):
    n, d = x.shape
    e = w.shape[1]
    tm = ROW_TILE
    seg = lambda i: (_segment_of_tile(i, tm), 0, 0)
    small = pl.BlockSpec((tm, TOP_K), lambda i: (i, 0))
    return pl.pallas_call(
        _router_kernel,
        out_shape=(
            jax.ShapeDtypeStruct((n, d // 2), jnp.uint32),
            jax.ShapeDtypeStruct((n, TOP_K), jnp.int32),
            jax.ShapeDtypeStruct((n, TOP_K), F32),
            jax.ShapeDtypeStruct((n, TOP_K), jnp.int32),
            jax.ShapeDtypeStruct((1, e), F32),
        ),
        grid=(n // tm,),
        in_specs=[
            pl.BlockSpec((tm, d), lambda i: (i, 0)),
            pl.BlockSpec((1, d), lambda i: (0, 0)),
            pl.BlockSpec((1, 1, d), seg),
            pl.BlockSpec((1, 1, d), seg),
            pl.BlockSpec((d, e), lambda i: (0, 0)),
            pl.BlockSpec((1, e), lambda i: (0, 0)),
        ],
        out_specs=(pl.BlockSpec((tm, d // 2), lambda i: (i, 0)), small, small, small,
                   pl.BlockSpec((1, e), lambda i: (0, 0))),
        scratch_shapes=[pltpu.VMEM((1, e), F32)],
        compiler_params=_params(("arbitrary",)),
        name="router",
    )(x, g.reshape(1, d), scale, shift, w, b.reshape(1, e))


def _moe_kernel(te_ref, tv_ref, src_ref, h_hbm, wg_ref, wu_ref, wd_ref, bg_ref, bu_ref, bd_ref, o_ref,
                gather_buf, gather_sem, issued_ref, xs_scr, act_scr, wg_scr, wu_scr, wd_scr):
    t = pl.program_id(0)
    s = pl.program_id(1)
    n_j = act_scr.shape[0]
    tile_rows = xs_scr.shape[0]
    n_sub_total = tile_rows // MOE_SUB_ROWS
    chunk = 1 << int(math.log2(tile_rows // (n_j * (n_sub_total // 2))))

    def sub_blocks(tile):
        return (tv_ref[tile] + MOE_SUB_ROWS - 1) // MOE_SUB_ROWS

    def sub_rows(r):
        return pl.ds(pl.multiple_of(r * MOE_SUB_ROWS, MOE_SUB_ROWS), MOE_SUB_ROWS)

    def start_rows(tile, first, count):
        for q in range(count):
            token = src_ref[tile * tile_rows + first + q]
            pltpu.make_async_copy(h_hbm.at[pl.ds(token, 1), :], gather_buf.at[pl.ds(first + q, 1), :], gather_sem).start()

    def start_next_chunk():
        first = issued_ref[0]
        start_rows(t + 1, first, chunk)
        issued_ref[0] = first + chunk

    def for_each_sub_block(n_sub, one, per_trip=None):
        def pair(i, carry):
            one(2 * i)
            one(2 * i + 1)
            if per_trip is not None:
                per_trip()
            return carry

        lax.fori_loop(0, n_sub // 2, pair, 0)

        @pl.when(n_sub % 2 == 1)
        def _():
            one(n_sub - 1)

    n_valid = tv_ref[t]
    n_sub = sub_blocks(t)

    @pl.when(s == 0)
    def _():
        @pl.when(t == 0)
        def _():
            issued_ref[0] = 0

        need = n_sub * MOE_SUB_ROWS
        have = issued_ref[0]

        def issue_missing(i, carry):
            start_rows(t, have + i * GATHER_UNROLL, GATHER_UNROLL)
            return carry

        lax.fori_loop(0, jnp.maximum(need - have, 0) // GATHER_UNROLL, issue_missing, 0)

        def wait_chunk(i, carry):
            pltpu.make_async_copy(h_hbm.at[pl.ds(0, chunk), :], gather_buf.at[pl.ds(0, chunk), :], gather_sem).wait()
            return carry

        lax.fori_loop(0, jnp.maximum(need, have) // chunk, wait_chunk, 0)
        issued_ref[0] = 0

        def unpack(r, carry):
            xs_scr[sub_rows(r), :] = _unpack_bf16_pairs(gather_buf[sub_rows(r), :])
            return carry

        lax.fori_loop(0, n_sub, unpack, 0)

    @pl.when((s < n_j) & (n_valid > 0))
    def _():
        wg_scr[...] = wg_ref[0, 0].astype(BF16)
        wu_scr[...] = wu_ref[0, 0].astype(BF16)

        def gate_up(r):
            x = xs_scr[sub_rows(r), :]
            g = jnp.dot(x, wg_scr[...], preferred_element_type=F32) + bg_ref[0]
            u = jnp.dot(x, wu_scr[...], preferred_element_type=F32) + bu_ref[0]
            g = jnp.minimum(g, SWIGLU_LIMIT)
            u = jnp.clip(u, -SWIGLU_LIMIT, SWIGLU_LIMIT)
            act = (u + 1.0) * (g * jax.nn.sigmoid(SWIGLU_ALPHA * g))
            act_scr[s, sub_rows(r), :] = act.astype(BF16)

        for_each_sub_block(n_sub, gate_up, per_trip=start_next_chunk)

    @pl.when(s >= n_j)
    def _():
        @pl.when(n_valid > 0)
        def _():
            wd_scr[...] = wd_ref[0, 0].astype(BF16)

            def down(r):
                act = jnp.concatenate([act_scr[jb, sub_rows(r), :] for jb in range(n_j)], axis=1)
                o_ref[sub_rows(r), :] = jnp.dot(act, wd_scr[...], preferred_element_type=F32) + bd_ref[0]

            for_each_sub_block(n_sub, down)

        def clear(r, carry):
            o_ref[sub_rows(r), :] = jnp.zeros((MOE_SUB_ROWS, o_ref.shape[1]), o_ref.dtype)
            return carry

        lax.fori_loop(n_sub, n_sub_total, clear, 0)


def _moe_experts(h_packed, src, tile_expert, tile_valid, layer, w_gate, b_gate, w_up, b_up, w_down, b_down):
    p_rows = src.shape[0]
    _, e, d, hdim = w_gate.shape
    tm, th, tn = MOE_TILE_ROWS, MOE_HIDDEN_BLOCK, MOE_OUT_BLOCK
    n_tiles = p_rows // tm
    n_j = hdim // th
    n_out = d // tn

    def hidden_block(t, s, tv):
        return jnp.where(tv[t] > 0, jnp.minimum(s, n_j - 1), n_j - 1)

    def out_block(t, s, tv):
        return jnp.where(tv[t] > 0, jnp.maximum(s - n_j, 0), n_out - 1)

    grid_spec = pltpu.PrefetchScalarGridSpec(
        num_scalar_prefetch=3,
        grid=(n_tiles, n_j + n_out),
        in_specs=[
            pl.BlockSpec(memory_space=pl.ANY),
            pl.BlockSpec((1, 1, d, th), lambda t, s, te, tv, src: (layer, te[t], 0, hidden_block(t, s, tv))),
            pl.BlockSpec((1, 1, d, th), lambda t, s, te, tv, src: (layer, te[t], 0, hidden_block(t, s, tv))),
            pl.BlockSpec((1, 1, hdim, tn), lambda t, s, te, tv, src: (layer, te[t], 0, out_block(t, s, tv))),
            pl.BlockSpec((1, 1, th), lambda t, s, te, tv, src: (te[t], 0, hidden_block(t, s, tv))),
            pl.BlockSpec((1, 1, th), lambda t, s, te, tv, src: (te[t], 0, hidden_block(t, s, tv))),
            pl.BlockSpec((1, 1, tn), lambda t, s, te, tv, src: (te[t], 0, out_block(t, s, tv))),
        ],
        out_specs=pl.BlockSpec((tm, tn), lambda t, s, te, tv, src: (t, jnp.maximum(s - n_j, 0))),
        scratch_shapes=[
            pltpu.VMEM((tm, d // 2), jnp.uint32),
            pltpu.SemaphoreType.DMA(()),
            pltpu.SMEM((1,), jnp.int32),
            pltpu.VMEM((tm, d), BF16),
            pltpu.VMEM((n_j, tm, th), BF16),
            pltpu.VMEM((d, th), BF16),
            pltpu.VMEM((d, th), BF16),
            pltpu.VMEM((hdim, tn), BF16),
        ],
    )
    return pl.pallas_call(
        _moe_kernel,
        out_shape=jax.ShapeDtypeStruct((p_rows, d), F32),
        grid_spec=grid_spec,
        compiler_params=_params(("arbitrary", "arbitrary")),
        name="moe_experts",
    )(tile_expert, tile_valid, src, h_packed, w_gate, w_up, w_down,
      b_gate[layer].reshape(e, 1, hdim), b_up[layer].reshape(e, 1, hdim), b_down[layer].reshape(e, 1, d))


def _combine_kernel(pos_ref, ys_hbm, gates_ref, x_ref, gate2_ref, norm_ref, o_ref, rows_buf, rows_sem, *, final_norm):
    i = pl.program_id(0)
    tm = x_ref.shape[0]
    slot = i % 2

    def start_gather(tile, slot):
        def issue(i, carry):
            for q in range(GATHER_UNROLL // TOP_K):
                r = i * (GATHER_UNROLL // TOP_K) + q
                for k in range(TOP_K):
                    row = pos_ref[(tile * tm + r) * TOP_K + k]
                    pltpu.make_async_copy(ys_hbm.at[pl.ds(row, 1), :], rows_buf.at[slot, pl.ds(k * tm + r, 1), :],
                                          rows_sem.at[slot]).start()
            return carry

        lax.fori_loop(0, tm // (GATHER_UNROLL // TOP_K), issue, 0)

    @pl.when(i == 0)
    def _():
        start_gather(0, 0)

    @pl.when(i + 1 < pl.num_programs(0))
    def _():
        start_gather(i + 1, 1 - slot)

    pltpu.make_async_copy(ys_hbm.at[pl.ds(0, TOP_K * tm), :], rows_buf.at[slot], rows_sem.at[slot]).wait()
    gates = gates_ref[...]
    moe = gates[:, 0:1] * rows_buf[slot, pl.ds(0, tm), :]
    for k in range(1, TOP_K):
        moe = moe + gates[:, k:k + 1] * rows_buf[slot, pl.ds(k * tm, tm), :]
    x = x_ref[...] + gate2_ref[0] * moe
    if final_norm:
        ms = jnp.mean(x * x, axis=-1, keepdims=True)
        x = x * lax.rsqrt(ms + EPS) * norm_ref[...]
    o_ref[...] = x


def _combine(ys, pos, gates, x, gate2, norm_g, final_norm):
    n, d = x.shape
    tm = COMBINE_ROWS
    grid_spec = pltpu.PrefetchScalarGridSpec(
        num_scalar_prefetch=1,
        grid=(n // tm,),
        in_specs=[
            pl.BlockSpec(memory_space=pl.ANY),
            pl.BlockSpec((tm, TOP_K), lambda i, pos: (i, 0)),
            pl.BlockSpec((tm, d), lambda i, pos: (i, 0)),
            pl.BlockSpec((1, 1, d), lambda i, pos: (_segment_of_tile(i, tm), 0, 0)),
            pl.BlockSpec((1, d), lambda i, pos: (0, 0)),
        ],
        out_specs=pl.BlockSpec((tm, d), lambda i, pos: (i, 0)),
        scratch_shapes=[pltpu.VMEM((2, TOP_K * tm, d), F32), pltpu.SemaphoreType.DMA((2,))],
    )
    return pl.pallas_call(
        functools.partial(_combine_kernel, final_norm=final_norm),
        out_shape=jax.ShapeDtypeStruct((n, d), F32),
        grid_spec=grid_spec,
        compiler_params=_params(("arbitrary",)),
        name="moe_combine",
    )(pos.reshape(-1), ys, gates, x, gate2, norm_g.reshape(1, d))


def _moe_layer(x, g, scale, shift, gate2, layer, w_router, b_router, w_gate, b_gate, w_up, b_up, w_down, b_down,
               norm_g, final_norm):
    n, d = x.shape
    e = w_router.shape[-1]
    tm = MOE_TILE_ROWS
    h, top_i, gates, rank, counts = _router(x, g, scale, shift, w_router[layer], b_router[layer])
    counts = counts[0].astype(jnp.int32)
    padded = ((counts + tm - 1) // tm) * tm
    ends = jnp.cumsum(padded)
    starts = ends - padded
    n_tiles = -(-(n * TOP_K) // tm) + e
    p_rows = n_tiles * tm
    tile_start = jnp.arange(n_tiles, dtype=jnp.int32) * tm
    n_used = ends[-1] // tm
    tile_expert = jnp.minimum(jnp.searchsorted(ends, tile_start, side="right"), e - 1).astype(jnp.int32)
    tile_valid = jnp.clip(counts[tile_expert] - (tile_start - starts[tile_expert]), 0, tm)
    tile_valid = jnp.where(tile_start < ends[-1], tile_valid, 0).astype(jnp.int32)
    last = jnp.maximum(n_used - 1, 0)
    tile_expert = jnp.where(tile_start < ends[-1], tile_expert, tile_expert[last]).astype(jnp.int32)
    pos = starts[top_i] + rank
    token = jnp.broadcast_to(jnp.arange(n, dtype=jnp.int32)[:, None], (n, TOP_K))
    src = jnp.zeros((p_rows,), jnp.int32).at[pos.reshape(-1)].set(token.reshape(-1))
    ys = _moe_experts(h, src, tile_expert, tile_valid, layer, w_gate, b_gate, w_up, b_up, w_down, b_down)
    return _combine(ys, pos, gates, x, gate2, norm_g, final_norm)


def kernel(x_prompt, x_sample, cache_k, cache_v, state_rglru, c, c_ctx, norm_mix, norm_ffn, w_mod, b_mod,
           attn_w_in, attn_b_in, attn_w_out, attn_b_out, attn_sink,
           rnn_w_in, rnn_b_in, rnn_conv_w, rnn_conv_b, rnn_w_a, rnn_b_a, rnn_w_x, rnn_b_x,
           rnn_lambda, rnn_w_out, rnn_b_out,
           moe_w_router, moe_b_router, moe_w_gate, moe_b_gate, moe_w_up, moe_b_up,
           moe_w_down, moe_b_down, final_norm):
    d = D_MODEL
    n_ctx = BATCH * SEQ
    attn_w = N_HEADS * HEAD_DIM
    kv_w = N_KV_HEADS * HEAD_DIM
    x = jnp.concatenate([x_prompt.reshape(n_ctx, d), x_sample.reshape(DEC_BATCH * DEC_SEQ, d)], axis=0)
    cond = jnp.concatenate([c_ctx[None, :], c, jnp.zeros((SUBLANES - 1 - DEC_BATCH, d), F32)], axis=0)
    mods = _modulation(cond, w_mod, b_mod)

    def mod(l, k):
        return mods[l, :, k * d:(k + 1) * d].reshape(SUBLANES, 1, d)

    new_k, new_v, new_s = [], [], []
    for l in range(DEPTH):
        j = l // 2
        sh1, sc1, g1, sh2, sc2, g2 = [mod(l, k) for k in range(N_MOD)]
        if l % 2 == 0:
            p = _proj_in(x, norm_mix[l], sc1, sh1, attn_w_in[j], attn_b_in[j], tn=PROJ_IN_ATTN_COLS)
            new_k.append(p[:n_ctx, attn_w:attn_w + kv_w].reshape(BATCH, SEQ, N_KV_HEADS, HEAD_DIM))
            new_v.append(p[:n_ctx, attn_w + kv_w:attn_w + 2 * kv_w].reshape(BATCH, SEQ, N_KV_HEADS, HEAD_DIM))
            mix_ctx = _ctx_mixer(p, attn_sink[j])
            mix_lat = _lat_mixer(p, cache_k[:, j].reshape(DEC_BATCH, PAST_LEN, kv_w),
                                 cache_v[:, j].reshape(DEC_BATCH, PAST_LEN, kv_w), attn_sink[j])
            mix = jnp.concatenate([mix_ctx, mix_lat], axis=0)
            x = _proj_out(mix, attn_w_out[j], attn_b_out[j], g1, x)
        else:
            p = _proj_in(x, norm_mix[l], sc1, sh1, rnn_w_in[j], rnn_b_in[j], tn=PROJ_IN_RNN_COLS)
            args = (rnn_conv_w[j], rnn_conv_b[j], rnn_w_a[j], rnn_b_a[j], rnn_w_x[j], rnn_b_x[j], rnn_lambda[j])
            h0_ctx = jnp.zeros((BATCH, 2, d), F32)
            mix_ctx, st = _rnn_core(p, 0, n_ctx, h0_ctx, *args, n_seq=SUBLANES, seq_len=SEQ)
            mix_lat, _ = _rnn_core(p, n_ctx, DEC_BATCH * DEC_SEQ, state_rglru[:, j], *args,
                                   n_seq=DEC_BATCH, seq_len=DEC_SEQ)
            new_s.append(st)
            mix = jnp.concatenate([mix_ctx, mix_lat], axis=0)
            x = _proj_out(mix, rnn_w_out[j], rnn_b_out[j], g1, x)
        x = _moe_layer(x, norm_ffn[l], sc2, sh2, g2, l, moe_w_router, moe_b_router,
                       moe_w_gate, moe_b_gate, moe_w_up, moe_b_up, moe_w_down, moe_b_down,
                       final_norm, final_norm=(l == DEPTH - 1))
    y = x
    y_prompt = y[:n_ctx].reshape(BATCH, SEQ, d)
    y_sample = y[n_ctx:].reshape(DEC_BATCH, DEC_SEQ, d)
    return (y_prompt, y_sample, jnp.stack(new_k, axis=1), jnp.stack(new_v, axis=1), jnp.stack(new_s, axis=1))
```

```python
import functools
import math

import numpy as np
import jax
import jax.numpy as jnp
from jax import lax
from jax.experimental import pallas as pl
from jax.experimental.pallas import tpu as pltpu

D_MODEL = 2048
BATCH = 32
SEQ = 256
DEPTH = 2
DEC_BATCH = 2
DEC_SEQ = 1024
PAST_LEN = 256
GRID_W = 64
N_HEADS = 16
N_KV_HEADS = 2
HEAD_DIM = 64
WINDOW = 128
ROPE_THETA = 10000.0
N_FOURIER_GROUPS = 4
N_RNN_BLOCKS = 8
CONV_WIDTH = 4
CONV_LEFT = 2
RG_C = 8.0
N_EXPERTS = 32
TOP_K = 4
SWIGLU_LIMIT = 7.0
SWIGLU_ALPHA = 1.702
N_MOD = 6
EPS = 1e-6

LANES = 128
SUBLANES = 8
VMEM_LIMIT_BYTES = 56 * 1024 * 1024

MOE_TILE_ROWS = 1536
MOE_SUB_ROWS = 256
MOE_HIDDEN_BLOCK = 256
MOE_OUT_BLOCK = 256
GATHER_UNROLL = 8
SCAN_ROW_PAD = 8
SCAN_UNROLL = 4
ROW_TILE = 512
PROJ_ROWS = 1024
COMBINE_ROWS = 128
PROJ_OUT_COLS = 512
PROJ_IN_ATTN_COLS = 768
PROJ_IN_RNN_COLS = 1024

F32 = jnp.float32
BF16 = jnp.bfloat16


def _params(semantics):
    return pltpu.CompilerParams(dimension_semantics=semantics, vmem_limit_bytes=VMEM_LIMIT_BYTES)


def _segment_of_tile(i, tile_rows):
    n_ctx_tiles = (BATCH * SEQ) // tile_rows
    tiles_per_latent = DEC_SEQ // tile_rows
    return jnp.where(i < n_ctx_tiles, 0, 1 + (i - n_ctx_tiles) // tiles_per_latent)


def _rms_modulate(x, g, scale, shift):
    ms = jnp.mean(x * x, axis=-1, keepdims=True)
    return (x * lax.rsqrt(ms + EPS) * g) * (1.0 + scale) + shift


def _modulation_kernel(c_ref, w_ref, b_ref, o_ref):
    c = c_ref[...]
    s = (c * jax.nn.sigmoid(c)).astype(BF16)
    o_ref[0] = jnp.dot(s, w_ref[0].astype(BF16), preferred_element_type=F32) + b_ref[0]


def _modulation(cond, w_mod, b_mod):
    d = D_MODEL
    tn = 512
    n_out = N_MOD * d
    return pl.pallas_call(
        _modulation_kernel,
        out_shape=jax.ShapeDtypeStruct((DEPTH, SUBLANES, n_out), F32),
        grid=(DEPTH, n_out // tn),
        in_specs=[
            pl.BlockSpec((SUBLANES, d), lambda l, j: (0, 0)),
            pl.BlockSpec((1, d, tn), lambda l, j: (l, 0, j)),
            pl.BlockSpec((1, 1, tn), lambda l, j: (l, 0, j)),
        ],
        out_specs=pl.BlockSpec((1, SUBLANES, tn), lambda l, j: (l, 0, j)),
        compiler_params=_params(("arbitrary", "arbitrary")),
        name="modulation",
    )(cond, w_mod, b_mod.reshape(DEPTH, 1, n_out))


def _proj_in_kernel(x_ref, g_ref, sc_ref, sh_ref, w_ref, b_ref, o_ref, h_ref):
    @pl.when(pl.program_id(1) == 0)
    def _():
        h_ref[...] = _rms_modulate(x_ref[...], g_ref[...], sc_ref[0], sh_ref[0]).astype(BF16)

    o_ref[...] = jnp.dot(h_ref[...], w_ref[...].astype(BF16), preferred_element_type=F32) + b_ref[...]


def _proj_in(x, g, scale, shift, w, b, tn):
    n, d = x.shape
    n_out = w.shape[1]
    tm = PROJ_ROWS
    seg = lambda i, j: (_segment_of_tile(i, tm), 0, 0)
    return pl.pallas_call(
        _proj_in_kernel,
        out_shape=jax.ShapeDtypeStruct((n, n_out), F32),
        grid=(n // tm, n_out // tn),
        in_specs=[
            pl.BlockSpec((tm, d), lambda i, j: (i, 0)),
            pl.BlockSpec((1, d), lambda i, j: (0, 0)),
            pl.BlockSpec((1, 1, d), seg),
            pl.BlockSpec((1, 1, d), seg),
            pl.BlockSpec((d, tn), lambda i, j: (0, j)),
            pl.BlockSpec((1, tn), lambda i, j: (0, j)),
        ],
        out_specs=pl.BlockSpec((tm, tn), lambda i, j: (i, j)),
        scratch_shapes=[pltpu.VMEM((tm, d), BF16)],
        compiler_params=_params(("arbitrary", "arbitrary")),
        name="proj_in",
    )(x, g.reshape(1, d), scale, shift, w, b.reshape(1, n_out))


def _proj_out_kernel(a_ref, w_ref, b_ref, gate_ref, res_ref, o_ref):
    y = jnp.dot(a_ref[...], w_ref[...].astype(BF16), preferred_element_type=F32) + b_ref[...]
    o_ref[...] = res_ref[...] + gate_ref[0] * y


def _proj_out(a, w, b, gate, res):
    n, k = a.shape
    d = w.shape[1]
    tm, tn = PROJ_ROWS, PROJ_OUT_COLS
    return pl.pallas_call(
        _proj_out_kernel,
        out_shape=jax.ShapeDtypeStruct((n, d), F32),
        grid=(n // tm, d // tn),
        in_specs=[
            pl.BlockSpec((tm, k), lambda i, j: (i, 0)),
            pl.BlockSpec((k, tn), lambda i, j: (0, j)),
            pl.BlockSpec((1, tn), lambda i, j: (0, j)),
            pl.BlockSpec((1, 1, tn), lambda i, j: (_segment_of_tile(i, tm), 0, j)),
            pl.BlockSpec((tm, tn), lambda i, j: (i, j)),
        ],
        out_specs=pl.BlockSpec((tm, tn), lambda i, j: (i, j)),
        compiler_params=_params(("arbitrary", "arbitrary")),
        name="proj_out",
    )(a, w, b.reshape(1, d), gate, res)


def _dot_nt(a, b):
    return lax.dot_general(a, b, (((1,), (1,)), ((), ())), preferred_element_type=F32)


def _head_pair_operands(k, v, group):
    lane = lax.broadcasted_iota(jnp.int32, k.shape, 1)
    low = lane < HEAD_DIM
    k_sw = pltpu.roll(k, HEAD_DIM, 1)
    v_sw = pltpu.roll(v, HEAD_DIM, 1)
    if group == 0:
        kd = jnp.where(low, k, k_sw)
        vd = jnp.where(low, v, v_sw)
    else:
        kd = jnp.where(low, k_sw, k)
        vd = jnp.where(low, v_sw, v)
    v_lo = jnp.where(low, vd, 0.0).astype(BF16)
    v_hi = jnp.where(low, 0.0, vd).astype(BF16)
    return kd.astype(BF16), v_lo, v_hi


def _split_pair(q2):
    lane = lax.broadcasted_iota(jnp.int32, q2.shape, 1)
    low = lane < HEAD_DIM
    qs = q2 * (HEAD_DIM ** -0.5)
    return jnp.where(low, qs, 0.0).astype(BF16), jnp.where(low, 0.0, qs).astype(BF16)


def _softmax_pv(scores, values, sink):
    m = jnp.full((scores[0].shape[0], 1), sink, F32)
    for s in scores:
        m = jnp.maximum(m, jnp.max(s, axis=-1, keepdims=True))
    den = jnp.exp(sink - m)
    ps = []
    for s in scores:
        p = jnp.exp(s - m)
        den = den + jnp.sum(p, axis=-1, keepdims=True)
        ps.append(p)
    out = None
    for p, v in zip(ps, values):
        o = jnp.dot((p / den).astype(BF16), v, preferred_element_type=F32)
        out = o if out is None else out + o
    return out


def _dft_matrices(t):
    idx = np.arange(t)
    ang = 2.0 * np.pi * ((idx[:, None] * idx[None, :]) % t) / t
    m = np.concatenate([np.cos(ang), np.sin(ang)], axis=0) / math.sqrt(t)
    return jnp.asarray(m, dtype=BF16)


def _dft_channel_matrix(c):
    idx = np.arange(c)
    ang = 2.0 * np.pi * ((idx[:, None] * idx[None, :]) % c) / c
    m = np.concatenate([np.cos(ang), -np.sin(ang)], axis=0) / math.sqrt(c)
    return jnp.asarray(m, dtype=BF16)


def _fourier_group(f_g, ts_ref, cs_ref):
    t = f_g.shape[0]
    ab = jnp.dot(ts_ref[...], f_g.astype(BF16), preferred_element_type=F32)
    lhs = jnp.concatenate([ab[:t], ab[t:]], axis=1).astype(BF16)
    return jnp.dot(lhs, cs_ref[...], preferred_element_type=F32)


def _ctx_mixer_kernel(sink_ref, p_ref, ts_ref, cs_ref, o_ref):
    attn_w = N_HEADS * HEAD_DIM
    kv_w = N_KV_HEADS * HEAD_DIM
    pair_w = 2 * HEAD_DIM
    group_heads = N_HEADS // N_KV_HEADS
    k = p_ref[:, attn_w:attn_w + kv_w]
    v = p_ref[:, attn_w + kv_w:attn_w + 2 * kv_w]
    for g in range(N_KV_HEADS):
        kd, v_lo, v_hi = _head_pair_operands(k, v, g)
        for i in range(group_heads // 2):
            pair = g * (group_heads // 2) + i
            q_lo, q_hi = _split_pair(p_ref[:, pair * pair_w:(pair + 1) * pair_w])
            o = _softmax_pv([_dot_nt(q_lo, kd)], [v_lo], sink_ref[2 * pair])
            o = o + _softmax_pv([_dot_nt(q_hi, kd)], [v_hi], sink_ref[2 * pair + 1])
            o_ref[:, pair * pair_w:(pair + 1) * pair_w] = o.astype(o_ref.dtype)
    f0 = attn_w + 2 * kv_w
    fg = (D_MODEL - attn_w) // N_FOURIER_GROUPS
    for g in range(N_FOURIER_GROUPS):
        z = _fourier_group(p_ref[:, f0 + g * fg:f0 + (g + 1) * fg], ts_ref, cs_ref)
        o_ref[:, attn_w + g * fg:attn_w + (g + 1) * fg] = z.astype(o_ref.dtype)


def _ctx_mixer(p, sink):
    n, width = BATCH * SEQ, p.shape[1]
    fg = (D_MODEL - N_HEADS * HEAD_DIM) // N_FOURIER_GROUPS
    return pl.pallas_call(
        _ctx_mixer_kernel,
        out_shape=jax.ShapeDtypeStruct((n, D_MODEL), BF16),
        grid=(n // SEQ,),
        in_specs=[
            pl.BlockSpec(memory_space=pltpu.SMEM),
            pl.BlockSpec((SEQ, width), lambda b: (b, 0)),
            pl.BlockSpec((2 * SEQ, SEQ), lambda b: (0, 0)),
            pl.BlockSpec((2 * fg, fg), lambda b: (0, 0)),
        ],
        out_specs=pl.BlockSpec((SEQ, D_MODEL), lambda b: (b, 0)),
        compiler_params=_params(("arbitrary",)),
        name="ctx_mixer",
    )(sink, p, _dft_matrices(SEQ), _dft_channel_matrix(fg))


def _rope_tables():
    rows = DEC_SEQ // GRID_W
    row = np.repeat(np.arange(rows, dtype=np.float32), GRID_W)
    col = np.tile(np.arange(GRID_W, dtype=np.float32), rows)
    n_freq = HEAD_DIM // 4
    inv = jnp.asarray(ROPE_THETA, F32) ** (-jnp.arange(n_freq, dtype=F32) / n_freq)
    ang = jnp.concatenate([row[:, None] * inv, col[:, None] * inv], axis=-1)
    cos = jnp.repeat(jnp.cos(ang), 2, axis=-1)
    sin = jnp.repeat(jnp.sin(ang), 2, axis=-1)
    sign = jnp.tile(jnp.asarray([-1.0, 1.0], F32), HEAD_DIM // 2)
    return jnp.tile(cos, (1, 2)), jnp.tile(sin * sign, (1, 2))


def _rope(x, cos, sin_signed):
    lane = lax.broadcasted_iota(jnp.int32, x.shape, 1)
    width = x.shape[1]
    partner = jnp.where(lane % 2 == 0, pltpu.roll(x, width - 1, 1), pltpu.roll(x, 1, 1))
    return x * cos + partner * sin_signed


def _lat_mixer_kernel(sink_ref, p_ref, ck_ref, cv_ref, cos_ref, sin_ref, ts_ref, cs_ref, o_ref, q_scr, k_scr):
    attn_w = N_HEADS * HEAD_DIM
    kv_w = N_KV_HEADS * HEAD_DIM
    pair_w = 2 * HEAD_DIM
    group_heads = N_HEADS // N_KV_HEADS
    q_rows = 256
    cos = cos_ref[...]
    sin = sin_ref[...]
    for pair in range(N_HEADS // 2):
        q_scr[:, pair * pair_w:(pair + 1) * pair_w] = _rope(p_ref[:, pair * pair_w:(pair + 1) * pair_w], cos, sin)
    k_scr[...] = _rope(p_ref[:, attn_w:attn_w + kv_w], cos, sin)
    v = p_ref[:, attn_w + kv_w:attn_w + 2 * kv_w]
    ck = ck_ref[0]
    cv = cv_ref[0]
    k = k_scr[...]
    ops = []
    for g in range(N_KV_HEADS):
        ops.append(_head_pair_operands(k, v, g) + _head_pair_operands(ck, cv, g))

    def chunk(c, carry):
        r0 = pl.multiple_of(c * q_rows, q_rows)
        qi = r0 + lax.broadcasted_iota(jnp.int32, (q_rows, DEC_SEQ), 0)
        kj = lax.broadcasted_iota(jnp.int32, (q_rows, DEC_SEQ), 1)
        valid = jnp.abs(qi - kj) <= WINDOW
        for g in range(N_KV_HEADS):
            kd, v_lo, v_hi, ckd, cv_lo, cv_hi = ops[g]
            for i in range(group_heads // 2):
                pair = g * (group_heads // 2) + i
                q_lo, q_hi = _split_pair(q_scr[pl.ds(r0, q_rows), pair * pair_w:(pair + 1) * pair_w])
                s_lo = jnp.where(valid, _dot_nt(q_lo, kd), -jnp.inf)
                o = _softmax_pv([_dot_nt(q_lo, ckd), s_lo], [cv_lo, v_lo], sink_ref[2 * pair])
                s_hi = jnp.where(valid, _dot_nt(q_hi, kd), -jnp.inf)
                o = o + _softmax_pv([_dot_nt(q_hi, ckd), s_hi], [cv_hi, v_hi], sink_ref[2 * pair + 1])
                o_ref[pl.ds(r0, q_rows), pair * pair_w:(pair + 1) * pair_w] = o.astype(o_ref.dtype)
        return carry

    lax.fori_loop(0, DEC_SEQ // q_rows, chunk, 0)
    f0 = attn_w + 2 * kv_w
    fg = (D_MODEL - attn_w) // N_FOURIER_GROUPS
    for g in range(N_FOURIER_GROUPS):
        z = _fourier_group(p_ref[:, f0 + g * fg:f0 + (g + 1) * fg], ts_ref, cs_ref)
        o_ref[:, attn_w + g * fg:attn_w + (g + 1) * fg] = z.astype(o_ref.dtype)


def _lat_mixer(p, cache_k, cache_v, sink):
    n, width = DEC_BATCH * DEC_SEQ, p.shape[1]
    first = (BATCH * SEQ) // DEC_SEQ
    kv_w = N_KV_HEADS * HEAD_DIM
    attn_w = N_HEADS * HEAD_DIM
    fg = (D_MODEL - attn_w) // N_FOURIER_GROUPS
    cos, sin = _rope_tables()
    return pl.pallas_call(
        _lat_mixer_kernel,
        out_shape=jax.ShapeDtypeStruct((n, D_MODEL), BF16),
        grid=(n // DEC_SEQ,),
        in_specs=[
            pl.BlockSpec(memory_space=pltpu.SMEM),
            pl.BlockSpec((DEC_SEQ, width), lambda b: (first + b, 0)),
            pl.BlockSpec((1, PAST_LEN, kv_w), lambda b: (b, 0, 0)),
            pl.BlockSpec((1, PAST_LEN, kv_w), lambda b: (b, 0, 0)),
            pl.BlockSpec((DEC_SEQ, 2 * HEAD_DIM), lambda b: (0, 0)),
            pl.BlockSpec((DEC_SEQ, 2 * HEAD_DIM), lambda b: (0, 0)),
            pl.BlockSpec((2 * DEC_SEQ, DEC_SEQ), lambda b: (0, 0)),
            pl.BlockSpec((2 * fg, fg), lambda b: (0, 0)),
        ],
        out_specs=pl.BlockSpec((DEC_SEQ, D_MODEL), lambda b: (b, 0)),
        scratch_shapes=[pltpu.VMEM((DEC_SEQ, attn_w), F32), pltpu.VMEM((DEC_SEQ, kv_w), F32)],
        compiler_params=_params(("arbitrary",)),
        name="lat_mixer",
    )(sink, p, cache_k, cache_v, cos, sin, _dft_matrices(DEC_SEQ), _dft_channel_matrix(fg))


def _rnn_core_kernel(xr_ref, gr_ref, cw_ref, cb_ref, wa_ref, ba_ref, wx_ref, bx_ref, lam_ref, h0_ref,
                     y_ref, st_ref, a_scr, b_scr, h_scr, *, n_seq, seq_len):
    rows = n_seq * seq_len
    n_lb = xr_ref.shape[1] // LANES
    xr = xr_ref[...]
    t_idx = lax.broadcasted_iota(jnp.int32, (rows, 1), 0) % seq_len
    xc = jnp.broadcast_to(cb_ref[...], xr.shape)
    for tap in range(CONV_WIDTH):
        off = tap - CONV_LEFT
        shifted = xr if off == 0 else pltpu.roll(xr, (-off) % rows, 0)
        valid = (t_idx + off >= 0) & (t_idx + off < seq_len)
        xc = xc + jnp.where(valid, shifted, 0.0) * cw_ref[tap:tap + 1, :]
    xcb = xc.astype(BF16)
    pitch = seq_len + SCAN_ROW_PAD

    def sigmoid(z):
        return 0.5 * jnp.tanh(0.5 * z) + 0.5

    def seq_rows(s):
        return pl.ds(s * pitch, seq_len)

    for d in range(2):
        r = sigmoid(jnp.dot(xcb, wa_ref[d, 0].astype(BF16), preferred_element_type=F32) + ba_ref[d:d + 1, :])
        gi = sigmoid(jnp.dot(xcb, wx_ref[d, 0].astype(BF16), preferred_element_type=F32) + bx_ref[d:d + 1, :])
        neg_lam = -lam_ref[d:d + 1, :]
        softplus = jnp.maximum(neg_lam, 0.0) + jnp.log1p(jnp.exp(-jnp.abs(neg_lam)))
        a = jnp.exp(-RG_C * r * softplus)
        b = jnp.sqrt(1.0 - a * a) * (gi * xc)
        for lb in range(n_lb):
            for s in range(n_seq):
                a_scr[d * n_lb + lb, seq_rows(s), :] = a[s * seq_len:(s + 1) * seq_len, lb * LANES:(lb + 1) * LANES]
                b_scr[d * n_lb + lb, seq_rows(s), :] = b[s * seq_len:(s + 1) * seq_len, lb * LANES:(lb + 1) * LANES]

    def time_rows(t):
        return pl.ds(t, n_seq, stride=pitch)

    def lane_block(ref, k, lb):
        return ref[:, k, lb * LANES:(lb + 1) * LANES]

    def fwd(t, hs):
        out = []
        for lb in range(n_lb):
            h = a_scr[lb, time_rows(t), :] * hs[lb] + b_scr[lb, time_rows(t), :]
            h_scr[lb, time_rows(t), :] = h
            out.append(h)
        return tuple(out)

    hs = lax.fori_loop(0, seq_len, fwd, tuple(lane_block(h0_ref, 0, lb) for lb in range(n_lb)), unroll=SCAN_UNROLL)
    for lb in range(n_lb):
        st_ref[:, 0, lb * LANES:(lb + 1) * LANES] = hs[lb]

    def bwd(i, hs):
        t = seq_len - 1 - i
        out = []
        for lb in range(n_lb):
            h = a_scr[n_lb + lb, time_rows(t), :] * hs[lb] + b_scr[n_lb + lb, time_rows(t), :]
            h_scr[lb, time_rows(t), :] = h_scr[lb, time_rows(t), :] + h
            out.append(h)
        return tuple(out)

    hs = lax.fori_loop(0, seq_len, bwd, tuple(lane_block(h0_ref, 1, lb) for lb in range(n_lb)), unroll=SCAN_UNROLL)
    for lb in range(n_lb):
        st_ref[:, 1, lb * LANES:(lb + 1) * LANES] = hs[lb]
    for lb in range(n_lb):
        for s in range(n_seq):
            gr = gr_ref[s * seq_len:(s + 1) * seq_len, lb * LANES:(lb + 1) * LANES]
            gelu = 0.5 * gr * (1.0 + jnp.tanh(math.sqrt(2.0 / math.pi) * (gr + 0.044715 * (gr * gr * gr))))
            y_ref[s * seq_len:(s + 1) * seq_len, lb * LANES:(lb + 1) * LANES] = (
                gelu * h_scr[lb, seq_rows(s), :]).astype(y_ref.dtype)


def _rnn_core(p, first_row, n, h0, conv_w, conv_b, w_a, b_a, w_x, b_x, lam, n_seq, seq_len):
    d_rnn = p.shape[1] // 2
    cb = d_rnn // N_RNN_BLOCKS
    rows = n_seq * seq_len
    n_batch = n // seq_len
    first = first_row // rows
    scan_rows = n_seq * (seq_len + SCAN_ROW_PAD)
    kern = functools.partial(_rnn_core_kernel, n_seq=n_seq, seq_len=seq_len)
    return pl.pallas_call(
        kern,
        out_shape=(jax.ShapeDtypeStruct((n, d_rnn), BF16), jax.ShapeDtypeStruct((n_batch, 2, d_rnn), F32)),
        grid=(n // rows, N_RNN_BLOCKS),
        in_specs=[
            pl.BlockSpec((rows, cb), lambda i, c: (first + i, c)),
            pl.BlockSpec((rows, cb), lambda i, c: (first + i, N_RNN_BLOCKS + c)),
            pl.BlockSpec((CONV_WIDTH, cb), lambda i, c: (0, c)),
            pl.BlockSpec((1, cb), lambda i, c: (0, c)),
            pl.BlockSpec((2, 1, cb, cb), lambda i, c: (0, c, 0, 0)),
            pl.BlockSpec((2, cb), lambda i, c: (0, c)),
            pl.BlockSpec((2, 1, cb, cb), lambda i, c: (0, c, 0, 0)),
            pl.BlockSpec((2, cb), lambda i, c: (0, c)),
            pl.BlockSpec((2, cb), lambda i, c: (0, c)),
            pl.BlockSpec((n_seq, 2, cb), lambda i, c: (i, 0, c)),
        ],
        out_specs=(
            pl.BlockSpec((rows, cb), lambda i, c: (i, c)),
            pl.BlockSpec((n_seq, 2, cb), lambda i, c: (i, 0, c)),
        ),
        scratch_shapes=[pltpu.VMEM((2 * cb // LANES, scan_rows, LANES), F32),
                        pltpu.VMEM((2 * cb // LANES, scan_rows, LANES), F32),
                        pltpu.VMEM((cb // LANES, scan_rows, LANES), F32)],
        compiler_params=_params(("arbitrary", "arbitrary")),
        name="rnn_core",
    )(p, p, conv_w, conv_b.reshape(1, d_rnn), w_a, b_a, w_x, b_x, lam, h0)


def _pack_bf16_pairs(h):
    half = h.shape[1] // 2
    bits = lax.bitcast_convert_type(h.astype(BF16).astype(F32), jnp.uint32)
    return (bits[:, :half] >> 16) | (bits[:, half:] & jnp.uint32(0xFFFF0000))


def _unpack_bf16_pairs(w):
    lo = lax.bitcast_convert_type(w << 16, F32)
    hi = lax.bitcast_convert_type(w & jnp.uint32(0xFFFF0000), F32)
    return jnp.concatenate([lo, hi], axis=1).astype(BF16)


def _router_kernel(x_ref, g_ref, sc_ref, sh_ref, w_ref, b_ref, h_ref, idx_ref, gate_ref, rank_ref, cnt_ref, run_ref):
    tm = x_ref.shape[0]

    @pl.when(pl.program_id(0) == 0)
    def _():
        run_ref[...] = jnp.zeros_like(run_ref)

    h = _rms_modulate(x_ref[...], g_ref[...], sc_ref[0], sh_ref[0])
    h_ref[...] = _pack_bf16_pairs(h)
    logits =jnp.dot(h, w_ref[...], preferred_element_type=F32, precision=lax.Precision.HIGHEST) + b_ref[...]
    lane = lax.broadcasted_iota(jnp.int32, logits.shape, 1).astype(F32)
    col = lax.broadcasted_iota(jnp.int32, (tm, TOP_K), 1)
    chosen = jnp.zeros(logits.shape, F32)
    top_v, top_i, hits = [], [], []
    work = logits
    for _ in range(TOP_K):
        m = jnp.max(work, axis=-1, keepdims=True)
        first = jnp.min(jnp.where(work == m, lane, float(N_EXPERTS)), axis=-1, keepdims=True)
        hit = lane == first
        work = jnp.where(hit, -jnp.inf, work)
        chosen = jnp.where(hit, 1.0, chosen)
        top_v.append(m)
        top_i.append(first)
        hits.append(hit)
    exps = [jnp.exp(v - top_v[0]) for v in top_v]
    den = exps[0]
    for e in exps[1:]:
        den = den + e
    ri = lax.broadcasted_iota(jnp.int32, (tm, tm), 0)
    ci = lax.broadcasted_iota(jnp.int32, (tm, tm), 1)
    before = (ci < ri).astype(BF16)
    rank_all = jnp.dot(before, chosen.astype(BF16), preferred_element_type=F32) + run_ref[...]
    idx_out = jnp.zeros((tm, TOP_K), F32)
    gate_out = jnp.zeros((tm, TOP_K), F32)
    rank_out = jnp.zeros((tm, TOP_K), F32)
    for k in range(TOP_K):
        rk = jnp.sum(jnp.where(hits[k], rank_all, 0.0), axis=-1, keepdims=True)
        idx_out = jnp.where(col == k, top_i[k], idx_out)
        gate_out = jnp.where(col == k, exps[k] / den, gate_out)
        rank_out = jnp.where(col == k, rk, rank_out)
    idx_ref[...] = idx_out.astype(jnp.int32)
    gate_ref[...] = gate_out
    rank_ref[...] = rank_out.astype(jnp.int32)
    run_ref[...] = run_ref[...] + jnp.sum(chosen, axis=0, keepdims=True)
    cnt_ref[...] = run_ref[...]


def _router(x, g, scale, shift, w, b):
    n, d = x.shape
    e = w.shape[1]
    tm = ROW_TILE
    seg = lambda i: (_segment_of_tile(i, tm), 0, 0)
    small = pl.BlockSpec((tm, TOP_K), lambda i: (i, 0))
    return pl.pallas_call(
        _router_kernel,
        out_shape=(
            jax.ShapeDtypeStruct((n, d // 2), jnp.uint32),
            jax.ShapeDtypeStruct((n, TOP_K), jnp.int32),
            jax.ShapeDtypeStruct((n, TOP_K), F32),
            jax.ShapeDtypeStruct((n, TOP_K), jnp.int32),
            jax.ShapeDtypeStruct((1, e), F32),
        ),
        grid=(n // tm,),
        in_specs=[
            pl.BlockSpec((tm, d), lambda i: (i, 0)),
            pl.BlockSpec((1, d), lambda i: (0, 0)),
            pl.BlockSpec((1, 1, d), seg),
            pl.BlockSpec((1, 1, d), seg),
            pl.BlockSpec((d, e), lambda i: (0, 0)),
            pl.BlockSpec((1, e), lambda i: (0, 0)),
        ],
        out_specs=(pl.BlockSpec((tm, d // 2), lambda i: (i, 0)), small, small, small,
                   pl.BlockSpec((1, e), lambda i: (0, 0))),
        scratch_shapes=[pltpu.VMEM((1, e), F32)],
        compiler_params=_params(("arbitrary",)),
        name="router",
    )(x, g.reshape(1, d), scale, shift, w, b.reshape(1, e))


def _moe_kernel(te_ref, tv_ref, src_ref, h_hbm, wg_ref, wu_ref, wd_ref, bg_ref, bu_ref, bd_ref, o_ref,
                gather_buf, gather_sem, xs_scr, act_scr, wg_scr, wu_scr, wd_scr):
    t = pl.program_id(0)
    j = pl.program_id(1)
    n_tiles = pl.num_programs(0) - 1
    n_j = act_scr.shape[1]
    tile_rows = xs_scr.shape[0]
    n_sub_total = tile_rows // MOE_SUB_ROWS
    ta = jnp.minimum(t, n_tiles - 1)
    tb = jnp.maximum(t - 1, 0)

    def sub_blocks(tile):
        return (tv_ref[tile] + MOE_SUB_ROWS - 1) // MOE_SUB_ROWS

    def sub_rows(r):
        return pl.ds(pl.multiple_of(r * MOE_SUB_ROWS, MOE_SUB_ROWS), MOE_SUB_ROWS)

    def start_gather(tile):
        def issue(i, carry):
            for q in range(GATHER_UNROLL):
                r = i * GATHER_UNROLL + q
                token = src_ref[tile * tile_rows + r]
                pltpu.make_async_copy(h_hbm.at[pl.ds(token, 1), :], gather_buf.at[pl.ds(r, 1), :], gather_sem).start()
            return carry

        lax.fori_loop(0, sub_blocks(tile) * (MOE_SUB_ROWS // GATHER_UNROLL), issue, 0)

    def wait_gather(tile):
        def wait_sub(r, carry):
            pltpu.make_async_copy(h_hbm.at[pl.ds(0, MOE_SUB_ROWS), :], gather_buf.at[sub_rows(r), :], gather_sem).wait()
            return carry

        lax.fori_loop(0, sub_blocks(tile), wait_sub, 0)

    run_a = (t < n_tiles) & (tv_ref[ta] > 0)
    run_b = (t >= 1) & (tv_ref[tb] > 0)
    n_a = jnp.where(run_a, sub_blocks(ta), 0)
    n_b = jnp.where(run_b, sub_blocks(tb), 0)
    slot_a = ta % 2
    slot_b = tb % 2

    @pl.when((j == 0) & run_a)
    def _():
        @pl.when(t == 0)
        def _():
            start_gather(0)

        wait_gather(t)

        def unpack(r, carry):
            xs_scr[sub_rows(r), :] = _unpack_bf16_pairs(gather_buf[sub_rows(r), :])
            return carry

        lax.fori_loop(0, n_a, unpack, 0)

        @pl.when(t + 1 < n_tiles)
        def _():
            start_gather(t + 1)

    @pl.when(run_a)
    def _():
        wg_scr[...] = wg_ref[0, 0].astype(BF16)
        wu_scr[...] = wu_ref[0, 0].astype(BF16)

    @pl.when(run_b)
    def _():
        wd_scr[...] = wd_ref[0, 0].astype(BF16)

    def gate_up(r):
        x = xs_scr[sub_rows(r), :]
        g = jnp.dot(x, wg_scr[...], preferred_element_type=F32) + bg_ref[0, j]
        u = jnp.dot(x, wu_scr[...], preferred_element_type=F32) + bu_ref[0, j]
        g = jnp.minimum(g, SWIGLU_LIMIT)
        u = jnp.clip(u, -SWIGLU_LIMIT, SWIGLU_LIMIT)
        act = (u + 1.0) * (g * jax.nn.sigmoid(SWIGLU_ALPHA * g))
        act_scr[slot_a, j, sub_rows(r), :] = act.astype(BF16)

    def down(r):
        act = jnp.concatenate([act_scr[slot_b, jb, sub_rows(r), :] for jb in range(n_j)], axis=1)
        o_ref[sub_rows(r), :] = jnp.dot(act, wd_scr[...], preferred_element_type=F32) + bd_ref[0, j]

    def for_each_sub_block(n_sub, one):
        def pair(i, carry):
            one(2 * i)
            one(2 * i + 1)
            return carry

        lax.fori_loop(0, n_sub // 2, pair, 0)

        @pl.when(n_sub % 2 == 1)
        def _():
            one(n_sub - 1)

    for_each_sub_block(n_a, gate_up)
    for_each_sub_block(n_b, down)

    @pl.when(t >= 1)
    def _():
        def clear(r, carry):
            o_ref[sub_rows(r), :] = jnp.zeros((MOE_SUB_ROWS, o_ref.shape[1]), o_ref.dtype)
            return carry

        lax.fori_loop(n_b, n_sub_total, clear, 0)


def _moe_experts(h_packed, src, tile_expert, tile_valid, layer, w_gate, b_gate, w_up, b_up, w_down, b_down):
    p_rows = src.shape[0]
    _, e, d, hdim = w_gate.shape
    tm, th, tn = MOE_TILE_ROWS, MOE_HIDDEN_BLOCK, MOE_OUT_BLOCK
    n_tiles = p_rows // tm
    n_j = hdim // th
    n_out = d // tn

    assert n_j == n_out, "gate/up and down halves share the block axis"

    def up_tile(t):
        return jnp.minimum(t, n_tiles - 1)

    def down_tile(t):
        return jnp.maximum(t - 1, 0)

    def up_block(t, j, tv):
        return jnp.where((t < n_tiles) & (tv[up_tile(t)] > 0), j, n_j - 1)

    def down_block(t, j, tv):
        return jnp.where((t >= 1) & (tv[down_tile(t)] > 0), j, n_out - 1)

    grid_spec = pltpu.PrefetchScalarGridSpec(
        num_scalar_prefetch=3,
        grid=(n_tiles + 1, n_j),
        in_specs=[
            pl.BlockSpec(memory_space=pl.ANY),
            pl.BlockSpec((1, 1, d, th), lambda t, j, te, tv, src: (layer, te[up_tile(t)], 0, up_block(t, j, tv))),
            pl.BlockSpec((1, 1, d, th), lambda t, j, te, tv, src: (layer, te[up_tile(t)], 0, up_block(t, j, tv))),
            pl.BlockSpec((1, 1, hdim, tn), lambda t, j, te, tv, src: (layer, te[down_tile(t)], 0, down_block(t, j, tv))),
            pl.BlockSpec((1, n_j, 1, th), lambda t, j, te, tv, src: (te[up_tile(t)], 0, 0, 0)),
            pl.BlockSpec((1, n_j, 1, th), lambda t, j, te, tv, src: (te[up_tile(t)], 0, 0, 0)),
            pl.BlockSpec((1, n_out, 1, tn), lambda t, j, te, tv, src: (te[down_tile(t)], 0, 0, 0)),
        ],
        out_specs=pl.BlockSpec((tm, tn), lambda t, j, te, tv, src: (down_tile(t), jnp.where(t >= 1, j, 0))),
        scratch_shapes=[
            pltpu.VMEM((tm, d // 2), jnp.uint32),
            pltpu.SemaphoreType.DMA(()),
            pltpu.VMEM((tm, d), BF16),
            pltpu.VMEM((2, n_j, tm, th), BF16),
            pltpu.VMEM((d, th), BF16),
            pltpu.VMEM((d, th), BF16),
            pltpu.VMEM((hdim, tn), BF16),
        ],
    )
    return pl.pallas_call(
        _moe_kernel,
        out_shape=jax.ShapeDtypeStruct((p_rows, d), F32),
        grid_spec=grid_spec,
        compiler_params=_params(("arbitrary", "arbitrary")),
        name="moe_experts",
    )(tile_expert, tile_valid, src, h_packed, w_gate, w_up, w_down,
      b_gate[layer].reshape(e, n_j, 1, th), b_up[layer].reshape(e, n_j, 1, th), b_down[layer].reshape(e, n_out, 1, tn))


def _combine_kernel(pos_ref, ys_hbm, gates_ref, x_ref, gate2_ref, norm_ref, o_ref, rows_buf, rows_sem, *, final_norm):
    i = pl.program_id(0)
    tm = x_ref.shape[0]
    slot = i % 2

    def start_gather(tile, slot):
        def issue(i, carry):
            for q in range(GATHER_UNROLL // TOP_K):
                r = i * (GATHER_UNROLL // TOP_K) + q
                for k in range(TOP_K):
                    row = pos_ref[(tile * tm + r) * TOP_K + k]
                    pltpu.make_async_copy(ys_hbm.at[pl.ds(row, 1), :], rows_buf.at[slot, pl.ds(k * tm + r, 1), :],
                                          rows_sem.at[slot]).start()
            return carry

        lax.fori_loop(0, tm // (GATHER_UNROLL // TOP_K), issue, 0)

    @pl.when(i == 0)
    def _():
        start_gather(0, 0)

    @pl.when(i + 1 < pl.num_programs(0))
    def _():
        start_gather(i + 1, 1 - slot)

    pltpu.make_async_copy(ys_hbm.at[pl.ds(0, TOP_K * tm), :], rows_buf.at[slot], rows_sem.at[slot]).wait()
    gates = gates_ref[...]
    moe = gates[:, 0:1] * rows_buf[slot, pl.ds(0, tm), :]
    for k in range(1, TOP_K):
        moe = moe + gates[:, k:k + 1] * rows_buf[slot, pl.ds(k * tm, tm), :]
    x = x_ref[...] + gate2_ref[0] * moe
    if final_norm:
        ms = jnp.mean(x * x, axis=-1, keepdims=True)
        x = x * lax.rsqrt(ms + EPS) * norm_ref[...]
    o_ref[...] = x


def _combine(ys, pos, gates, x, gate2, norm_g, final_norm):
    n, d = x.shape
    tm = COMBINE_ROWS
    grid_spec = pltpu.PrefetchScalarGridSpec(
        num_scalar_prefetch=1,
        grid=(n // tm,),
        in_specs=[
            pl.BlockSpec(memory_space=pl.ANY),
            pl.BlockSpec((tm, TOP_K), lambda i, pos: (i, 0)),
            pl.BlockSpec((tm, d), lambda i, pos: (i, 0)),
            pl.BlockSpec((1, 1, d), lambda i, pos: (_segment_of_tile(i, tm), 0, 0)),
            pl.BlockSpec((1, d), lambda i, pos: (0, 0)),
        ],
        out_specs=pl.BlockSpec((tm, d), lambda i, pos: (i, 0)),
        scratch_shapes=[pltpu.VMEM((2, TOP_K * tm, d), F32), pltpu.SemaphoreType.DMA((2,))],
    )
    return pl.pallas_call(
        functools.partial(_combine_kernel, final_norm=final_norm),
        out_shape=jax.ShapeDtypeStruct((n, d), F32),
        grid_spec=grid_spec,
        compiler_params=_params(("arbitrary",)),
        name="moe_combine",
    )(pos.reshape(-1), ys, gates, x, gate2, norm_g.reshape(1, d))


def _moe_layer(x, g, scale, shift, gate2, layer, w_router, b_router, w_gate, b_gate, w_up, b_up, w_down, b_down,
               norm_g, final_norm):
    n, d = x.shape
    e = w_router.shape[-1]
    tm = MOE_TILE_ROWS
    h, top_i, gates, rank, counts = _router(x, g, scale, shift, w_router[layer], b_router[layer])
    counts = counts[0].astype(jnp.int32)
    padded = ((counts + tm - 1) // tm) * tm
    ends = jnp.cumsum(padded)
    starts = ends - padded
    n_tiles = -(-(n * TOP_K) // tm) + e
    p_rows = n_tiles * tm
    tile_start = jnp.arange(n_tiles, dtype=jnp.int32) * tm
    n_used = ends[-1] // tm
    tile_expert = jnp.minimum(jnp.searchsorted(ends, tile_start, side="right"), e - 1).astype(jnp.int32)
    tile_valid = jnp.clip(counts[tile_expert] - (tile_start - starts[tile_expert]), 0, tm)
    tile_valid = jnp.where(tile_start < ends[-1], tile_valid, 0).astype(jnp.int32)
    last = jnp.maximum(n_used - 1, 0)
    tile_expert = jnp.where(tile_start < ends[-1], tile_expert, tile_expert[last]).astype(jnp.int32)
    pos = starts[top_i] + rank
    token = jnp.broadcast_to(jnp.arange(n, dtype=jnp.int32)[:, None], (n, TOP_K))
    src = jnp.zeros((p_rows,), jnp.int32).at[pos.reshape(-1)].set(token.reshape(-1))
    ys = _moe_experts(h, src, tile_expert, tile_valid, layer, w_gate, b_gate, w_up, b_up, w_down, b_down)
    return _combine(ys, pos, gates, x, gate2, norm_g, final_norm)


def kernel(x_prompt, x_sample, cache_k, cache_v, state_rglru, c, c_ctx, norm_mix, norm_ffn, w_mod, b_mod,
           attn_w_in, attn_b_in, attn_w_out, attn_b_out, attn_sink,
           rnn_w_in, rnn_b_in, rnn_conv_w, rnn_conv_b, rnn_w_a, rnn_b_a, rnn_w_x, rnn_b_x,
           rnn_lambda, rnn_w_out, rnn_b_out,
           moe_w_router, moe_b_router, moe_w_gate, moe_b_gate, moe_w_up, moe_b_up,
           moe_w_down, moe_b_down, final_norm):
    d = D_MODEL
    n_ctx = BATCH * SEQ
    attn_w = N_HEADS * HEAD_DIM
    kv_w = N_KV_HEADS * HEAD_DIM
    x = jnp.concatenate([x_prompt.reshape(n_ctx, d), x_sample.reshape(DEC_BATCH * DEC_SEQ, d)], axis=0)
    cond = jnp.concatenate([c_ctx[None, :], c, jnp.zeros((SUBLANES - 1 - DEC_BATCH, d), F32)], axis=0)
    mods = _modulation(cond, w_mod, b_mod)

    def mod(l, k):
        return mods[l, :, k * d:(k + 1) * d].reshape(SUBLANES, 1, d)

    new_k, new_v, new_s = [], [], []
    for l in range(DEPTH):
        j = l // 2
        sh1, sc1, g1, sh2, sc2, g2 = [mod(l, k) for k in range(N_MOD)]
        if l % 2 == 0:
            p = _proj_in(x, norm_mix[l], sc1, sh1, attn_w_in[j], attn_b_in[j], tn=PROJ_IN_ATTN_COLS)
            new_k.append(p[:n_ctx, attn_w:attn_w + kv_w].reshape(BATCH, SEQ, N_KV_HEADS, HEAD_DIM))
            new_v.append(p[:n_ctx, attn_w + kv_w:attn_w + 2 * kv_w].reshape(BATCH, SEQ, N_KV_HEADS, HEAD_DIM))
            mix_ctx = _ctx_mixer(p, attn_sink[j])
            mix_lat = _lat_mixer(p, cache_k[:, j].reshape(DEC_BATCH, PAST_LEN, kv_w),
                                 cache_v[:, j].reshape(DEC_BATCH, PAST_LEN, kv_w), attn_sink[j])
            mix = jnp.concatenate([mix_ctx, mix_lat], axis=0)
            x = _proj_out(mix, attn_w_out[j], attn_b_out[j], g1, x)
        else:
            p = _proj_in(x, norm_mix[l], sc1, sh1, rnn_w_in[j], rnn_b_in[j], tn=PROJ_IN_RNN_COLS)
            args = (rnn_conv_w[j], rnn_conv_b[j], rnn_w_a[j], rnn_b_a[j], rnn_w_x[j], rnn_b_x[j], rnn_lambda[j])
            h0_ctx = jnp.zeros((BATCH, 2, d), F32)
            mix_ctx, st = _rnn_core(p, 0, n_ctx, h0_ctx, *args, n_seq=SUBLANES, seq_len=SEQ)
            mix_lat, _ = _rnn_core(p, n_ctx, DEC_BATCH * DEC_SEQ, state_rglru[:, j], *args,
                                   n_seq=DEC_BATCH, seq_len=DEC_SEQ)
            new_s.append(st)
            mix = jnp.concatenate([mix_ctx, mix_lat], axis=0)
            x = _proj_out(mix, rnn_w_out[j], rnn_b_out[j], g1, x)
        x = _moe_layer(x, norm_ffn[l], sc2, sh2, g2, l, moe_w_router, moe_b_router,
                       moe_w_gate, moe_b_gate, moe_w_up, moe_b_up, moe_w_down, moe_b_down,
                       final_norm, final_norm=(l == DEPTH - 1))
    y = x
    y_prompt = y[:n_ctx].reshape(BATCH, SEQ, d)
    y_sample = y[n_ctx:].reshape(DEC_BATCH, DEC_SEQ, d)
    return (y_prompt, y_sample, jnp.stack(new_k, axis=1), jnp.stack(new_v, axis=1), jnp.stack(new_s, axis=1))
```

```python
import functools
import math

import numpy as np
import jax
import jax.numpy as jnp
from jax import lax
from jax.experimental import pallas as pl
from jax.experimental.pallas import tpu as pltpu

D_MODEL = 2048
BATCH = 32
SEQ = 256
DEPTH = 2
DEC_BATCH = 2
DEC_SEQ = 1024
PAST_LEN = 256
GRID_W = 64
N_HEADS = 16
N_KV_HEADS = 2
HEAD_DIM = 64
WINDOW = 128
ROPE_THETA = 10000.0
N_FOURIER_GROUPS = 4
N_RNN_BLOCKS = 8
CONV_WIDTH = 4
CONV_LEFT = 2
RG_C = 8.0
N_EXPERTS = 32
TOP_K = 4
SWIGLU_LIMIT = 7.0
SWIGLU_ALPHA = 1.702
N_MOD = 6
EPS = 1e-6

LANES = 128
SUBLANES = 8
VMEM_LIMIT_BYTES = 56 * 1024 * 1024

MOE_TILE_ROWS = 1536
MOE_SUB_ROWS = 256
MOE_HIDDEN_BLOCK = 256
MOE_OUT_BLOCK = 256
GATHER_UNROLL = 8
SCAN_ROW_PAD = 8
SCAN_UNROLL = 4
ROW_TILE = 512
PROJ_ROWS = 1024
COMBINE_ROWS = 128
PROJ_OUT_COLS = 512
PROJ_IN_ATTN_COLS = 768
PROJ_IN_RNN_COLS = 1024

F32 = jnp.float32
BF16 = jnp.bfloat16


def _params(semantics):
    return pltpu.CompilerParams(dimension_semantics=semantics, vmem_limit_bytes=VMEM_LIMIT_BYTES)


def _segment_of_tile(i, tile_rows):
    n_ctx_tiles = (BATCH * SEQ) // tile_rows
    tiles_per_latent = DEC_SEQ // tile_rows
    return jnp.where(i < n_ctx_tiles, 0, 1 + (i - n_ctx_tiles) // tiles_per_latent)


def _rms_modulate(x, g, scale, shift):
    ms = jnp.mean(x * x, axis=-1, keepdims=True)
    return (x * lax.rsqrt(ms + EPS) * g) * (1.0 + scale) + shift


def _modulation_kernel(c_ref, w_ref, b_ref, o_ref):
    c = c_ref[...]
    s = (c * jax.nn.sigmoid(c)).astype(BF16)
    o_ref[0] = jnp.dot(s, w_ref[0].astype(BF16), preferred_element_type=F32) + b_ref[0]


def _modulation(cond, w_mod, b_mod):
    d = D_MODEL
    tn = 512
    n_out = N_MOD * d
    return pl.pallas_call(
        _modulation_kernel,
        out_shape=jax.ShapeDtypeStruct((DEPTH, SUBLANES, n_out), F32),
        grid=(DEPTH, n_out // tn),
        in_specs=[
            pl.BlockSpec((SUBLANES, d), lambda l, j: (0, 0)),
            pl.BlockSpec((1, d, tn), lambda l, j: (l, 0, j)),
            pl.BlockSpec((1, 1, tn), lambda l, j: (l, 0, j)),
        ],
        out_specs=pl.BlockSpec((1, SUBLANES, tn), lambda l, j: (l, 0, j)),
        compiler_params=_params(("arbitrary", "arbitrary")),
        name="modulation",
    )(cond, w_mod, b_mod.reshape(DEPTH, 1, n_out))


def _proj_in_kernel(x_ref, g_ref, sc_ref, sh_ref, w_ref, b_ref, o_ref, h_ref):
    @pl.when(pl.program_id(1) == 0)
    def _():
        h_ref[...] = _rms_modulate(x_ref[...], g_ref[...], sc_ref[0], sh_ref[0]).astype(BF16)

    o_ref[...] = jnp.dot(h_ref[...], w_ref[...].astype(BF16), preferred_element_type=F32) + b_ref[...]


def _proj_in(x, g, scale, shift, w, b, tn):
    n, d = x.shape
    n_out = w.shape[1]
    tm = PROJ_ROWS
    seg = lambda i, j: (_segment_of_tile(i, tm), 0, 0)
    return pl.pallas_call(
        _proj_in_kernel,
        out_shape=jax.ShapeDtypeStruct((n, n_out), F32),
        grid=(n // tm, n_out // tn),
        in_specs=[
            pl.BlockSpec((tm, d), lambda i, j: (i, 0)),
            pl.BlockSpec((1, d), lambda i, j: (0, 0)),
            pl.BlockSpec((1, 1, d), seg),
            pl.BlockSpec((1, 1, d), seg),
            pl.BlockSpec((d, tn), lambda i, j: (0, j)),
            pl.BlockSpec((1, tn), lambda i, j: (0, j)),
        ],
        out_specs=pl.BlockSpec((tm, tn), lambda i, j: (i, j)),
        scratch_shapes=[pltpu.VMEM((tm, d), BF16)],
        compiler_params=_params(("arbitrary", "arbitrary")),
        name="proj_in",
    )(x, g.reshape(1, d), scale, shift, w, b.reshape(1, n_out))


def _proj_out_kernel(a_ctx_ref, a_lat_ref, w_ref, b_ref, gate_ref, res_ref, o_ref, *, n_ctx_tiles):
    def project(a_ref):
        y = jnp.dot(a_ref[...], w_ref[...].astype(BF16), preferred_element_type=F32) + b_ref[...]
        o_ref[...] = res_ref[...] + gate_ref[0] * y

    @pl.when(pl.program_id(0) < n_ctx_tiles)
    def _():
        project(a_ctx_ref)

    @pl.when(pl.program_id(0) >= n_ctx_tiles)
    def _():
        project(a_lat_ref)


def _proj_out(a_ctx, a_lat, w, b, gate, res):
    n, d = res.shape
    k = a_ctx.shape[1]
    tm, tn = PROJ_ROWS, PROJ_OUT_COLS
    n_ctx_tiles = a_ctx.shape[0] // tm
    return pl.pallas_call(
        functools.partial(_proj_out_kernel, n_ctx_tiles=n_ctx_tiles),
        out_shape=jax.ShapeDtypeStruct((n, d), F32),
        grid=(n // tm, d // tn),
        in_specs=[
            pl.BlockSpec((tm, k), lambda i, j: (jnp.minimum(i, n_ctx_tiles - 1), 0)),
            pl.BlockSpec((tm, k), lambda i, j: (jnp.maximum(i - n_ctx_tiles, 0), 0)),
            pl.BlockSpec((k, tn), lambda i, j: (0, j)),
            pl.BlockSpec((1, tn), lambda i, j: (0, j)),
            pl.BlockSpec((1, 1, tn), lambda i, j: (_segment_of_tile(i, tm), 0, j)),
            pl.BlockSpec((tm, tn), lambda i, j: (i, j)),
        ],
        out_specs=pl.BlockSpec((tm, tn), lambda i, j: (i, j)),
        compiler_params=_params(("arbitrary", "arbitrary")),
        name="proj_out",
    )(a_ctx, a_lat, w, b.reshape(1, d), gate, res)


def _dot_nt(a, b):
    return lax.dot_general(a, b, (((1,), (1,)), ((), ())), preferred_element_type=F32)


def _head_pair_operands(k, v, group):
    lane = lax.broadcasted_iota(jnp.int32, k.shape, 1)
    low = lane < HEAD_DIM
    k_sw = pltpu.roll(k, HEAD_DIM, 1)
    v_sw = pltpu.roll(v, HEAD_DIM, 1)
    if group == 0:
        kd = jnp.where(low, k, k_sw)
        vd = jnp.where(low, v, v_sw)
    else:
        kd = jnp.where(low, k_sw, k)
        vd = jnp.where(low, v_sw, v)
    v_lo = jnp.where(low, vd, 0.0).astype(BF16)
    v_hi = jnp.where(low, 0.0, vd).astype(BF16)
    return kd.astype(BF16), v_lo, v_hi


def _split_pair(q2):
    lane = lax.broadcasted_iota(jnp.int32, q2.shape, 1)
    low = lane < HEAD_DIM
    qs = q2 * (HEAD_DIM ** -0.5)
    return jnp.where(low, qs, 0.0).astype(BF16), jnp.where(low, 0.0, qs).astype(BF16)


def _softmax_pv(scores, values, sink):
    m = jnp.full((scores[0].shape[0], 1), sink, F32)
    for s in scores:
        m = jnp.maximum(m, jnp.max(s, axis=-1, keepdims=True))
    den = jnp.exp(sink - m)
    ps = []
    for s in scores:
        p = jnp.exp(s - m)
        den = den + jnp.sum(p, axis=-1, keepdims=True)
        ps.append(p)
    out = None
    for p, v in zip(ps, values):
        o = jnp.dot((p / den).astype(BF16), v, preferred_element_type=F32)
        out = o if out is None else out + o
    return out


def _dft_matrices(t):
    idx = np.arange(t)
    ang = 2.0 * np.pi * ((idx[:, None] * idx[None, :]) % t) / t
    m = np.concatenate([np.cos(ang), np.sin(ang)], axis=0) / math.sqrt(t)
    return jnp.asarray(m, dtype=BF16)


def _dft_channel_matrix(c):
    idx = np.arange(c)
    ang = 2.0 * np.pi * ((idx[:, None] * idx[None, :]) % c) / c
    m = np.concatenate([np.cos(ang), -np.sin(ang)], axis=0) / math.sqrt(c)
    return jnp.asarray(m, dtype=BF16)


def _fourier_group(f_g, ts_ref, cs_ref):
    t = f_g.shape[0]
    ab = jnp.dot(ts_ref[...], f_g.astype(BF16), preferred_element_type=F32)
    lhs = jnp.concatenate([ab[:t], ab[t:]], axis=1).astype(BF16)
    return jnp.dot(lhs, cs_ref[...], preferred_element_type=F32)


def _ctx_mixer_kernel(sink_ref, p_ref, ts_ref, cs_ref, o_ref):
    attn_w = N_HEADS * HEAD_DIM
    kv_w = N_KV_HEADS * HEAD_DIM
    pair_w = 2 * HEAD_DIM
    group_heads = N_HEADS // N_KV_HEADS
    k = p_ref[:, attn_w:attn_w + kv_w]
    v = p_ref[:, attn_w + kv_w:attn_w + 2 * kv_w]
    for g in range(N_KV_HEADS):
        kd, v_lo, v_hi = _head_pair_operands(k, v, g)
        for i in range(group_heads // 2):
            pair = g * (group_heads // 2) + i
            q_lo, q_hi = _split_pair(p_ref[:, pair * pair_w:(pair + 1) * pair_w])
            o = _softmax_pv([_dot_nt(q_lo, kd)], [v_lo], sink_ref[2 * pair])
            o = o + _softmax_pv([_dot_nt(q_hi, kd)], [v_hi], sink_ref[2 * pair + 1])
            o_ref[:, pair * pair_w:(pair + 1) * pair_w] = o.astype(o_ref.dtype)
    f0 = attn_w + 2 * kv_w
    fg = (D_MODEL - attn_w) // N_FOURIER_GROUPS
    for g in range(N_FOURIER_GROUPS):
        z = _fourier_group(p_ref[:, f0 + g * fg:f0 + (g + 1) * fg], ts_ref, cs_ref)
        o_ref[:, attn_w + g * fg:attn_w + (g + 1) * fg] = z.astype(o_ref.dtype)


def _ctx_mixer(p, sink):
    n, width = BATCH * SEQ, p.shape[1]
    fg = (D_MODEL - N_HEADS * HEAD_DIM) // N_FOURIER_GROUPS
    return pl.pallas_call(
        _ctx_mixer_kernel,
        out_shape=jax.ShapeDtypeStruct((n, D_MODEL), BF16),
        grid=(n // SEQ,),
        in_specs=[
            pl.BlockSpec(memory_space=pltpu.SMEM),
            pl.BlockSpec((SEQ, width), lambda b: (b, 0)),
            pl.BlockSpec((2 * SEQ, SEQ), lambda b: (0, 0)),
            pl.BlockSpec((2 * fg, fg), lambda b: (0, 0)),
        ],
        out_specs=pl.BlockSpec((SEQ, D_MODEL), lambda b: (b, 0)),
        compiler_params=_params(("arbitrary",)),
        name="ctx_mixer",
    )(sink, p, _dft_matrices(SEQ), _dft_channel_matrix(fg))


def _rope_tables():
    rows = DEC_SEQ // GRID_W
    row = np.repeat(np.arange(rows, dtype=np.float32), GRID_W)
    col = np.tile(np.arange(GRID_W, dtype=np.float32), rows)
    n_freq = HEAD_DIM // 4
    inv = jnp.asarray(ROPE_THETA, F32) ** (-jnp.arange(n_freq, dtype=F32) / n_freq)
    ang = jnp.concatenate([row[:, None] * inv, col[:, None] * inv], axis=-1)
    cos = jnp.repeat(jnp.cos(ang), 2, axis=-1)
    sin = jnp.repeat(jnp.sin(ang), 2, axis=-1)
    sign = jnp.tile(jnp.asarray([-1.0, 1.0], F32), HEAD_DIM // 2)
    return jnp.tile(cos, (1, 2)), jnp.tile(sin * sign, (1, 2))


def _rope(x, cos, sin_signed):
    lane = lax.broadcasted_iota(jnp.int32, x.shape, 1)
    width = x.shape[1]
    partner = jnp.where(lane % 2 == 0, pltpu.roll(x, width - 1, 1), pltpu.roll(x, 1, 1))
    return x * cos + partner * sin_signed


def _lat_mixer_kernel(sink_ref, p_ref, ck_ref, cv_ref, cos_ref, sin_ref, ts_ref, cs_ref, o_ref, q_scr, k_scr):
    attn_w = N_HEADS * HEAD_DIM
    kv_w = N_KV_HEADS * HEAD_DIM
    pair_w = 2 * HEAD_DIM
    group_heads = N_HEADS // N_KV_HEADS
    q_rows = 256
    cos = cos_ref[...]
    sin = sin_ref[...]
    for pair in range(N_HEADS // 2):
        q_scr[:, pair * pair_w:(pair + 1) * pair_w] = _rope(p_ref[:, pair * pair_w:(pair + 1) * pair_w], cos, sin)
    k_scr[...] = _rope(p_ref[:, attn_w:attn_w + kv_w], cos, sin)
    v = p_ref[:, attn_w + kv_w:attn_w + 2 * kv_w]
    ck = ck_ref[0]
    cv = cv_ref[0]
    k = k_scr[...]
    ops = []
    for g in range(N_KV_HEADS):
        ops.append(_head_pair_operands(k, v, g) + _head_pair_operands(ck, cv, g))

    def chunk(c, carry):
        r0 = pl.multiple_of(c * q_rows, q_rows)
        qi = r0 + lax.broadcasted_iota(jnp.int32, (q_rows, DEC_SEQ), 0)
        kj = lax.broadcasted_iota(jnp.int32, (q_rows, DEC_SEQ), 1)
        valid = jnp.abs(qi - kj) <= WINDOW
        for g in range(N_KV_HEADS):
            kd, v_lo, v_hi, ckd, cv_lo, cv_hi = ops[g]
            for i in range(group_heads // 2):
                pair = g * (group_heads // 2) + i
                q_lo, q_hi = _split_pair(q_scr[pl.ds(r0, q_rows), pair * pair_w:(pair + 1) * pair_w])
                s_lo = jnp.where(valid, _dot_nt(q_lo, kd), -jnp.inf)
                o = _softmax_pv([_dot_nt(q_lo, ckd), s_lo], [cv_lo, v_lo], sink_ref[2 * pair])
                s_hi = jnp.where(valid, _dot_nt(q_hi, kd), -jnp.inf)
                o = o + _softmax_pv([_dot_nt(q_hi, ckd), s_hi], [cv_hi, v_hi], sink_ref[2 * pair + 1])
                o_ref[pl.ds(r0, q_rows), pair * pair_w:(pair + 1) * pair_w] = o.astype(o_ref.dtype)
        return carry

    lax.fori_loop(0, DEC_SEQ // q_rows, chunk, 0)
    f0 = attn_w + 2 * kv_w
    fg = (D_MODEL - attn_w) // N_FOURIER_GROUPS
    for g in range(N_FOURIER_GROUPS):
        z = _fourier_group(p_ref[:, f0 + g * fg:f0 + (g + 1) * fg], ts_ref, cs_ref)
        o_ref[:, attn_w + g * fg:attn_w + (g + 1) * fg] = z.astype(o_ref.dtype)


def _lat_mixer(p, cache_k, cache_v, sink):
    n, width = DEC_BATCH * DEC_SEQ, p.shape[1]
    first = (BATCH * SEQ) // DEC_SEQ
    kv_w = N_KV_HEADS * HEAD_DIM
    attn_w = N_HEADS * HEAD_DIM
    fg = (D_MODEL - attn_w) // N_FOURIER_GROUPS
    cos, sin = _rope_tables()
    return pl.pallas_call(
        _lat_mixer_kernel,
        out_shape=jax.ShapeDtypeStruct((n, D_MODEL), BF16),
        grid=(n // DEC_SEQ,),
        in_specs=[
            pl.BlockSpec(memory_space=pltpu.SMEM),
            pl.BlockSpec((DEC_SEQ, width), lambda b: (first + b, 0)),
            pl.BlockSpec((1, PAST_LEN, kv_w), lambda b: (b, 0, 0)),
            pl.BlockSpec((1, PAST_LEN, kv_w), lambda b: (b, 0, 0)),
            pl.BlockSpec((DEC_SEQ, 2 * HEAD_DIM), lambda b: (0, 0)),
            pl.BlockSpec((DEC_SEQ, 2 * HEAD_DIM), lambda b: (0, 0)),
            pl.BlockSpec((2 * DEC_SEQ, DEC_SEQ), lambda b: (0, 0)),
            pl.BlockSpec((2 * fg, fg), lambda b: (0, 0)),
        ],
        out_specs=pl.BlockSpec((DEC_SEQ, D_MODEL), lambda b: (b, 0)),
        scratch_shapes=[pltpu.VMEM((DEC_SEQ, attn_w), F32), pltpu.VMEM((DEC_SEQ, kv_w), F32)],
        compiler_params=_params(("arbitrary",)),
        name="lat_mixer",
    )(sink, p, cache_k, cache_v, cos, sin, _dft_matrices(DEC_SEQ), _dft_channel_matrix(fg))


def _rnn_core_kernel(xr_ref, gr_ref, cw_ref, cb_ref, wa_ref, ba_ref, wx_ref, bx_ref, lam_ref, h0_ref,
                     y_ref, st_ref, a_scr, b_scr, h_scr, *, n_seq, seq_len):
    rows = n_seq * seq_len
    n_lb = xr_ref.shape[1] // LANES
    xr = xr_ref[...]
    t_idx = lax.broadcasted_iota(jnp.int32, (rows, 1), 0) % seq_len
    xc = jnp.broadcast_to(cb_ref[...], xr.shape)
    for tap in range(CONV_WIDTH):
        off = tap - CONV_LEFT
        shifted = xr if off == 0 else pltpu.roll(xr, (-off) % rows, 0)
        valid = (t_idx + off >= 0) & (t_idx + off < seq_len)
        xc = xc + jnp.where(valid, shifted, 0.0) * cw_ref[tap:tap + 1, :]
    xcb = xc.astype(BF16)
    pitch = seq_len + SCAN_ROW_PAD

    def sigmoid(z):
        return 0.5 * jnp.tanh(0.5 * z) + 0.5

    def seq_rows(s):
        return pl.ds(s * pitch, seq_len)

    for d in range(2):
        r = sigmoid(jnp.dot(xcb, wa_ref[d, 0].astype(BF16), preferred_element_type=F32) + ba_ref[d:d + 1, :])
        gi = sigmoid(jnp.dot(xcb, wx_ref[d, 0].astype(BF16), preferred_element_type=F32) + bx_ref[d:d + 1, :])
        neg_lam = -lam_ref[d:d + 1, :]
        softplus = jnp.maximum(neg_lam, 0.0) + jnp.log1p(jnp.exp(-jnp.abs(neg_lam)))
        a = jnp.exp(-RG_C * r * softplus)
        b = jnp.sqrt(1.0 - a * a) * (gi * xc)
        for lb in range(n_lb):
            for s in range(n_seq):
                a_scr[d * n_lb + lb, seq_rows(s), :] = a[s * seq_len:(s + 1) * seq_len, lb * LANES:(lb + 1) * LANES]
                b_scr[d * n_lb + lb, seq_rows(s), :] = b[s * seq_len:(s + 1) * seq_len, lb * LANES:(lb + 1) * LANES]

    def time_rows(t):
        return pl.ds(t, n_seq, stride=pitch)

    def lane_block(ref, k, lb):
        return ref[:, k, lb * LANES:(lb + 1) * LANES]

    def fwd(t, hs):
        out = []
        for lb in range(n_lb):
            h = a_scr[lb, time_rows(t), :] * hs[lb] + b_scr[lb, time_rows(t), :]
            h_scr[lb, time_rows(t), :] = h
            out.append(h)
        return tuple(out)

    hs = lax.fori_loop(0, seq_len, fwd, tuple(lane_block(h0_ref, 0, lb) for lb in range(n_lb)), unroll=SCAN_UNROLL)
    for lb in range(n_lb):
        st_ref[:, 0, lb * LANES:(lb + 1) * LANES] = hs[lb]

    def bwd(i, hs):
        t = seq_len - 1 - i
        out = []
        for lb in range(n_lb):
            h = a_scr[n_lb + lb, time_rows(t), :] * hs[lb] + b_scr[n_lb + lb, time_rows(t), :]
            h_scr[lb, time_rows(t), :] = h_scr[lb, time_rows(t), :] + h
            out.append(h)
        return tuple(out)

    hs = lax.fori_loop(0, seq_len, bwd, tuple(lane_block(h0_ref, 1, lb) for lb in range(n_lb)), unroll=SCAN_UNROLL)
    for lb in range(n_lb):
        st_ref[:, 1, lb * LANES:(lb + 1) * LANES] = hs[lb]
    for lb in range(n_lb):
        for s in range(n_seq):
            gr = gr_ref[s * seq_len:(s + 1) * seq_len, lb * LANES:(lb + 1) * LANES]
            gelu = 0.5 * gr * (1.0 + jnp.tanh(math.sqrt(2.0 / math.pi) * (gr + 0.044715 * (gr * gr * gr))))
            y_ref[s * seq_len:(s + 1) * seq_len, lb * LANES:(lb + 1) * LANES] = (
                gelu * h_scr[lb, seq_rows(s), :]).astype(y_ref.dtype)


def _rnn_core(p, first_row, n, h0, conv_w, conv_b, w_a, b_a, w_x, b_x, lam, n_seq, seq_len):
    d_rnn = p.shape[1] // 2
    cb = d_rnn // N_RNN_BLOCKS
    rows = n_seq * seq_len
    n_batch = n // seq_len
    first = first_row // rows
    scan_rows = n_seq * (seq_len + SCAN_ROW_PAD)
    kern = functools.partial(_rnn_core_kernel, n_seq=n_seq, seq_len=seq_len)
    return pl.pallas_call(
        kern,
        out_shape=(jax.ShapeDtypeStruct((n, d_rnn), BF16), jax.ShapeDtypeStruct((n_batch, 2, d_rnn), F32)),
        grid=(n // rows, N_RNN_BLOCKS),
        in_specs=[
            pl.BlockSpec((rows, cb), lambda i, c: (first + i, c)),
            pl.BlockSpec((rows, cb), lambda i, c: (first + i, N_RNN_BLOCKS + c)),
            pl.BlockSpec((CONV_WIDTH, cb), lambda i, c: (0, c)),
            pl.BlockSpec((1, cb), lambda i, c: (0, c)),
            pl.BlockSpec((2, 1, cb, cb), lambda i, c: (0, c, 0, 0)),
            pl.BlockSpec((2, cb), lambda i, c: (0, c)),
            pl.BlockSpec((2, 1, cb, cb), lambda i, c: (0, c, 0, 0)),
            pl.BlockSpec((2, cb), lambda i, c: (0, c)),
            pl.BlockSpec((2, cb), lambda i, c: (0, c)),
            pl.BlockSpec((n_seq, 2, cb), lambda i, c: (i, 0, c)),
        ],
        out_specs=(
            pl.BlockSpec((rows, cb), lambda i, c: (i, c)),
            pl.BlockSpec((n_seq, 2, cb), lambda i, c: (i, 0, c)),
        ),
        scratch_shapes=[pltpu.VMEM((2 * cb // LANES, scan_rows, LANES), F32),
                        pltpu.VMEM((2 * cb // LANES, scan_rows, LANES), F32),
                        pltpu.VMEM((cb // LANES, scan_rows, LANES), F32)],
        compiler_params=_params(("arbitrary", "arbitrary")),
        name="rnn_core",
    )(p, p, conv_w, conv_b.reshape(1, d_rnn), w_a, b_a, w_x, b_x, lam, h0)


def _pack_bf16_pairs(h):
    half = h.shape[1] // 2
    bits = lax.bitcast_convert_type(h.astype(BF16).astype(F32), jnp.uint32)
    return (bits[:, :half] >> 16) | (bits[:, half:] & jnp.uint32(0xFFFF0000))


def _unpack_bf16_pairs(w):
    lo = lax.bitcast_convert_type(w << 16, F32)
    hi = lax.bitcast_convert_type(w & jnp.uint32(0xFFFF0000), F32)
    return lo.astype(BF16), hi.astype(BF16)


def _router_kernel(x_ref, g_ref, sc_ref, sh_ref, w_ref, b_ref, h_ref, idx_ref, gate_ref, rank_ref, cnt_ref, run_ref):
    tm = x_ref.shape[0]

    @pl.when(pl.program_id(0) == 0)
    def _():
        run_ref[...] = jnp.zeros_like(run_ref)

    h = _rms_modulate(x_ref[...], g_ref[...], sc_ref[0], sh_ref[0])
    packed = _pack_bf16_pairs(h)
    lane_rows = packed.shape[1] // LANES
    for i in range(lane_rows):
        h_ref[pl.ds(i, tm, stride=lane_rows), :] = packed[:, i * LANES:(i + 1) * LANES]
    logits =jnp.dot(h, w_ref[...], preferred_element_type=F32, precision=lax.Precision.HIGHEST) + b_ref[...]
    lane = lax.broadcasted_iota(jnp.int32, logits.shape, 1).astype(F32)
    col = lax.broadcasted_iota(jnp.int32, (tm, TOP_K), 1)
    chosen = jnp.zeros(logits.shape, F32)
    top_v, top_i, hits = [], [], []
    work = logits
    for _ in range(TOP_K):
        m = jnp.max(work, axis=-1, keepdims=True)
        first = jnp.min(jnp.where(work == m, lane, float(N_EXPERTS)), axis=-1, keepdims=True)
        hit = lane == first
        work = jnp.where(hit, -jnp.inf, work)
        chosen = jnp.where(hit, 1.0, chosen)
        top_v.append(m)
        top_i.append(first)
        hits.append(hit)
    exps = [jnp.exp(v - top_v[0]) for v in top_v]
    den = exps[0]
    for e in exps[1:]:
        den = den + e
    ri = lax.broadcasted_iota(jnp.int32, (tm, tm), 0)
    ci = lax.broadcasted_iota(jnp.int32, (tm, tm), 1)
    before = (ci < ri).astype(BF16)
    rank_all = jnp.dot(before, chosen.astype(BF16), preferred_element_type=F32) + run_ref[...]
    idx_out = jnp.zeros((tm, TOP_K), F32)
    gate_out = jnp.zeros((tm, TOP_K), F32)
    rank_out = jnp.zeros((tm, TOP_K), F32)
    for k in range(TOP_K):
        rk = jnp.sum(jnp.where(hits[k], rank_all, 0.0), axis=-1, keepdims=True)
        idx_out = jnp.where(col == k, top_i[k], idx_out)
        gate_out = jnp.where(col == k, exps[k] / den, gate_out)
        rank_out = jnp.where(col == k, rk, rank_out)
    idx_ref[...] = idx_out.astype(jnp.int32)
    gate_ref[...] = gate_out
    rank_ref[...] = rank_out.astype(jnp.int32)
    run_ref[...] = run_ref[...] + jnp.sum(chosen, axis=0, keepdims=True)
    cnt_ref[...] = run_ref[...]


def _router(x, g, scale, shift, w, b):
    n, d = x.shape
    e = w.shape[1]
    tm = ROW_TILE
    seg = lambda i: (_segment_of_tile(i, tm), 0, 0)
    small = pl.BlockSpec((tm, TOP_K), lambda i: (i, 0))
    return pl.pallas_call(
        _router_kernel,
        out_shape=(
            jax.ShapeDtypeStruct((n * (d // 2 // LANES), LANES), jnp.uint32),
            jax.ShapeDtypeStruct((n, TOP_K), jnp.int32),
            jax.ShapeDtypeStruct((n, TOP_K), F32),
            jax.ShapeDtypeStruct((n, TOP_K), jnp.int32),
            jax.ShapeDtypeStruct((1, e), F32),
        ),
        grid=(n // tm,),
        in_specs=[
            pl.BlockSpec((tm, d), lambda i: (i, 0)),
            pl.BlockSpec((1, d), lambda i: (0, 0)),
            pl.BlockSpec((1, 1, d), seg),
            pl.BlockSpec((1, 1, d), seg),
            pl.BlockSpec((d, e), lambda i: (0, 0)),
            pl.BlockSpec((1, e), lambda i: (0, 0)),
        ],
        out_specs=(pl.BlockSpec((tm * (d // 2 // LANES), LANES), lambda i: (i, 0)), small, small, small,
                   pl.BlockSpec((1, e), lambda i: (0, 0))),
        scratch_shapes=[pltpu.VMEM((1, e), F32)],
        compiler_params=_params(("arbitrary",)),
        name="router",
    )(x, g.reshape(1, d), scale, shift, w, b.reshape(1, e))


def _moe_kernel(te_ref, tv_ref, src_ref, h_hbm, wg_ref, wu_ref, wd_ref, bg_ref, bu_ref, bd_ref, o_ref,
                gather_buf, gather_sem, xs_scr, act_scr, wg_scr, wu_scr, wd_scr):
    t = pl.program_id(0)
    j = pl.program_id(1)
    n_tiles = pl.num_programs(0) - 1
    n_j = act_scr.shape[1]
    tile_rows = xs_scr.shape[0]
    n_sub_total = tile_rows // MOE_SUB_ROWS
    ta = jnp.minimum(t, n_tiles - 1)
    tb = jnp.maximum(t - 1, 0)

    def sub_blocks(tile):
        return (tv_ref[tile] + MOE_SUB_ROWS - 1) // MOE_SUB_ROWS

    def sub_rows(r):
        return pl.ds(pl.multiple_of(r * MOE_SUB_ROWS, MOE_SUB_ROWS), MOE_SUB_ROWS)

    lane_rows = xs_scr.shape[1] // 2 // LANES
    half = xs_scr.shape[1] // 2

    def token_rows(first_token, count=1):
        return pl.ds(pl.multiple_of(first_token * lane_rows, lane_rows), count * lane_rows)

    def start_gather(tile):
        def issue(i, carry):
            for q in range(GATHER_UNROLL):
                r = i * GATHER_UNROLL + q
                token = src_ref[tile * tile_rows + r]
                pltpu.make_async_copy(h_hbm.at[token_rows(token), :], gather_buf.at[token_rows(r), :], gather_sem).start()
            return carry

        lax.fori_loop(0, sub_blocks(tile) * (MOE_SUB_ROWS // GATHER_UNROLL), issue, 0)

    def wait_gather(tile):
        def wait_sub(r, carry):
            block = token_rows(r * MOE_SUB_ROWS, MOE_SUB_ROWS)
            pltpu.make_async_copy(h_hbm.at[block, :], gather_buf.at[block, :], gather_sem).wait()
            return carry

        lax.fori_loop(0, sub_blocks(tile), wait_sub, 0)

    run_a = (t < n_tiles) & (tv_ref[ta] > 0)
    run_b = (t >= 1) & (tv_ref[tb] > 0)
    n_a = jnp.where(run_a, sub_blocks(ta), 0)
    n_b = jnp.where(run_b, sub_blocks(tb), 0)
    slot_a = ta % 2
    slot_b = tb % 2

    @pl.when((j == 0) & run_a)
    def _():
        @pl.when(t == 0)
        def _():
            start_gather(0)

        wait_gather(t)

        def unpack(r, carry):
            for i in range(lane_rows):
                first = pl.multiple_of(r * MOE_SUB_ROWS * lane_rows, lane_rows) + i
                lo, hi = _unpack_bf16_pairs(gather_buf[pl.ds(first, MOE_SUB_ROWS, stride=lane_rows), :])
                xs_scr[sub_rows(r), i * LANES:(i + 1) * LANES] = lo
                xs_scr[sub_rows(r), half + i * LANES:half + (i + 1) * LANES] = hi
            return carry

        lax.fori_loop(0, n_a, unpack, 0)

        @pl.when(t + 1 < n_tiles)
        def _():
            start_gather(t + 1)

    @pl.when(run_a)
    def _():
        wg_scr[...] = wg_ref[0, 0].astype(BF16)
        wu_scr[...] = wu_ref[0, 0].astype(BF16)

    @pl.when(run_b)
    def _():
        wd_scr[...] = wd_ref[0, 0].astype(BF16)

    def span_rows(r, span):
        return pl.ds(pl.multiple_of(r * MOE_SUB_ROWS, MOE_SUB_ROWS), span * MOE_SUB_ROWS)

    def gate_up(r, span):
        rows = span_rows(r, span)
        x = xs_scr[rows, :]
        g = jnp.dot(x, wg_scr[...], preferred_element_type=F32) + bg_ref[0, j]
        u = jnp.dot(x, wu_scr[...], preferred_element_type=F32) + bu_ref[0, j]
        g = jnp.minimum(g, SWIGLU_LIMIT)
        u = jnp.clip(u, -SWIGLU_LIMIT, SWIGLU_LIMIT)
        act = (u + 1.0) * (g * jax.nn.sigmoid(SWIGLU_ALPHA * g))
        act_scr[slot_a, j, rows, :] = act.astype(BF16)

    def down(r, span):
        rows = span_rows(r, span)
        act = jnp.concatenate([act_scr[slot_b, jb, rows, :] for jb in range(n_j)], axis=1)
        o_ref[rows, :] = jnp.dot(act, wd_scr[...], preferred_element_type=F32) + bd_ref[0, j]

    def for_each_sub_block(n_sub, one, fused_pair):
        def pair(i, carry):
            if fused_pair:
                one(2 * i, 2)
            else:
                one(2 * i, 1)
                one(2 * i + 1, 1)
            return carry

        lax.fori_loop(0, n_sub // 2, pair, 0)

        @pl.when(n_sub % 2 == 1)
        def _():
            one(n_sub - 1, 1)

    for_each_sub_block(n_a, gate_up, fused_pair=False)
    for_each_sub_block(n_b, down, fused_pair=True)

    @pl.when(t >= 1)
    def _():
        def clear(r, carry):
            o_ref[sub_rows(r), :] = jnp.zeros((MOE_SUB_ROWS, o_ref.shape[1]), o_ref.dtype)
            return carry

        lax.fori_loop(n_b, n_sub_total, clear, 0)


def _moe_experts(h_packed, src, tile_expert, tile_valid, layer, w_gate, b_gate, w_up, b_up, w_down, b_down):
    p_rows = src.shape[0]
    _, e, d, hdim = w_gate.shape
    tm, th, tn = MOE_TILE_ROWS, MOE_HIDDEN_BLOCK, MOE_OUT_BLOCK
    n_tiles = p_rows // tm
    n_j = hdim // th
    n_out = d // tn

    assert n_j == n_out, "gate/up and down halves share the block axis"

    def up_tile(t):
        return jnp.minimum(t, n_tiles - 1)

    def down_tile(t):
        return jnp.maximum(t - 1, 0)

    def up_block(t, j, tv):
        return jnp.where((t < n_tiles) & (tv[up_tile(t)] > 0), j, n_j - 1)

    def down_block(t, j, tv):
        return jnp.where((t >= 1) & (tv[down_tile(t)] > 0), j, n_out - 1)

    grid_spec = pltpu.PrefetchScalarGridSpec(
        num_scalar_prefetch=3,
        grid=(n_tiles + 1, n_j),
        in_specs=[
            pl.BlockSpec(memory_space=pl.ANY),
            pl.BlockSpec((1, 1, d, th), lambda t, j, te, tv, src: (layer, te[up_tile(t)], 0, up_block(t, j, tv))),
            pl.BlockSpec((1, 1, d, th), lambda t, j, te, tv, src: (layer, te[up_tile(t)], 0, up_block(t, j, tv))),
            pl.BlockSpec((1, 1, hdim, tn), lambda t, j, te, tv, src: (layer, te[down_tile(t)], 0, down_block(t, j, tv))),
            pl.BlockSpec((1, n_j, 1, th), lambda t, j, te, tv, src: (te[up_tile(t)], 0, 0, 0)),
            pl.BlockSpec((1, n_j, 1, th), lambda t, j, te, tv, src: (te[up_tile(t)], 0, 0, 0)),
            pl.BlockSpec((1, n_out, 1, tn), lambda t, j, te, tv, src: (te[down_tile(t)], 0, 0, 0)),
        ],
        out_specs=pl.BlockSpec((tm, tn), lambda t, j, te, tv, src: (down_tile(t), jnp.where(t >= 1, j, 0))),
        scratch_shapes=[
            pltpu.VMEM((tm * (d // 2 // LANES), LANES), jnp.uint32),
            pltpu.SemaphoreType.DMA(()),
            pltpu.VMEM((tm, d), BF16),
            pltpu.VMEM((2, n_j, tm, th), BF16),
            pltpu.VMEM((d, th), BF16),
            pltpu.VMEM((d, th), BF16),
            pltpu.VMEM((hdim, tn), BF16),
        ],
    )
    return pl.pallas_call(
        _moe_kernel,
        out_shape=jax.ShapeDtypeStruct((p_rows, d), F32),
        grid_spec=grid_spec,
        compiler_params=_params(("arbitrary", "arbitrary")),
        name="moe_experts",
    )(tile_expert, tile_valid, src, h_packed, w_gate, w_up, w_down,
      b_gate[layer].reshape(e, n_j, 1, th), b_up[layer].reshape(e, n_j, 1, th), b_down[layer].reshape(e, n_out, 1, tn))


def _combine_kernel(pos_ref, ys_hbm, gates_ref, x_ref, gate2_ref, norm_ref, o_ref, rows_buf, rows_sem, *, final_norm):
    i = pl.program_id(0)
    tm = x_ref.shape[0]
    slot = i % 2

    def start_gather(tile, slot):
        def issue(i, carry):
            for q in range(GATHER_UNROLL // TOP_K):
                r = i * (GATHER_UNROLL // TOP_K) + q
                for k in range(TOP_K):
                    row = pos_ref[(tile * tm + r) * TOP_K + k]
                    pltpu.make_async_copy(ys_hbm.at[pl.ds(row, 1), :], rows_buf.at[slot, pl.ds(k * tm + r, 1), :],
                                          rows_sem.at[slot]).start()
            return carry

        lax.fori_loop(0, tm // (GATHER_UNROLL // TOP_K), issue, 0)

    @pl.when(i == 0)
    def _():
        start_gather(0, 0)

    @pl.when(i + 1 < pl.num_programs(0))
    def _():
        start_gather(i + 1, 1 - slot)

    pltpu.make_async_copy(ys_hbm.at[pl.ds(0, TOP_K * tm), :], rows_buf.at[slot], rows_sem.at[slot]).wait()
    gates = gates_ref[...]
    moe = gates[:, 0:1] * rows_buf[slot, pl.ds(0, tm), :]
    for k in range(1, TOP_K):
        moe = moe + gates[:, k:k + 1] * rows_buf[slot, pl.ds(k * tm, tm), :]
    x = x_ref[...] + gate2_ref[0] * moe
    if final_norm:
        ms = jnp.mean(x * x, axis=-1, keepdims=True)
        x = x * lax.rsqrt(ms + EPS) * norm_ref[...]
    o_ref[...] = x


def _combine(ys, pos, gates, x, gate2, norm_g, final_norm):
    n, d = x.shape
    tm = COMBINE_ROWS
    grid_spec = pltpu.PrefetchScalarGridSpec(
        num_scalar_prefetch=1,
        grid=(n // tm,),
        in_specs=[
            pl.BlockSpec(memory_space=pl.ANY),
            pl.BlockSpec((tm, TOP_K), lambda i, pos: (i, 0)),
            pl.BlockSpec((tm, d), lambda i, pos: (i, 0)),
            pl.BlockSpec((1, 1, d), lambda i, pos: (_segment_of_tile(i, tm), 0, 0)),
            pl.BlockSpec((1, d), lambda i, pos: (0, 0)),
        ],
        out_specs=pl.BlockSpec((tm, d), lambda i, pos: (i, 0)),
        scratch_shapes=[pltpu.VMEM((2, TOP_K * tm, d), F32), pltpu.SemaphoreType.DMA((2,))],
    )
    return pl.pallas_call(
        functools.partial(_combine_kernel, final_norm=final_norm),
        out_shape=jax.ShapeDtypeStruct((n, d), F32),
        grid_spec=grid_spec,
        compiler_params=_params(("arbitrary",)),
        name="moe_combine",
    )(pos.reshape(-1), ys, gates, x, gate2, norm_g.reshape(1, d))


def _moe_layer(x, g, scale, shift, gate2, layer, w_router, b_router, w_gate, b_gate, w_up, b_up, w_down, b_down,
               norm_g, final_norm):
    n, d = x.shape
    e = w_router.shape[-1]
    tm = MOE_TILE_ROWS
    h, top_i, gates, rank, counts = _router(x, g, scale, shift, w_router[layer], b_router[layer])
    counts = counts[0].astype(jnp.int32)
    padded = ((counts + tm - 1) // tm) * tm
    ends = jnp.cumsum(padded)
    starts = ends - padded
    n_tiles = -(-(n * TOP_K) // tm) + e
    p_rows = n_tiles * tm
    tile_start = jnp.arange(n_tiles, dtype=jnp.int32) * tm
    n_used = ends[-1] // tm
    tile_expert = jnp.minimum(jnp.searchsorted(ends, tile_start, side="right"), e - 1).astype(jnp.int32)
    tile_valid = jnp.clip(counts[tile_expert] - (tile_start - starts[tile_expert]), 0, tm)
    tile_valid = jnp.where(tile_start < ends[-1], tile_valid, 0).astype(jnp.int32)
    last = jnp.maximum(n_used - 1, 0)
    tile_expert = jnp.where(tile_start < ends[-1], tile_expert, tile_expert[last]).astype(jnp.int32)
    pos = starts[top_i] + rank
    token = jnp.broadcast_to(jnp.arange(n, dtype=jnp.int32)[:, None], (n, TOP_K))
    src = jnp.zeros((p_rows,), jnp.int32).at[pos.reshape(-1)].set(token.reshape(-1))
    ys = _moe_experts(h, src, tile_expert, tile_valid, layer, w_gate, b_gate, w_up, b_up, w_down, b_down)
    return _combine(ys, pos, gates, x, gate2, norm_g, final_norm)


def kernel(x_prompt, x_sample, cache_k, cache_v, state_rglru, c, c_ctx, norm_mix, norm_ffn, w_mod, b_mod,
           attn_w_in, attn_b_in, attn_w_out, attn_b_out, attn_sink,
           rnn_w_in, rnn_b_in, rnn_conv_w, rnn_conv_b, rnn_w_a, rnn_b_a, rnn_w_x, rnn_b_x,
           rnn_lambda, rnn_w_out, rnn_b_out,
           moe_w_router, moe_b_router, moe_w_gate, moe_b_gate, moe_w_up, moe_b_up,
           moe_w_down, moe_b_down, final_norm):
    d = D_MODEL
    n_ctx = BATCH * SEQ
    attn_w = N_HEADS * HEAD_DIM
    kv_w = N_KV_HEADS * HEAD_DIM
    x = jnp.concatenate([x_prompt.reshape(n_ctx, d), x_sample.reshape(DEC_BATCH * DEC_SEQ, d)], axis=0)
    cond = jnp.concatenate([c_ctx[None, :], c, jnp.zeros((SUBLANES - 1 - DEC_BATCH, d), F32)], axis=0)
    mods = _modulation(cond, w_mod, b_mod)

    def mod(l, k):
        return mods[l, :, k * d:(k + 1) * d].reshape(SUBLANES, 1, d)

    new_k, new_v, new_s = [], [], []
    for l in range(DEPTH):
        j = l // 2
        sh1, sc1, g1, sh2, sc2, g2 = [mod(l, k) for k in range(N_MOD)]
        if l % 2 == 0:
            p = _proj_in(x, norm_mix[l], sc1, sh1, attn_w_in[j], attn_b_in[j], tn=PROJ_IN_ATTN_COLS)
            new_k.append(p[:n_ctx, attn_w:attn_w + kv_w].reshape(BATCH, SEQ, N_KV_HEADS, HEAD_DIM))
            new_v.append(p[:n_ctx, attn_w + kv_w:attn_w + 2 * kv_w].reshape(BATCH, SEQ, N_KV_HEADS, HEAD_DIM))
            mix_ctx = _ctx_mixer(p, attn_sink[j])
            mix_lat = _lat_mixer(p, cache_k[:, j].reshape(DEC_BATCH, PAST_LEN, kv_w),
                                 cache_v[:, j].reshape(DEC_BATCH, PAST_LEN, kv_w), attn_sink[j])
            x = _proj_out(mix_ctx, mix_lat, attn_w_out[j], attn_b_out[j], g1, x)
        else:
            p = _proj_in(x, norm_mix[l], sc1, sh1, rnn_w_in[j], rnn_b_in[j], tn=PROJ_IN_RNN_COLS)
            args = (rnn_conv_w[j], rnn_conv_b[j], rnn_w_a[j], rnn_b_a[j], rnn_w_x[j], rnn_b_x[j], rnn_lambda[j])
            h0_ctx = jnp.zeros((BATCH, 2, d), F32)
            mix_ctx, st = _rnn_core(p, 0, n_ctx, h0_ctx, *args, n_seq=SUBLANES, seq_len=SEQ)
            mix_lat, _ = _rnn_core(p, n_ctx, DEC_BATCH * DEC_SEQ, state_rglru[:, j], *args,
                                   n_seq=DEC_BATCH, seq_len=DEC_SEQ)
            new_s.append(st)
            x = _proj_out(mix_ctx, mix_lat, rnn_w_out[j], rnn_b_out[j], g1, x)
        x = _moe_layer(x, norm_ffn[l], sc2, sh2, g2, l, moe_w_router, moe_b_router,
                       moe_w_gate, moe_b_gate, moe_w_up, moe_b_up, moe_w_down, moe_b_down,
                       final_norm, final_norm=(l == DEPTH - 1))
    y = x
    y_prompt = y[:n_ctx].reshape(BATCH, SEQ, d)
    y_sample = y[n_ctx:].reshape(DEC_BATCH, DEC_SEQ, d)
    return (y_prompt, y_sample, jnp.stack(new_k, axis=1), jnp.stack(new_v, axis=1), jnp.stack(new_s, axis=1))
```

```python
import functools
import math

import numpy as np
import jax
import jax.numpy as jnp
from jax import lax
from jax.experimental import pallas as pl
from jax.experimental.pallas import tpu as pltpu

D_MODEL = 2048
BATCH = 32
SEQ = 256
DEPTH = 2
DEC_BATCH = 2
DEC_SEQ = 1024
PAST_LEN = 256
GRID_W = 64
N_HEADS = 16
N_KV_HEADS = 2
HEAD_DIM = 64
WINDOW = 128
ROPE_THETA = 10000.0
N_FOURIER_GROUPS = 4
N_RNN_BLOCKS = 8
CONV_WIDTH = 4
CONV_LEFT = 2
RG_C = 8.0
N_EXPERTS = 32
TOP_K = 4
SWIGLU_LIMIT = 7.0
SWIGLU_ALPHA = 1.702
N_MOD = 6
EPS = 1e-6

LANES = 128
SUBLANES = 8
VMEM_LIMIT_BYTES = 56 * 1024 * 1024

MOE_TILE_ROWS = 1536
MOE_SUB_ROWS = 256
MOE_HIDDEN_BLOCK = 256
MOE_OUT_BLOCK = 256
GATHER_UNROLL = 8
SCAN_ROW_PAD = 8
SCAN_UNROLL = 4
ROW_TILE = 512
PROJ_ROWS = 1024
COMBINE_ROWS = 128
PROJ_OUT_COLS = 512
PROJ_IN_ATTN_COLS = 768
PROJ_IN_RNN_COLS = 1024

F32 = jnp.float32
BF16 = jnp.bfloat16


def _params(semantics):
    return pltpu.CompilerParams(dimension_semantics=semantics, vmem_limit_bytes=VMEM_LIMIT_BYTES)


def _segment_of_tile(i, tile_rows):
    n_ctx_tiles = (BATCH * SEQ) // tile_rows
    tiles_per_latent = DEC_SEQ // tile_rows
    return jnp.where(i < n_ctx_tiles, 0, 1 + (i - n_ctx_tiles) // tiles_per_latent)


def _rms_modulate(x, g, scale, shift):
    ms = jnp.mean(x * x, axis=-1, keepdims=True)
    return (x * lax.rsqrt(ms + EPS) * g) * (1.0 + scale) + shift


def _modulation_kernel(c_ref, w_ref, b_ref, o_ref):
    c = c_ref[...]
    s = (c * jax.nn.sigmoid(c)).astype(BF16)
    o_ref[0] = jnp.dot(s, w_ref[0].astype(BF16), preferred_element_type=F32) + b_ref[0]


def _modulation(cond, w_mod, b_mod):
    d = D_MODEL
    tn = 512
    n_out = N_MOD * d
    return pl.pallas_call(
        _modulation_kernel,
        out_shape=jax.ShapeDtypeStruct((DEPTH, SUBLANES, n_out), F32),
        grid=(DEPTH, n_out // tn),
        in_specs=[
            pl.BlockSpec((SUBLANES, d), lambda l, j: (0, 0)),
            pl.BlockSpec((1, d, tn), lambda l, j: (l, 0, j)),
            pl.BlockSpec((1, 1, tn), lambda l, j: (l, 0, j)),
        ],
        out_specs=pl.BlockSpec((1, SUBLANES, tn), lambda l, j: (l, 0, j)),
        compiler_params=_params(("arbitrary", "arbitrary")),
        name="modulation",
    )(cond, w_mod, b_mod.reshape(DEPTH, 1, n_out))


def _proj_in_kernel(x_ref, g_ref, sc_ref, sh_ref, w_ref, b_ref, o_ref, h_ref):
    @pl.when(pl.program_id(1) == 0)
    def _():
        h_ref[...] = _rms_modulate(x_ref[...], g_ref[...], sc_ref[0], sh_ref[0]).astype(BF16)

    o_ref[...] = jnp.dot(h_ref[...], w_ref[...].astype(BF16), preferred_element_type=F32) + b_ref[...]


def _proj_in(x, g, scale, shift, w, b, tn):
    n, d = x.shape
    n_out = w.shape[1]
    tm = PROJ_ROWS
    seg = lambda i, j: (_segment_of_tile(i, tm), 0, 0)
    return pl.pallas_call(
        _proj_in_kernel,
        out_shape=jax.ShapeDtypeStruct((n, n_out), F32),
        grid=(n // tm, n_out // tn),
        in_specs=[
            pl.BlockSpec((tm, d), lambda i, j: (i, 0)),
            pl.BlockSpec((1, d), lambda i, j: (0, 0)),
            pl.BlockSpec((1, 1, d), seg),
            pl.BlockSpec((1, 1, d), seg),
            pl.BlockSpec((d, tn), lambda i, j: (0, j)),
            pl.BlockSpec((1, tn), lambda i, j: (0, j)),
        ],
        out_specs=pl.BlockSpec((tm, tn), lambda i, j: (i, j)),
        scratch_shapes=[pltpu.VMEM((tm, d), BF16)],
        compiler_params=_params(("arbitrary", "arbitrary")),
        name="proj_in",
    )(x, g.reshape(1, d), scale, shift, w, b.reshape(1, n_out))


def _proj_out_kernel(a_ctx_ref, a_lat_ref, w_ref, b_ref, gate_ref, res_ref, o_ref, *, n_ctx_tiles):
    def project(a_ref):
        y = jnp.dot(a_ref[...], w_ref[...].astype(BF16), preferred_element_type=F32) + b_ref[...]
        o_ref[...] = res_ref[...] + gate_ref[0] * y

    @pl.when(pl.program_id(0) < n_ctx_tiles)
    def _():
        project(a_ctx_ref)

    @pl.when(pl.program_id(0) >= n_ctx_tiles)
    def _():
        project(a_lat_ref)


def _proj_out(a_ctx, a_lat, w, b, gate, res):
    n, d = res.shape
    k = a_ctx.shape[1]
    tm, tn = PROJ_ROWS, PROJ_OUT_COLS
    n_ctx_tiles = a_ctx.shape[0] // tm
    return pl.pallas_call(
        functools.partial(_proj_out_kernel, n_ctx_tiles=n_ctx_tiles),
        out_shape=jax.ShapeDtypeStruct((n, d), F32),
        grid=(n // tm, d // tn),
        in_specs=[
            pl.BlockSpec((tm, k), lambda i, j: (jnp.minimum(i, n_ctx_tiles - 1), 0)),
            pl.BlockSpec((tm, k), lambda i, j: (jnp.maximum(i - n_ctx_tiles, 0), 0)),
            pl.BlockSpec((k, tn), lambda i, j: (0, j)),
            pl.BlockSpec((1, tn), lambda i, j: (0, j)),
            pl.BlockSpec((1, 1, tn), lambda i, j: (_segment_of_tile(i, tm), 0, j)),
            pl.BlockSpec((tm, tn), lambda i, j: (i, j)),
        ],
        out_specs=pl.BlockSpec((tm, tn), lambda i, j: (i, j)),
        compiler_params=_params(("arbitrary", "arbitrary")),
        name="proj_out",
    )(a_ctx, a_lat, w, b.reshape(1, d), gate, res)


def _dot_nt(a, b):
    return lax.dot_general(a, b, (((1,), (1,)), ((), ())), preferred_element_type=F32)


def _head_pair_operands(k, v, group):
    lane = lax.broadcasted_iota(jnp.int32, k.shape, 1)
    low = lane < HEAD_DIM
    k_sw = pltpu.roll(k, HEAD_DIM, 1)
    v_sw = pltpu.roll(v, HEAD_DIM, 1)
    if group == 0:
        kd = jnp.where(low, k, k_sw)
        vd = jnp.where(low, v, v_sw)
    else:
        kd = jnp.where(low, k_sw, k)
        vd = jnp.where(low, v_sw, v)
    v_lo = jnp.where(low, vd, 0.0).astype(BF16)
    v_hi = jnp.where(low, 0.0, vd).astype(BF16)
    return kd.astype(BF16), v_lo, v_hi


def _split_pair(q2):
    lane = lax.broadcasted_iota(jnp.int32, q2.shape, 1)
    low = lane < HEAD_DIM
    qs = q2 * (HEAD_DIM ** -0.5)
    return jnp.where(low, qs, 0.0).astype(BF16), jnp.where(low, 0.0, qs).astype(BF16)


def _softmax_pv(scores, values, sink):
    m = jnp.full((scores[0].shape[0], 1), sink, F32)
    for s in scores:
        m = jnp.maximum(m, jnp.max(s, axis=-1, keepdims=True))
    den = jnp.exp(sink - m)
    ps = []
    for s in scores:
        p = jnp.exp(s - m)
        den = den + jnp.sum(p, axis=-1, keepdims=True)
        ps.append(p)
    out = None
    for p, v in zip(ps, values):
        o = jnp.dot((p / den).astype(BF16), v, preferred_element_type=F32)
        out = o if out is None else out + o
    return out


def _dft_matrices(t):
    idx = np.arange(t)
    ang = 2.0 * np.pi * ((idx[:, None] * idx[None, :]) % t) / t
    m = np.concatenate([np.cos(ang), np.sin(ang)], axis=0) / math.sqrt(t)
    return jnp.asarray(m, dtype=BF16)


def _dft_channel_matrix(c):
    idx = np.arange(c)
    ang = 2.0 * np.pi * ((idx[:, None] * idx[None, :]) % c) / c
    m = np.concatenate([np.cos(ang), -np.sin(ang)], axis=0) / math.sqrt(c)
    return jnp.asarray(m, dtype=BF16)


def _fourier_group(f_g, ts_ref, cs_ref):
    t = f_g.shape[0]
    ab = jnp.dot(ts_ref[...], f_g.astype(BF16), preferred_element_type=F32)
    lhs = jnp.concatenate([ab[:t], ab[t:]], axis=1).astype(BF16)
    return jnp.dot(lhs, cs_ref[...], preferred_element_type=F32)


def _ctx_mixer_kernel(sink_ref, p_ref, ts_ref, cs_ref, o_ref):
    attn_w = N_HEADS * HEAD_DIM
    kv_w = N_KV_HEADS * HEAD_DIM
    pair_w = 2 * HEAD_DIM
    group_heads = N_HEADS // N_KV_HEADS
    k = p_ref[:, attn_w:attn_w + kv_w]
    v = p_ref[:, attn_w + kv_w:attn_w + 2 * kv_w]
    for g in range(N_KV_HEADS):
        kd, v_lo, v_hi = _head_pair_operands(k, v, g)
        for i in range(group_heads // 2):
            pair = g * (group_heads // 2) + i
            q_lo, q_hi = _split_pair(p_ref[:, pair * pair_w:(pair + 1) * pair_w])
            o = _softmax_pv([_dot_nt(q_lo, kd)], [v_lo], sink_ref[2 * pair])
            o = o + _softmax_pv([_dot_nt(q_hi, kd)], [v_hi], sink_ref[2 * pair + 1])
            o_ref[:, pair * pair_w:(pair + 1) * pair_w] = o.astype(o_ref.dtype)
    f0 = attn_w + 2 * kv_w
    fg = (D_MODEL - attn_w) // N_FOURIER_GROUPS
    for g in range(N_FOURIER_GROUPS):
        z = _fourier_group(p_ref[:, f0 + g * fg:f0 + (g + 1) * fg], ts_ref, cs_ref)
        o_ref[:, attn_w + g * fg:attn_w + (g + 1) * fg] = z.astype(o_ref.dtype)


def _ctx_mixer(p, sink):
    n, width = BATCH * SEQ, p.shape[1]
    fg = (D_MODEL - N_HEADS * HEAD_DIM) // N_FOURIER_GROUPS
    return pl.pallas_call(
        _ctx_mixer_kernel,
        out_shape=jax.ShapeDtypeStruct((n, D_MODEL), BF16),
        grid=(n // SEQ,),
        in_specs=[
            pl.BlockSpec(memory_space=pltpu.SMEM),
            pl.BlockSpec((SEQ, width), lambda b: (b, 0)),
            pl.BlockSpec((2 * SEQ, SEQ), lambda b: (0, 0)),
            pl.BlockSpec((2 * fg, fg), lambda b: (0, 0)),
        ],
        out_specs=pl.BlockSpec((SEQ, D_MODEL), lambda b: (b, 0)),
        compiler_params=_params(("arbitrary",)),
        name="ctx_mixer",
    )(sink, p, _dft_matrices(SEQ), _dft_channel_matrix(fg))


def _rope_tables():
    rows = DEC_SEQ // GRID_W
    row = np.repeat(np.arange(rows, dtype=np.float32), GRID_W)
    col = np.tile(np.arange(GRID_W, dtype=np.float32), rows)
    n_freq = HEAD_DIM // 4
    inv = jnp.asarray(ROPE_THETA, F32) ** (-jnp.arange(n_freq, dtype=F32) / n_freq)
    ang = jnp.concatenate([row[:, None] * inv, col[:, None] * inv], axis=-1)
    cos = jnp.repeat(jnp.cos(ang), 2, axis=-1)
    sin = jnp.repeat(jnp.sin(ang), 2, axis=-1)
    sign = jnp.tile(jnp.asarray([-1.0, 1.0], F32), HEAD_DIM // 2)
    return jnp.tile(cos, (1, 2)), jnp.tile(sin * sign, (1, 2))


def _rope(x, cos, sin_signed):
    lane = lax.broadcasted_iota(jnp.int32, x.shape, 1)
    width = x.shape[1]
    partner = jnp.where(lane % 2 == 0, pltpu.roll(x, width - 1, 1), pltpu.roll(x, 1, 1))
    return x * cos + partner * sin_signed


def _lat_mixer_kernel(sink_ref, p_ref, ck_ref, cv_ref, cos_ref, sin_ref, ts_ref, cs_ref, o_ref, q_scr, k_scr):
    attn_w = N_HEADS * HEAD_DIM
    kv_w = N_KV_HEADS * HEAD_DIM
    pair_w = 2 * HEAD_DIM
    group_heads = N_HEADS // N_KV_HEADS
    q_rows = 256
    cos = cos_ref[...]
    sin = sin_ref[...]
    for pair in range(N_HEADS // 2):
        q_scr[:, pair * pair_w:(pair + 1) * pair_w] = _rope(p_ref[:, pair * pair_w:(pair + 1) * pair_w], cos, sin)
    k_scr[...] = _rope(p_ref[:, attn_w:attn_w + kv_w], cos, sin)
    v = p_ref[:, attn_w + kv_w:attn_w + 2 * kv_w]
    ck = ck_ref[0]
    cv = cv_ref[0]
    k = k_scr[...]
    ops = []
    for g in range(N_KV_HEADS):
        ops.append(_head_pair_operands(k, v, g) + _head_pair_operands(ck, cv, g))

    def chunk(c, carry):
        r0 = pl.multiple_of(c * q_rows, q_rows)
        qi = r0 + lax.broadcasted_iota(jnp.int32, (q_rows, DEC_SEQ), 0)
        kj = lax.broadcasted_iota(jnp.int32, (q_rows, DEC_SEQ), 1)
        valid = jnp.abs(qi - kj) <= WINDOW
        for g in range(N_KV_HEADS):
            kd, v_lo, v_hi, ckd, cv_lo, cv_hi = ops[g]
            for i in range(group_heads // 2):
                pair = g * (group_heads // 2) + i
                q_lo, q_hi = _split_pair(q_scr[pl.ds(r0, q_rows), pair * pair_w:(pair + 1) * pair_w])
                s_lo = jnp.where(valid, _dot_nt(q_lo, kd), -jnp.inf)
                o = _softmax_pv([_dot_nt(q_lo, ckd), s_lo], [cv_lo, v_lo], sink_ref[2 * pair])
                s_hi = jnp.where(valid, _dot_nt(q_hi, kd), -jnp.inf)
                o = o + _softmax_pv([_dot_nt(q_hi, ckd), s_hi], [cv_hi, v_hi], sink_ref[2 * pair + 1])
                o_ref[pl.ds(r0, q_rows), pair * pair_w:(pair + 1) * pair_w] = o.astype(o_ref.dtype)
        return carry

    lax.fori_loop(0, DEC_SEQ // q_rows, chunk, 0)
    f0 = attn_w + 2 * kv_w
    fg = (D_MODEL - attn_w) // N_FOURIER_GROUPS
    for g in range(N_FOURIER_GROUPS):
        z = _fourier_group(p_ref[:, f0 + g * fg:f0 + (g + 1) * fg], ts_ref, cs_ref)
        o_ref[:, attn_w + g * fg:attn_w + (g + 1) * fg] = z.astype(o_ref.dtype)


def _lat_mixer(p, cache_k, cache_v, sink):
    n, width = DEC_BATCH * DEC_SEQ, p.shape[1]
    first = (BATCH * SEQ) // DEC_SEQ
    kv_w = N_KV_HEADS * HEAD_DIM
    attn_w = N_HEADS * HEAD_DIM
    fg = (D_MODEL - attn_w) // N_FOURIER_GROUPS
    cos, sin = _rope_tables()
    return pl.pallas_call(
        _lat_mixer_kernel,
        out_shape=jax.ShapeDtypeStruct((n, D_MODEL), BF16),
        grid=(n // DEC_SEQ,),
        in_specs=[
            pl.BlockSpec(memory_space=pltpu.SMEM),
            pl.BlockSpec((DEC_SEQ, width), lambda b: (first + b, 0)),
            pl.BlockSpec((1, PAST_LEN, kv_w), lambda b: (b, 0, 0)),
            pl.BlockSpec((1, PAST_LEN, kv_w), lambda b: (b, 0, 0)),
            pl.BlockSpec((DEC_SEQ, 2 * HEAD_DIM), lambda b: (0, 0)),
            pl.BlockSpec((DEC_SEQ, 2 * HEAD_DIM), lambda b: (0, 0)),
            pl.BlockSpec((2 * DEC_SEQ, DEC_SEQ), lambda b: (0, 0)),
            pl.BlockSpec((2 * fg, fg), lambda b: (0, 0)),
        ],
        out_specs=pl.BlockSpec((DEC_SEQ, D_MODEL), lambda b: (b, 0)),
        scratch_shapes=[pltpu.VMEM((DEC_SEQ, attn_w), F32), pltpu.VMEM((DEC_SEQ, kv_w), F32)],
        compiler_params=_params(("arbitrary",)),
        name="lat_mixer",
    )(sink, p, cache_k, cache_v, cos, sin, _dft_matrices(DEC_SEQ), _dft_channel_matrix(fg))


def _rnn_core_kernel(xr_ref, gr_ref, cw_ref, cb_ref, wa_ref, ba_ref, wx_ref, bx_ref, lam_ref, h0_ref,
                     y_ref, st_ref, a_scr, b_scr, h_scr, *, n_seq, seq_len):
    rows = n_seq * seq_len
    n_lb = xr_ref.shape[1] // LANES
    xr = xr_ref[...]
    t_idx = lax.broadcasted_iota(jnp.int32, (rows, 1), 0) % seq_len
    xc = jnp.broadcast_to(cb_ref[...], xr.shape)
    for tap in range(CONV_WIDTH):
        off = tap - CONV_LEFT
        shifted = xr if off == 0 else pltpu.roll(xr, (-off) % rows, 0)
        valid = (t_idx + off >= 0) & (t_idx + off < seq_len)
        xc = xc + jnp.where(valid, shifted, 0.0) * cw_ref[tap:tap + 1, :]
    xcb = xc.astype(BF16)
    pitch = seq_len + SCAN_ROW_PAD

    def sigmoid(z):
        return 0.5 * jnp.tanh(0.5 * z) + 0.5

    def seq_rows(s):
        return pl.ds(s * pitch, seq_len)

    for d in range(2):
        r = sigmoid(jnp.dot(xcb, wa_ref[d, 0].astype(BF16), preferred_element_type=F32) + ba_ref[d:d + 1, :])
        gi = sigmoid(jnp.dot(xcb, wx_ref[d, 0].astype(BF16), preferred_element_type=F32) + bx_ref[d:d + 1, :])
        neg_lam = -lam_ref[d:d + 1, :]
        softplus = jnp.maximum(neg_lam, 0.0) + jnp.log1p(jnp.exp(-jnp.abs(neg_lam)))
        a = jnp.exp(-RG_C * r * softplus)
        b = jnp.sqrt(1.0 - a * a) * (gi * xc)
        for lb in range(n_lb):
            for s in range(n_seq):
                a_scr[d * n_lb + lb, seq_rows(s), :] = a[s * seq_len:(s + 1) * seq_len, lb * LANES:(lb + 1) * LANES]
                b_scr[d * n_lb + lb, seq_rows(s), :] = b[s * seq_len:(s + 1) * seq_len, lb * LANES:(lb + 1) * LANES]

    def time_rows(t):
        return pl.ds(t, n_seq, stride=pitch)

    def lane_block(ref, k, lb):
        return ref[:, k, lb * LANES:(lb + 1) * LANES]

    def fwd(t, hs):
        out = []
        for lb in range(n_lb):
            h = a_scr[lb, time_rows(t), :] * hs[lb] + b_scr[lb, time_rows(t), :]
            h_scr[lb, time_rows(t), :] = h
            out.append(h)
        return tuple(out)

    hs = lax.fori_loop(0, seq_len, fwd, tuple(lane_block(h0_ref, 0, lb) for lb in range(n_lb)), unroll=SCAN_UNROLL)
    for lb in range(n_lb):
        st_ref[:, 0, lb * LANES:(lb + 1) * LANES] = hs[lb]

    def bwd(i, hs):
        t = seq_len - 1 - i
        out = []
        for lb in range(n_lb):
            h = a_scr[n_lb + lb, time_rows(t), :] * hs[lb] + b_scr[n_lb + lb, time_rows(t), :]
            h_scr[lb, time_rows(t), :] = h_scr[lb, time_rows(t), :] + h
            out.append(h)
        return tuple(out)

    hs = lax.fori_loop(0, seq_len, bwd, tuple(lane_block(h0_ref, 1, lb) for lb in range(n_lb)), unroll=SCAN_UNROLL)
    for lb in range(n_lb):
        st_ref[:, 1, lb * LANES:(lb + 1) * LANES] = hs[lb]
    for lb in range(n_lb):
        for s in range(n_seq):
            gr = gr_ref[s * seq_len:(s + 1) * seq_len, lb * LANES:(lb + 1) * LANES]
            gelu = 0.5 * gr * (1.0 + jnp.tanh(math.sqrt(2.0 / math.pi) * (gr + 0.044715 * (gr * gr * gr))))
            y_ref[s * seq_len:(s + 1) * seq_len, lb * LANES:(lb + 1) * LANES] = (
                gelu * h_scr[lb, seq_rows(s), :]).astype(y_ref.dtype)


def _rnn_core(p, first_row, n, h0, conv_w, conv_b, w_a, b_a, w_x, b_x, lam, n_seq, seq_len):
    d_rnn = p.shape[1] // 2
    cb = d_rnn // N_RNN_BLOCKS
    rows = n_seq * seq_len
    n_batch = n // seq_len
    first = first_row // rows
    scan_rows = n_seq * (seq_len + SCAN_ROW_PAD)
    kern = functools.partial(_rnn_core_kernel, n_seq=n_seq, seq_len=seq_len)
    return pl.pallas_call(
        kern,
        out_shape=(jax.ShapeDtypeStruct((n, d_rnn), BF16), jax.ShapeDtypeStruct((n_batch, 2, d_rnn), F32)),
        grid=(n // rows, N_RNN_BLOCKS),
        in_specs=[
            pl.BlockSpec((rows, cb), lambda i, c: (first + i, c)),
            pl.BlockSpec((rows, cb), lambda i, c: (first + i, N_RNN_BLOCKS + c)),
            pl.BlockSpec((CONV_WIDTH, cb), lambda i, c: (0, c)),
            pl.BlockSpec((1, cb), lambda i, c: (0, c)),
            pl.BlockSpec((2, 1, cb, cb), lambda i, c: (0, c, 0, 0)),
            pl.BlockSpec((2, cb), lambda i, c: (0, c)),
            pl.BlockSpec((2, 1, cb, cb), lambda i, c: (0, c, 0, 0)),
            pl.BlockSpec((2, cb), lambda i, c: (0, c)),
            pl.BlockSpec((2, cb), lambda i, c: (0, c)),
            pl.BlockSpec((n_seq, 2, cb), lambda i, c: (i, 0, c)),
        ],
        out_specs=(
            pl.BlockSpec((rows, cb), lambda i, c: (i, c)),
            pl.BlockSpec((n_seq, 2, cb), lambda i, c: (i, 0, c)),
        ),
        scratch_shapes=[pltpu.VMEM((2 * cb // LANES, scan_rows, LANES), F32),
                        pltpu.VMEM((2 * cb // LANES, scan_rows, LANES), F32),
                        pltpu.VMEM((cb // LANES, scan_rows, LANES), F32)],
        compiler_params=_params(("arbitrary", "arbitrary")),
        name="rnn_core",
    )(p, p, conv_w, conv_b.reshape(1, d_rnn), w_a, b_a, w_x, b_x, lam, h0)


def _pack_bf16_pairs(h):
    half = h.shape[1] // 2
    bits = lax.bitcast_convert_type(h.astype(BF16).astype(F32), jnp.uint32)
    return (bits[:, :half] >> 16) | (bits[:, half:] & jnp.uint32(0xFFFF0000))


def _unpack_bf16_pairs(w):
    lo = lax.bitcast_convert_type(w << 16, F32)
    hi = lax.bitcast_convert_type(w & jnp.uint32(0xFFFF0000), F32)
    return lo.astype(BF16), hi.astype(BF16)


def _router_kernel(x_ref, g_ref, sc_ref, sh_ref, w_ref, b_ref, h_ref, idx_ref, gate_ref, rank_ref, cnt_ref, run_ref):
    tm = x_ref.shape[0]

    @pl.when(pl.program_id(0) == 0)
    def _():
        run_ref[...] = jnp.zeros_like(run_ref)

    h = _rms_modulate(x_ref[...], g_ref[...], sc_ref[0], sh_ref[0])
    packed = _pack_bf16_pairs(h)
    lane_rows = packed.shape[1] // LANES
    for i in range(lane_rows):
        h_ref[pl.ds(i, tm, stride=lane_rows), :] = packed[:, i * LANES:(i + 1) * LANES]
    logits =jnp.dot(h, w_ref[...], preferred_element_type=F32, precision=lax.Precision.HIGHEST) + b_ref[...]
    lane = lax.broadcasted_iota(jnp.int32, logits.shape, 1).astype(F32)
    col = lax.broadcasted_iota(jnp.int32, (tm, TOP_K), 1)
    chosen = jnp.zeros(logits.shape, F32)
    top_v, top_i, hits = [], [], []
    work = logits
    for _ in range(TOP_K):
        m = jnp.max(work, axis=-1, keepdims=True)
        first = jnp.min(jnp.where(work == m, lane, float(N_EXPERTS)), axis=-1, keepdims=True)
        hit = lane == first
        work = jnp.where(hit, -jnp.inf, work)
        chosen = jnp.where(hit, 1.0, chosen)
        top_v.append(m)
        top_i.append(first)
        hits.append(hit)
    exps = [jnp.exp(v - top_v[0]) for v in top_v]
    den = exps[0]
    for e in exps[1:]:
        den = den + e
    ri = lax.broadcasted_iota(jnp.int32, (tm, tm), 0)
    ci = lax.broadcasted_iota(jnp.int32, (tm, tm), 1)
    before = (ci < ri).astype(BF16)
    rank_all = jnp.dot(before, chosen.astype(BF16), preferred_element_type=F32) + run_ref[...]
    idx_out = jnp.zeros((tm, TOP_K), F32)
    gate_out = jnp.zeros((tm, TOP_K), F32)
    rank_out = jnp.zeros((tm, TOP_K), F32)
    for k in range(TOP_K):
        rk = jnp.sum(jnp.where(hits[k], rank_all, 0.0), axis=-1, keepdims=True)
        idx_out = jnp.where(col == k, top_i[k], idx_out)
        gate_out = jnp.where(col == k, exps[k] / den, gate_out)
        rank_out = jnp.where(col == k, rk, rank_out)
    idx_ref[...] = idx_out.astype(jnp.int32)
    gate_ref[...] = gate_out
    rank_ref[...] = rank_out.astype(jnp.int32)
    run_ref[...] = run_ref[...] + jnp.sum(chosen, axis=0, keepdims=True)
    cnt_ref[...] = run_ref[...]


def _router(x, g, scale, shift, w, b):
    n, d = x.shape
    e = w.shape[1]
    tm = ROW_TILE
    seg = lambda i: (_segment_of_tile(i, tm), 0, 0)
    small = pl.BlockSpec((tm, TOP_K), lambda i: (i, 0))
    return pl.pallas_call(
        _router_kernel,
        out_shape=(
            jax.ShapeDtypeStruct((n * (d // 2 // LANES), LANES), jnp.uint32),
            jax.ShapeDtypeStruct((n, TOP_K), jnp.int32),
            jax.ShapeDtypeStruct((n, TOP_K), F32),
            jax.ShapeDtypeStruct((n, TOP_K), jnp.int32),
            jax.ShapeDtypeStruct((1, e), F32),
        ),
        grid=(n // tm,),
        in_specs=[
            pl.BlockSpec((tm, d), lambda i: (i, 0)),
            pl.BlockSpec((1, d), lambda i: (0, 0)),
            pl.BlockSpec((1, 1, d), seg),
            pl.BlockSpec((1, 1, d), seg),
            pl.BlockSpec((d, e), lambda i: (0, 0)),
            pl.BlockSpec((1, e), lambda i: (0, 0)),
        ],
        out_specs=(pl.BlockSpec((tm * (d // 2 // LANES), LANES), lambda i: (i, 0)), small, small, small,
                   pl.BlockSpec((1, e), lambda i: (0, 0))),
        scratch_shapes=[pltpu.VMEM((1, e), F32)],
        compiler_params=_params(("arbitrary",)),
        name="router",
    )(x, g.reshape(1, d), scale, shift, w, b.reshape(1, e))


def _moe_kernel(te_ref, tv_ref, src_ref, h_hbm, wg_ref, wu_ref, wd_ref, bg_ref, bu_ref, bd_ref, o_ref,
                gather_buf, gather_sem, xs_scr, act_scr, wg_scr, wu_scr, wd_scr):
    t = pl.program_id(0)
    j = pl.program_id(1)
    n_tiles = pl.num_programs(0) - 1
    n_j = act_scr.shape[1]
    tile_rows = xs_scr.shape[0]
    n_sub_total = tile_rows // MOE_SUB_ROWS
    out_rows = o_ref.shape[0] // tile_rows
    ta = jnp.minimum(t, n_tiles - 1)
    tb = jnp.maximum(t - 1, 0)

    def sub_blocks(tile):
        return (tv_ref[tile] + MOE_SUB_ROWS - 1) // MOE_SUB_ROWS

    def sub_rows(r):
        return pl.ds(pl.multiple_of(r * MOE_SUB_ROWS, MOE_SUB_ROWS), MOE_SUB_ROWS)

    lane_rows = xs_scr.shape[1] // 2 // LANES
    half = xs_scr.shape[1] // 2

    def token_rows(first_token, count=1):
        return pl.ds(pl.multiple_of(first_token * lane_rows, lane_rows), count * lane_rows)

    def start_gather(tile):
        def issue(i, carry):
            for q in range(GATHER_UNROLL):
                r = i * GATHER_UNROLL + q
                token = src_ref[tile * tile_rows + r]
                pltpu.make_async_copy(h_hbm.at[token_rows(token), :], gather_buf.at[token_rows(r), :], gather_sem).start()
            return carry

        lax.fori_loop(0, sub_blocks(tile) * (MOE_SUB_ROWS // GATHER_UNROLL), issue, 0)

    def wait_gather(tile):
        def wait_sub(r, carry):
            block = token_rows(r * MOE_SUB_ROWS, MOE_SUB_ROWS)
            pltpu.make_async_copy(h_hbm.at[block, :], gather_buf.at[block, :], gather_sem).wait()
            return carry

        lax.fori_loop(0, sub_blocks(tile), wait_sub, 0)

    run_a = (t < n_tiles) & (tv_ref[ta] > 0)
    run_b = (t >= 1) & (tv_ref[tb] > 0)
    n_a = jnp.where(run_a, sub_blocks(ta), 0)
    n_b = jnp.where(run_b, sub_blocks(tb), 0)
    slot_a = ta % 2
    slot_b = tb % 2

    @pl.when((j == 0) & run_a)
    def _():
        @pl.when(t == 0)
        def _():
            start_gather(0)

        wait_gather(t)

        def unpack(r, carry):
            for i in range(lane_rows):
                first = pl.multiple_of(r * MOE_SUB_ROWS * lane_rows, lane_rows) + i
                lo, hi = _unpack_bf16_pairs(gather_buf[pl.ds(first, MOE_SUB_ROWS, stride=lane_rows), :])
                xs_scr[sub_rows(r), i * LANES:(i + 1) * LANES] = lo
                xs_scr[sub_rows(r), half + i * LANES:half + (i + 1) * LANES] = hi
            return carry

        lax.fori_loop(0, n_a, unpack, 0)

        @pl.when(t + 1 < n_tiles)
        def _():
            start_gather(t + 1)

    @pl.when(run_a)
    def _():
        wg_scr[...] = wg_ref[0, 0].astype(BF16)
        wu_scr[...] = wu_ref[0, 0].astype(BF16)

    @pl.when(run_b)
    def _():
        wd_scr[...] = wd_ref[0, 0].astype(BF16)

    def span_rows(r, span):
        return pl.ds(pl.multiple_of(r * MOE_SUB_ROWS, MOE_SUB_ROWS), span * MOE_SUB_ROWS)

    def gate_up(r, span):
        rows = span_rows(r, span)
        x = xs_scr[rows, :]
        g = jnp.dot(x, wg_scr[...], preferred_element_type=F32) + bg_ref[0, j]
        u = jnp.dot(x, wu_scr[...], preferred_element_type=F32) + bu_ref[0, j]
        g = jnp.minimum(g, SWIGLU_LIMIT)
        u = jnp.clip(u, -SWIGLU_LIMIT, SWIGLU_LIMIT)
        act = (u + 1.0) * (g * jax.nn.sigmoid(SWIGLU_ALPHA * g))
        act_scr[slot_a, j, rows, :] = act.astype(BF16)

    def down(r, span):
        rows = span_rows(r, span)
        act = jnp.concatenate([act_scr[slot_b, jb, rows, :] for jb in range(n_j)], axis=1)
        y = jnp.dot(act, wd_scr[...], preferred_element_type=F32) + bd_ref[0, j]
        first = pl.multiple_of(r * MOE_SUB_ROWS * out_rows, out_rows) + j
        o_ref[pl.ds(first, span * MOE_SUB_ROWS, stride=out_rows), :] = _pack_bf16_pairs(y)

    def for_each_sub_block(n_sub, one, fused_pair):
        def pair(i, carry):
            if fused_pair:
                one(2 * i, 2)
            else:
                one(2 * i, 1)
                one(2 * i + 1, 1)
            return carry

        lax.fori_loop(0, n_sub // 2, pair, 0)

        @pl.when(n_sub % 2 == 1)
        def _():
            one(n_sub - 1, 1)

    for_each_sub_block(n_a, gate_up, fused_pair=False)
    for_each_sub_block(n_b, down, fused_pair=True)

    @pl.when((t >= 1) & (j == 0))
    def _():
        def clear(r, carry):
            block = pl.ds(pl.multiple_of(r * MOE_SUB_ROWS * out_rows, out_rows), MOE_SUB_ROWS * out_rows)
            o_ref[block, :] = jnp.zeros((MOE_SUB_ROWS * out_rows, o_ref.shape[1]), o_ref.dtype)
            return carry

        lax.fori_loop(n_b, n_sub_total, clear, 0)


def _moe_experts(h_packed, src, tile_expert, tile_valid, layer, w_gate, b_gate, w_up, b_up, w_down, b_down):
    p_rows = src.shape[0]
    _, e, d, hdim = w_gate.shape
    tm, th, tn = MOE_TILE_ROWS, MOE_HIDDEN_BLOCK, MOE_OUT_BLOCK
    n_tiles = p_rows // tm
    n_j = hdim // th
    n_out = d // tn

    assert n_j == n_out, "gate/up and down halves share the block axis"
    assert tn == 2 * LANES, "one output column block packs into one 128-lane row of bf16 pairs"

    def up_tile(t):
        return jnp.minimum(t, n_tiles - 1)

    def down_tile(t):
        return jnp.maximum(t - 1, 0)

    def up_block(t, j, tv):
        return jnp.where((t < n_tiles) & (tv[up_tile(t)] > 0), j, n_j - 1)

    def down_block(t, j, tv):
        return jnp.where((t >= 1) & (tv[down_tile(t)] > 0), j, n_out - 1)

    grid_spec = pltpu.PrefetchScalarGridSpec(
        num_scalar_prefetch=3,
        grid=(n_tiles + 1, n_j),
        in_specs=[
            pl.BlockSpec(memory_space=pl.ANY),
            pl.BlockSpec((1, 1, d, th), lambda t, j, te, tv, src: (layer, te[up_tile(t)], 0, up_block(t, j, tv))),
            pl.BlockSpec((1, 1, d, th), lambda t, j, te, tv, src: (layer, te[up_tile(t)], 0, up_block(t, j, tv))),
            pl.BlockSpec((1, 1, hdim, tn), lambda t, j, te, tv, src: (layer, te[down_tile(t)], 0, down_block(t, j, tv))),
            pl.BlockSpec((1, n_j, 1, th), lambda t, j, te, tv, src: (te[up_tile(t)], 0, 0, 0)),
            pl.BlockSpec((1, n_j, 1, th), lambda t, j, te, tv, src: (te[up_tile(t)], 0, 0, 0)),
            pl.BlockSpec((1, n_out, 1, tn), lambda t, j, te, tv, src: (te[down_tile(t)], 0, 0, 0)),
        ],
        out_specs=pl.BlockSpec((tm * n_out, LANES), lambda t, j, te, tv, src: (down_tile(t), 0)),
        scratch_shapes=[
            pltpu.VMEM((tm * (d // 2 // LANES), LANES), jnp.uint32),
            pltpu.SemaphoreType.DMA(()),
            pltpu.VMEM((tm, d), BF16),
            pltpu.VMEM((2, n_j, tm, th), BF16),
            pltpu.VMEM((d, th), BF16),
            pltpu.VMEM((d, th), BF16),
            pltpu.VMEM((hdim, tn), BF16),
        ],
    )
    return pl.pallas_call(
        _moe_kernel,
        out_shape=jax.ShapeDtypeStruct((p_rows * n_out, LANES), jnp.uint32),
        grid_spec=grid_spec,
        compiler_params=_params(("arbitrary", "arbitrary")),
        name="moe_experts",
    )(tile_expert, tile_valid, src, h_packed, w_gate, w_up, w_down,
      b_gate[layer].reshape(e, n_j, 1, th), b_up[layer].reshape(e, n_j, 1, th), b_down[layer].reshape(e, n_out, 1, tn))


def _combine_kernel(pos_ref, ys_hbm, gates_ref, x_ref, gate2_ref, norm_ref, o_ref, rows_buf, rows_sem, *, final_norm):
    i = pl.program_id(0)
    tm = x_ref.shape[0]
    slot = i % 2
    lane_rows = rows_buf.shape[1] // (TOP_K * tm)

    def token_rows(first, count=1):
        return pl.ds(pl.multiple_of(first * lane_rows, lane_rows), count * lane_rows)

    def start_gather(tile, slot):
        def issue(i, carry):
            for q in range(GATHER_UNROLL // TOP_K):
                r = i * (GATHER_UNROLL // TOP_K) + q
                for k in range(TOP_K):
                    row = pos_ref[(tile * tm + r) * TOP_K + k]
                    pltpu.make_async_copy(ys_hbm.at[token_rows(row), :], rows_buf.at[slot, token_rows(k * tm + r), :],
                                          rows_sem.at[slot]).start()
            return carry

        lax.fori_loop(0, tm // (GATHER_UNROLL // TOP_K), issue, 0)

    @pl.when(i == 0)
    def _():
        start_gather(0, 0)

    @pl.when(i + 1 < pl.num_programs(0))
    def _():
        start_gather(i + 1, 1 - slot)

    pltpu.make_async_copy(ys_hbm.at[pl.ds(0, TOP_K * tm * lane_rows), :], rows_buf.at[slot], rows_sem.at[slot]).wait()
    gates = gates_ref[...]
    for jr in range(lane_rows):
        lo = hi = None
        for k in range(TOP_K):
            w = rows_buf[slot, pl.ds(k * tm * lane_rows + jr, tm, stride=lane_rows), :]
            g = gates[:, k:k + 1]
            lo_k = g * lax.bitcast_convert_type(w << 16, F32)
            hi_k = g * lax.bitcast_convert_type(w & jnp.uint32(0xFFFF0000), F32)
            lo = lo_k if lo is None else lo + lo_k
            hi = hi_k if hi is None else hi + hi_k
        for part, cols in ((lo, pl.ds(2 * jr * LANES, LANES)), (hi, pl.ds((2 * jr + 1) * LANES, LANES))):
            o_ref[:, cols] = x_ref[:, cols] + gate2_ref[0, :, cols] * part
    if final_norm:
        x = o_ref[...]
        ms = jnp.mean(x * x, axis=-1, keepdims=True)
        o_ref[...] = x * lax.rsqrt(ms + EPS) * norm_ref[...]


def _combine(ys, pos, gates, x, gate2, norm_g, final_norm):
    n, d = x.shape
    tm = COMBINE_ROWS
    lane_rows = d // (2 * LANES)
    grid_spec = pltpu.PrefetchScalarGridSpec(
        num_scalar_prefetch=1,
        grid=(n // tm,),
        in_specs=[
            pl.BlockSpec(memory_space=pl.ANY),
            pl.BlockSpec((tm, TOP_K), lambda i, pos: (i, 0)),
            pl.BlockSpec((tm, d), lambda i, pos: (i, 0)),
            pl.BlockSpec((1, 1, d), lambda i, pos: (_segment_of_tile(i, tm), 0, 0)),
            pl.BlockSpec((1, d), lambda i, pos: (0, 0)),
        ],
        out_specs=pl.BlockSpec((tm, d), lambda i, pos: (i, 0)),
        scratch_shapes=[pltpu.VMEM((2, TOP_K * tm * lane_rows, LANES), jnp.uint32), pltpu.SemaphoreType.DMA((2,))],
    )
    return pl.pallas_call(
        functools.partial(_combine_kernel, final_norm=final_norm),
        out_shape=jax.ShapeDtypeStruct((n, d), F32),
        grid_spec=grid_spec,
        compiler_params=_params(("arbitrary",)),
        name="moe_combine",
    )(pos.reshape(-1), ys, gates, x, gate2, norm_g.reshape(1, d))


def _moe_layer(x, g, scale, shift, gate2, layer, w_router, b_router, w_gate, b_gate, w_up, b_up, w_down, b_down,
               norm_g, final_norm):
    n, d = x.shape
    e = w_router.shape[-1]
    tm = MOE_TILE_ROWS
    h, top_i, gates, rank, counts = _router(x, g, scale, shift, w_router[layer], b_router[layer])
    counts = counts[0].astype(jnp.int32)
    padded = ((counts + tm - 1) // tm) * tm
    ends = jnp.cumsum(padded)
    starts = ends - padded
    n_tiles = -(-(n * TOP_K) // tm) + e
    p_rows = n_tiles * tm
    tile_start = jnp.arange(n_tiles, dtype=jnp.int32) * tm
    n_used = ends[-1] // tm
    tile_expert = jnp.minimum(jnp.searchsorted(ends, tile_start, side="right"), e - 1).astype(jnp.int32)
    tile_valid = jnp.clip(counts[tile_expert] - (tile_start - starts[tile_expert]), 0, tm)
    tile_valid = jnp.where(tile_start < ends[-1], tile_valid, 0).astype(jnp.int32)
    last = jnp.maximum(n_used - 1, 0)
    tile_expert = jnp.where(tile_start < ends[-1], tile_expert, tile_expert[last]).astype(jnp.int32)
    pos = starts[top_i] + rank
    token = jnp.broadcast_to(jnp.arange(n, dtype=jnp.int32)[:, None], (n, TOP_K))
    src = jnp.zeros((p_rows,), jnp.int32).at[pos.reshape(-1)].set(token.reshape(-1))
    ys = _moe_experts(h, src, tile_expert, tile_valid, layer, w_gate, b_gate, w_up, b_up, w_down, b_down)
    return _combine(ys, pos, gates, x, gate2, norm_g, final_norm)


def kernel(x_prompt, x_sample, cache_k, cache_v, state_rglru, c, c_ctx, norm_mix, norm_ffn, w_mod, b_mod,
           attn_w_in, attn_b_in, attn_w_out, attn_b_out, attn_sink,
           rnn_w_in, rnn_b_in, rnn_conv_w, rnn_conv_b, rnn_w_a, rnn_b_a, rnn_w_x, rnn_b_x,
           rnn_lambda, rnn_w_out, rnn_b_out,
           moe_w_router, moe_b_router, moe_w_gate, moe_b_gate, moe_w_up, moe_b_up,
           moe_w_down, moe_b_down, final_norm):
    d = D_MODEL
    n_ctx = BATCH * SEQ
    attn_w = N_HEADS * HEAD_DIM
    kv_w = N_KV_HEADS * HEAD_DIM
    x = jnp.concatenate([x_prompt.reshape(n_ctx, d), x_sample.reshape(DEC_BATCH * DEC_SEQ, d)], axis=0)
    cond = jnp.concatenate([c_ctx[None, :], c, jnp.zeros((SUBLANES - 1 - DEC_BATCH, d), F32)], axis=0)
    mods = _modulation(cond, w_mod, b_mod)

    def mod(l, k):
        return mods[l, :, k * d:(k + 1) * d].reshape(SUBLANES, 1, d)

    new_k, new_v, new_s = [], [], []
    for l in range(DEPTH):
        j = l // 2
        sh1, sc1, g1, sh2, sc2, g2 = [mod(l, k) for k in range(N_MOD)]
        if l % 2 == 0:
            p = _proj_in(x, norm_mix[l], sc1, sh1, attn_w_in[j], attn_b_in[j], tn=PROJ_IN_ATTN_COLS)
            new_k.append(p[:n_ctx, attn_w:attn_w + kv_w].reshape(BATCH, SEQ, N_KV_HEADS, HEAD_DIM))
            new_v.append(p[:n_ctx, attn_w + kv_w:attn_w + 2 * kv_w].reshape(BATCH, SEQ, N_KV_HEADS, HEAD_DIM))
            mix_ctx = _ctx_mixer(p, attn_sink[j])
            mix_lat = _lat_mixer(p, cache_k[:, j].reshape(DEC_BATCH, PAST_LEN, kv_w),
                                 cache_v[:, j].reshape(DEC_BATCH, PAST_LEN, kv_w), attn_sink[j])
            x = _proj_out(mix_ctx, mix_lat, attn_w_out[j], attn_b_out[j], g1, x)
        else:
            p = _proj_in(x, norm_mix[l], sc1, sh1, rnn_w_in[j], rnn_b_in[j], tn=PROJ_IN_RNN_COLS)
            args = (rnn_conv_w[j], rnn_conv_b[j], rnn_w_a[j], rnn_b_a[j], rnn_w_x[j], rnn_b_x[j], rnn_lambda[j])
            h0_ctx = jnp.zeros((BATCH, 2, d), F32)
            mix_ctx, st = _rnn_core(p, 0, n_ctx, h0_ctx, *args, n_seq=SUBLANES, seq_len=SEQ)
            mix_lat, _ = _rnn_core(p, n_ctx, DEC_BATCH * DEC_SEQ, state_rglru[:, j], *args,
                                   n_seq=DEC_BATCH, seq_len=DEC_SEQ)
            new_s.append(st)
            x = _proj_out(mix_ctx, mix_lat, rnn_w_out[j], rnn_b_out[j], g1, x)
        x = _moe_layer(x, norm_ffn[l], sc2, sh2, g2, l, moe_w_router, moe_b_router,
                       moe_w_gate, moe_b_gate, moe_w_up, moe_b_up, moe_w_down, moe_b_down,
                       final_norm, final_norm=(l == DEPTH - 1))
    y = x
    y_prompt = y[:n_ctx].reshape(BATCH, SEQ, d)
    y_sample = y[n_ctx:].reshape(DEC_BATCH, DEC_SEQ, d)
    return (y_prompt, y_sample, jnp.stack(new_k, axis=1), jnp.stack(new_v, axis=1), jnp.stack(new_s, axis=1))
```

```python
import functools
import math

import numpy as np
import jax
import jax.numpy as jnp
from jax import lax
from jax.experimental import pallas as pl
from jax.experimental.pallas import tpu as pltpu

D_MODEL = 2048
BATCH = 32
SEQ = 256
DEPTH = 2
DEC_BATCH = 2
DEC_SEQ = 1024
PAST_LEN = 256
GRID_W = 64
N_HEADS = 16
N_KV_HEADS = 2
HEAD_DIM = 64
WINDOW = 128
ROPE_THETA = 10000.0
N_FOURIER_GROUPS = 4
N_RNN_BLOCKS = 8
CONV_WIDTH = 4
CONV_LEFT = 2
RG_C = 8.0
N_EXPERTS = 32
TOP_K = 4
SWIGLU_LIMIT = 7.0
SWIGLU_ALPHA = 1.702
N_MOD = 6
EPS = 1e-6

LANES = 128
SUBLANES = 8
VMEM_LIMIT_BYTES = 56 * 1024 * 1024

MOE_TILE_ROWS = 1536
MOE_SUB_ROWS = 256
MOE_HIDDEN_BLOCK = 256
MOE_OUT_BLOCK = 256
GATHER_UNROLL = 8
SCAN_ROW_PAD = 8
SCAN_UNROLL = 4
ROW_TILE = 512
PROJ_ROWS = 1024
COMBINE_ROWS = 128
PROJ_OUT_COLS = 512
PROJ_IN_ATTN_COLS = 768
PROJ_IN_RNN_COLS = 1024

F32 = jnp.float32
BF16 = jnp.bfloat16


def _params(semantics):
    return pltpu.CompilerParams(dimension_semantics=semantics, vmem_limit_bytes=VMEM_LIMIT_BYTES)


def _segment_of_tile(i, tile_rows):
    n_ctx_tiles = (BATCH * SEQ) // tile_rows
    tiles_per_latent = DEC_SEQ // tile_rows
    return jnp.where(i < n_ctx_tiles, 0, 1 + (i - n_ctx_tiles) // tiles_per_latent)


def _rms_modulate(x, g, scale, shift):
    ms = jnp.mean(x * x, axis=-1, keepdims=True)
    return (x * lax.rsqrt(ms + EPS) * g) * (1.0 + scale) + shift


def _modulation_kernel(c_ref, w_ref, b_ref, o_ref):
    c = c_ref[...]
    s = (c * jax.nn.sigmoid(c)).astype(BF16)
    o_ref[0] = jnp.dot(s, w_ref[0].astype(BF16), preferred_element_type=F32) + b_ref[0]


def _modulation(cond, w_mod, b_mod):
    d = D_MODEL
    tn = 512
    n_out = N_MOD * d
    return pl.pallas_call(
        _modulation_kernel,
        out_shape=jax.ShapeDtypeStruct((DEPTH, SUBLANES, n_out), F32),
        grid=(DEPTH, n_out // tn),
        in_specs=[
            pl.BlockSpec((SUBLANES, d), lambda l, j: (0, 0)),
            pl.BlockSpec((1, d, tn), lambda l, j: (l, 0, j)),
            pl.BlockSpec((1, 1, tn), lambda l, j: (l, 0, j)),
        ],
        out_specs=pl.BlockSpec((1, SUBLANES, tn), lambda l, j: (l, 0, j)),
        compiler_params=_params(("arbitrary", "arbitrary")),
        name="modulation",
    )(cond, w_mod, b_mod.reshape(DEPTH, 1, n_out))


def _proj_in_kernel(x_ref, g_ref, sc_ref, sh_ref, w_ref, b_ref, o_ref, h_ref):
    @pl.when(pl.program_id(1) == 0)
    def _():
        h_ref[...] = _rms_modulate(x_ref[...], g_ref[...], sc_ref[0], sh_ref[0]).astype(BF16)

    o_ref[...] = jnp.dot(h_ref[...], w_ref[...].astype(BF16), preferred_element_type=F32) + b_ref[...]


def _proj_in(x, g, scale, shift, w, b, tn):
    n, d = x.shape
    n_out = w.shape[1]
    tm = PROJ_ROWS
    seg = lambda i, j: (_segment_of_tile(i, tm), 0, 0)
    return pl.pallas_call(
        _proj_in_kernel,
        out_shape=jax.ShapeDtypeStruct((n, n_out), F32),
        grid=(n // tm, n_out // tn),
        in_specs=[
            pl.BlockSpec((tm, d), lambda i, j: (i, 0)),
            pl.BlockSpec((1, d), lambda i, j: (0, 0)),
            pl.BlockSpec((1, 1, d), seg),
            pl.BlockSpec((1, 1, d), seg),
            pl.BlockSpec((d, tn), lambda i, j: (0, j)),
            pl.BlockSpec((1, tn), lambda i, j: (0, j)),
        ],
        out_specs=pl.BlockSpec((tm, tn), lambda i, j: (i, j)),
        scratch_shapes=[pltpu.VMEM((tm, d), BF16)],
        compiler_params=_params(("arbitrary", "arbitrary")),
        name="proj_in",
    )(x, g.reshape(1, d), scale, shift, w, b.reshape(1, n_out))


def _proj_out_kernel(a_ctx_ref, a_lat_ref, w_ref, b_ref, gate_ref, res_ref, o_ref, *, n_ctx_tiles):
    def project(a_ref):
        y = jnp.dot(a_ref[...], w_ref[...].astype(BF16), preferred_element_type=F32) + b_ref[...]
        o_ref[...] = res_ref[...] + gate_ref[0] * y

    @pl.when(pl.program_id(0) < n_ctx_tiles)
    def _():
        project(a_ctx_ref)

    @pl.when(pl.program_id(0) >= n_ctx_tiles)
    def _():
        project(a_lat_ref)


def _proj_out(a_ctx, a_lat, w, b, gate, res):
    n, d = res.shape
    k = a_ctx.shape[1]
    tm, tn = PROJ_ROWS, PROJ_OUT_COLS
    n_ctx_tiles = a_ctx.shape[0] // tm
    return pl.pallas_call(
        functools.partial(_proj_out_kernel, n_ctx_tiles=n_ctx_tiles),
        out_shape=jax.ShapeDtypeStruct((n, d), F32),
        grid=(n // tm, d // tn),
        in_specs=[
            pl.BlockSpec((tm, k), lambda i, j: (jnp.minimum(i, n_ctx_tiles - 1), 0)),
            pl.BlockSpec((tm, k), lambda i, j: (jnp.maximum(i - n_ctx_tiles, 0), 0)),
            pl.BlockSpec((k, tn), lambda i, j: (0, j)),
            pl.BlockSpec((1, tn), lambda i, j: (0, j)),
            pl.BlockSpec((1, 1, tn), lambda i, j: (_segment_of_tile(i, tm), 0, j)),
            pl.BlockSpec((tm, tn), lambda i, j: (i, j)),
        ],
        out_specs=pl.BlockSpec((tm, tn), lambda i, j: (i, j)),
        compiler_params=_params(("arbitrary", "arbitrary")),
        name="proj_out",
    )(a_ctx, a_lat, w, b.reshape(1, d), gate, res)


def _dot_nt(a, b):
    return lax.dot_general(a, b, (((1,), (1,)), ((), ())), preferred_element_type=F32)


def _head_pair_operands(k, v, group):
    lane = lax.broadcasted_iota(jnp.int32, k.shape, 1)
    low = lane < HEAD_DIM
    k_sw = pltpu.roll(k, HEAD_DIM, 1)
    v_sw = pltpu.roll(v, HEAD_DIM, 1)
    if group == 0:
        kd = jnp.where(low, k, k_sw)
        vd = jnp.where(low, v, v_sw)
    else:
        kd = jnp.where(low, k_sw, k)
        vd = jnp.where(low, v_sw, v)
    v_lo = jnp.where(low, vd, 0.0).astype(BF16)
    v_hi = jnp.where(low, 0.0, vd).astype(BF16)
    return kd.astype(BF16), v_lo, v_hi


def _split_pair(q2):
    lane = lax.broadcasted_iota(jnp.int32, q2.shape, 1)
    low = lane < HEAD_DIM
    qs = q2 * (HEAD_DIM ** -0.5)
    return jnp.where(low, qs, 0.0).astype(BF16), jnp.where(low, 0.0, qs).astype(BF16)


def _softmax_pv(scores, values, sink):
    m = jnp.full((scores[0].shape[0], 1), sink, F32)
    for s in scores:
        m = jnp.maximum(m, jnp.max(s, axis=-1, keepdims=True))
    den = jnp.exp(sink - m)
    ps = []
    for s in scores:
        p = jnp.exp(s - m)
        den = den + jnp.sum(p, axis=-1, keepdims=True)
        ps.append(p)
    out = None
    for p, v in zip(ps, values):
        o = jnp.dot((p / den).astype(BF16), v, preferred_element_type=F32)
        out = o if out is None else out + o
    return out


def _dft_matrices(t):
    idx = np.arange(t)
    ang = 2.0 * np.pi * ((idx[:, None] * idx[None, :]) % t) / t
    m = np.concatenate([np.cos(ang), np.sin(ang)], axis=0) / math.sqrt(t)
    return jnp.asarray(m, dtype=BF16)


def _dft_channel_matrix(c):
    idx = np.arange(c)
    ang = 2.0 * np.pi * ((idx[:, None] * idx[None, :]) % c) / c
    m = np.concatenate([np.cos(ang), -np.sin(ang)], axis=0) / math.sqrt(c)
    return jnp.asarray(m, dtype=BF16)


def _fourier_group(f_g, ts_ref, cs_ref):
    t = f_g.shape[0]
    ab = jnp.dot(ts_ref[...], f_g.astype(BF16), preferred_element_type=F32)
    lhs = jnp.concatenate([ab[:t], ab[t:]], axis=1).astype(BF16)
    return jnp.dot(lhs, cs_ref[...], preferred_element_type=F32)


def _ctx_mixer_kernel(sink_ref, p_ref, ts_ref, cs_ref, o_ref):
    attn_w = N_HEADS * HEAD_DIM
    kv_w = N_KV_HEADS * HEAD_DIM
    pair_w = 2 * HEAD_DIM
    group_heads = N_HEADS // N_KV_HEADS
    k = p_ref[:, attn_w:attn_w + kv_w]
    v = p_ref[:, attn_w + kv_w:attn_w + 2 * kv_w]
    for g in range(N_KV_HEADS):
        kd, v_lo, v_hi = _head_pair_operands(k, v, g)
        for i in range(group_heads // 2):
            pair = g * (group_heads // 2) + i
            q_lo, q_hi = _split_pair(p_ref[:, pair * pair_w:(pair + 1) * pair_w])
            o = _softmax_pv([_dot_nt(q_lo, kd)], [v_lo], sink_ref[2 * pair])
            o = o + _softmax_pv([_dot_nt(q_hi, kd)], [v_hi], sink_ref[2 * pair + 1])
            o_ref[:, pair * pair_w:(pair + 1) * pair_w] = o.astype(o_ref.dtype)
    f0 = attn_w + 2 * kv_w
    fg = (D_MODEL - attn_w) // N_FOURIER_GROUPS
    for g in range(N_FOURIER_GROUPS):
        z = _fourier_group(p_ref[:, f0 + g * fg:f0 + (g + 1) * fg], ts_ref, cs_ref)
        o_ref[:, attn_w + g * fg:attn_w + (g + 1) * fg] = z.astype(o_ref.dtype)


def _ctx_mixer(p, sink):
    n, width = BATCH * SEQ, p.shape[1]
    fg = (D_MODEL - N_HEADS * HEAD_DIM) // N_FOURIER_GROUPS
    return pl.pallas_call(
        _ctx_mixer_kernel,
        out_shape=jax.ShapeDtypeStruct((n, D_MODEL), BF16),
        grid=(n // SEQ,),
        in_specs=[
            pl.BlockSpec(memory_space=pltpu.SMEM),
            pl.BlockSpec((SEQ, width), lambda b: (b, 0)),
            pl.BlockSpec((2 * SEQ, SEQ), lambda b: (0, 0)),
            pl.BlockSpec((2 * fg, fg), lambda b: (0, 0)),
        ],
        out_specs=pl.BlockSpec((SEQ, D_MODEL), lambda b: (b, 0)),
        compiler_params=_params(("arbitrary",)),
        name="ctx_mixer",
    )(sink, p, _dft_matrices(SEQ), _dft_channel_matrix(fg))


def _rope_tables():
    rows = DEC_SEQ // GRID_W
    row = np.repeat(np.arange(rows, dtype=np.float32), GRID_W)
    col = np.tile(np.arange(GRID_W, dtype=np.float32), rows)
    n_freq = HEAD_DIM // 4
    inv = jnp.asarray(ROPE_THETA, F32) ** (-jnp.arange(n_freq, dtype=F32) / n_freq)
    ang = jnp.concatenate([row[:, None] * inv, col[:, None] * inv], axis=-1)
    cos = jnp.repeat(jnp.cos(ang), 2, axis=-1)
    sin = jnp.repeat(jnp.sin(ang), 2, axis=-1)
    sign = jnp.tile(jnp.asarray([-1.0, 1.0], F32), HEAD_DIM // 2)
    return jnp.tile(cos, (1, 2)), jnp.tile(sin * sign, (1, 2))


def _rope(x, cos, sin_signed):
    lane = lax.broadcasted_iota(jnp.int32, x.shape, 1)
    width = x.shape[1]
    partner = jnp.where(lane % 2 == 0, pltpu.roll(x, width - 1, 1), pltpu.roll(x, 1, 1))
    return x * cos + partner * sin_signed


def _lat_mixer_kernel(sink_ref, p_ref, ck_ref, cv_ref, cos_ref, sin_ref, ts_ref, cs_ref, o_ref, q_scr, k_scr):
    attn_w = N_HEADS * HEAD_DIM
    kv_w = N_KV_HEADS * HEAD_DIM
    pair_w = 2 * HEAD_DIM
    group_heads = N_HEADS // N_KV_HEADS
    q_rows = 256
    cos = cos_ref[...]
    sin = sin_ref[...]
    for pair in range(N_HEADS // 2):
        q_scr[:, pair * pair_w:(pair + 1) * pair_w] = _rope(p_ref[:, pair * pair_w:(pair + 1) * pair_w], cos, sin)
    k_scr[...] = _rope(p_ref[:, attn_w:attn_w + kv_w], cos, sin)
    v = p_ref[:, attn_w + kv_w:attn_w + 2 * kv_w]
    ck = ck_ref[0]
    cv = cv_ref[0]
    k = k_scr[...]
    ops = []
    for g in range(N_KV_HEADS):
        ops.append(_head_pair_operands(k, v, g) + _head_pair_operands(ck, cv, g))

    def chunk(c, carry):
        r0 = pl.multiple_of(c * q_rows, q_rows)
        qi = r0 + lax.broadcasted_iota(jnp.int32, (q_rows, DEC_SEQ), 0)
        kj = lax.broadcasted_iota(jnp.int32, (q_rows, DEC_SEQ), 1)
        valid = jnp.abs(qi - kj) <= WINDOW
        for g in range(N_KV_HEADS):
            kd, v_lo, v_hi, ckd, cv_lo, cv_hi = ops[g]
            for i in range(group_heads // 2):
                pair = g * (group_heads // 2) + i
                q_lo, q_hi = _split_pair(q_scr[pl.ds(r0, q_rows), pair * pair_w:(pair + 1) * pair_w])
                s_lo = jnp.where(valid, _dot_nt(q_lo, kd), -jnp.inf)
                o = _softmax_pv([_dot_nt(q_lo, ckd), s_lo], [cv_lo, v_lo], sink_ref[2 * pair])
                s_hi = jnp.where(valid, _dot_nt(q_hi, kd), -jnp.inf)
                o = o + _softmax_pv([_dot_nt(q_hi, ckd), s_hi], [cv_hi, v_hi], sink_ref[2 * pair + 1])
                o_ref[pl.ds(r0, q_rows), pair * pair_w:(pair + 1) * pair_w] = o.astype(o_ref.dtype)
        return carry

    lax.fori_loop(0, DEC_SEQ // q_rows, chunk, 0)
    f0 = attn_w + 2 * kv_w
    fg = (D_MODEL - attn_w) // N_FOURIER_GROUPS
    for g in range(N_FOURIER_GROUPS):
        z = _fourier_group(p_ref[:, f0 + g * fg:f0 + (g + 1) * fg], ts_ref, cs_ref)
        o_ref[:, attn_w + g * fg:attn_w + (g + 1) * fg] = z.astype(o_ref.dtype)


def _lat_mixer(p, cache_k, cache_v, sink):
    n, width = DEC_BATCH * DEC_SEQ, p.shape[1]
    first = (BATCH * SEQ) // DEC_SEQ
    kv_w = N_KV_HEADS * HEAD_DIM
    attn_w = N_HEADS * HEAD_DIM
    fg = (D_MODEL - attn_w) // N_FOURIER_GROUPS
    cos, sin = _rope_tables()
    return pl.pallas_call(
        _lat_mixer_kernel,
        out_shape=jax.ShapeDtypeStruct((n, D_MODEL), BF16),
        grid=(n // DEC_SEQ,),
        in_specs=[
            pl.BlockSpec(memory_space=pltpu.SMEM),
            pl.BlockSpec((DEC_SEQ, width), lambda b: (first + b, 0)),
            pl.BlockSpec((1, PAST_LEN, kv_w), lambda b: (b, 0, 0)),
            pl.BlockSpec((1, PAST_LEN, kv_w), lambda b: (b, 0, 0)),
            pl.BlockSpec((DEC_SEQ, 2 * HEAD_DIM), lambda b: (0, 0)),
            pl.BlockSpec((DEC_SEQ, 2 * HEAD_DIM), lambda b: (0, 0)),
            pl.BlockSpec((2 * DEC_SEQ, DEC_SEQ), lambda b: (0, 0)),
            pl.BlockSpec((2 * fg, fg), lambda b: (0, 0)),
        ],
        out_specs=pl.BlockSpec((DEC_SEQ, D_MODEL), lambda b: (b, 0)),
        scratch_shapes=[pltpu.VMEM((DEC_SEQ, attn_w), F32), pltpu.VMEM((DEC_SEQ, kv_w), F32)],
        compiler_params=_params(("arbitrary",)),
        name="lat_mixer",
    )(sink, p, cache_k, cache_v, cos, sin, _dft_matrices(DEC_SEQ), _dft_channel_matrix(fg))


def _rnn_core_kernel(xr_ref, gr_ref, cw_ref, cb_ref, wa_ref, ba_ref, wx_ref, bx_ref, lam_ref, h0_ref,
                     y_ref, st_ref, a_scr, b_scr, h_scr, *, n_seq, seq_len):
    rows = n_seq * seq_len
    n_lb = xr_ref.shape[1] // LANES
    xr = xr_ref[...]
    t_idx = lax.broadcasted_iota(jnp.int32, (rows, 1), 0) % seq_len
    xc = jnp.broadcast_to(cb_ref[...], xr.shape)
    for tap in range(CONV_WIDTH):
        off = tap - CONV_LEFT
        shifted = xr if off == 0 else pltpu.roll(xr, (-off) % rows, 0)
        valid = (t_idx + off >= 0) & (t_idx + off < seq_len)
        xc = xc + jnp.where(valid, shifted, 0.0) * cw_ref[tap:tap + 1, :]
    xcb = xc.astype(BF16)
    pitch = seq_len + SCAN_ROW_PAD

    def sigmoid(z):
        return 0.5 * jnp.tanh(0.5 * z) + 0.5

    def seq_rows(s):
        return pl.ds(s * pitch, seq_len)

    for d in range(2):
        r = sigmoid(jnp.dot(xcb, wa_ref[d, 0].astype(BF16), preferred_element_type=F32) + ba_ref[d:d + 1, :])
        gi = sigmoid(jnp.dot(xcb, wx_ref[d, 0].astype(BF16), preferred_element_type=F32) + bx_ref[d:d + 1, :])
        neg_lam = -lam_ref[d:d + 1, :]
        softplus = jnp.maximum(neg_lam, 0.0) + jnp.log1p(jnp.exp(-jnp.abs(neg_lam)))
        a = jnp.exp(-RG_C * r * softplus)
        b = jnp.sqrt(1.0 - a * a) * (gi * xc)
        for lb in range(n_lb):
            for s in range(n_seq):
                a_scr[d * n_lb + lb, seq_rows(s), :] = a[s * seq_len:(s + 1) * seq_len, lb * LANES:(lb + 1) * LANES]
                b_scr[d * n_lb + lb, seq_rows(s), :] = b[s * seq_len:(s + 1) * seq_len, lb * LANES:(lb + 1) * LANES]

    def time_rows(t):
        return pl.ds(t, n_seq, stride=pitch)

    def lane_block(ref, k, lb):
        return ref[:, k, lb * LANES:(lb + 1) * LANES]

    def fwd(t, hs):
        out = []
        for lb in range(n_lb):
            h = a_scr[lb, time_rows(t), :] * hs[lb] + b_scr[lb, time_rows(t), :]
            h_scr[lb, time_rows(t), :] = h
            out.append(h)
        return tuple(out)

    hs = lax.fori_loop(0, seq_len, fwd, tuple(lane_block(h0_ref, 0, lb) for lb in range(n_lb)), unroll=SCAN_UNROLL)
    for lb in range(n_lb):
        st_ref[:, 0, lb * LANES:(lb + 1) * LANES] = hs[lb]

    def bwd(i, hs):
        t = seq_len - 1 - i
        out = []
        for lb in range(n_lb):
            h = a_scr[n_lb + lb, time_rows(t), :] * hs[lb] + b_scr[n_lb + lb, time_rows(t), :]
            h_scr[lb, time_rows(t), :] = h_scr[lb, time_rows(t), :] + h
            out.append(h)
        return tuple(out)

    hs = lax.fori_loop(0, seq_len, bwd, tuple(lane_block(h0_ref, 1, lb) for lb in range(n_lb)), unroll=SCAN_UNROLL)
    for lb in range(n_lb):
        st_ref[:, 1, lb * LANES:(lb + 1) * LANES] = hs[lb]
    for lb in range(n_lb):
        for s in range(n_seq):
            gr = gr_ref[s * seq_len:(s + 1) * seq_len, lb * LANES:(lb + 1) * LANES]
            gelu = 0.5 * gr * (1.0 + jnp.tanh(math.sqrt(2.0 / math.pi) * (gr + 0.044715 * (gr * gr * gr))))
            y_ref[s * seq_len:(s + 1) * seq_len, lb * LANES:(lb + 1) * LANES] = (
                gelu * h_scr[lb, seq_rows(s), :]).astype(y_ref.dtype)


def _rnn_core(p, first_row, n, h0, conv_w, conv_b, w_a, b_a, w_x, b_x, lam, n_seq, seq_len):
    d_rnn = p.shape[1] // 2
    cb = d_rnn // N_RNN_BLOCKS
    rows = n_seq * seq_len
    n_batch = n // seq_len
    first = first_row // rows
    scan_rows = n_seq * (seq_len + SCAN_ROW_PAD)
    kern = functools.partial(_rnn_core_kernel, n_seq=n_seq, seq_len=seq_len)
    return pl.pallas_call(
        kern,
        out_shape=(jax.ShapeDtypeStruct((n, d_rnn), BF16), jax.ShapeDtypeStruct((n_batch, 2, d_rnn), F32)),
        grid=(n // rows, N_RNN_BLOCKS),
        in_specs=[
            pl.BlockSpec((rows, cb), lambda i, c: (first + i, c)),
            pl.BlockSpec((rows, cb), lambda i, c: (first + i, N_RNN_BLOCKS + c)),
            pl.BlockSpec((CONV_WIDTH, cb), lambda i, c: (0, c)),
            pl.BlockSpec((1, cb), lambda i, c: (0, c)),
            pl.BlockSpec((2, 1, cb, cb), lambda i, c: (0, c, 0, 0)),
            pl.BlockSpec((2, cb), lambda i, c: (0, c)),
            pl.BlockSpec((2, 1, cb, cb), lambda i, c: (0, c, 0, 0)),
            pl.BlockSpec((2, cb), lambda i, c: (0, c)),
            pl.BlockSpec((2, cb), lambda i, c: (0, c)),
            pl.BlockSpec((n_seq, 2, cb), lambda i, c: (i, 0, c)),
        ],
        out_specs=(
            pl.BlockSpec((rows, cb), lambda i, c: (i, c)),
            pl.BlockSpec((n_seq, 2, cb), lambda i, c: (i, 0, c)),
        ),
        scratch_shapes=[pltpu.VMEM((2 * cb // LANES, scan_rows, LANES), F32),
                        pltpu.VMEM((2 * cb // LANES, scan_rows, LANES), F32),
                        pltpu.VMEM((cb // LANES, scan_rows, LANES), F32)],
        compiler_params=_params(("arbitrary", "arbitrary")),
        name="rnn_core",
    )(p, p, conv_w, conv_b.reshape(1, d_rnn), w_a, b_a, w_x, b_x, lam, h0)


def _pack_bf16_pairs(h):
    half = h.shape[1] // 2
    bits = lax.bitcast_convert_type(h.astype(BF16).astype(F32), jnp.uint32)
    return (bits[:, :half] >> 16) | (bits[:, half:] & jnp.uint32(0xFFFF0000))


def _unpack_bf16_pairs(w):
    lo = lax.bitcast_convert_type(w << 16, F32)
    hi = lax.bitcast_convert_type(w & jnp.uint32(0xFFFF0000), F32)
    return lo.astype(BF16), hi.astype(BF16)


def _router_kernel(x_ref, g_ref, sc_ref, sh_ref, w_ref, b_ref, h_ref, idx_ref, gate_ref, rank_ref, cnt_ref, run_ref):
    tm = x_ref.shape[0]

    @pl.when(pl.program_id(0) == 0)
    def _():
        run_ref[...] = jnp.zeros_like(run_ref)

    h = _rms_modulate(x_ref[...], g_ref[...], sc_ref[0], sh_ref[0])
    packed = _pack_bf16_pairs(h)
    lane_rows = packed.shape[1] // LANES
    for i in range(lane_rows):
        h_ref[pl.ds(i, tm, stride=lane_rows), :] = packed[:, i * LANES:(i + 1) * LANES]
    w = w_ref[...]
    h_hi, w_hi = h.astype(BF16), w.astype(BF16)
    h_lo = (h - h_hi.astype(F32)).astype(BF16)
    w_lo = (w - w_hi.astype(F32)).astype(BF16)
    logits = (jnp.dot(h_hi, w_hi, preferred_element_type=F32)
              + (jnp.dot(h_lo, w_hi, preferred_element_type=F32) + jnp.dot(h_hi, w_lo, preferred_element_type=F32))
              + b_ref[...])
    lane = lax.broadcasted_iota(jnp.int32, logits.shape, 1).astype(F32)
    col = lax.broadcasted_iota(jnp.int32, (tm, TOP_K), 1)
    chosen = jnp.zeros(logits.shape, F32)
    top_v, top_i, hits = [], [], []
    work = logits
    for _ in range(TOP_K):
        m = jnp.max(work, axis=-1, keepdims=True)
        first = jnp.min(jnp.where(work == m, lane, float(N_EXPERTS)), axis=-1, keepdims=True)
        hit = lane == first
        work = jnp.where(hit, -jnp.inf, work)
        chosen = jnp.where(hit, 1.0, chosen)
        top_v.append(m)
        top_i.append(first)
        hits.append(hit)
    exps = [jnp.exp(v - top_v[0]) for v in top_v]
    den = exps[0]
    for e in exps[1:]:
        den = den + e
    ri = lax.broadcasted_iota(jnp.int32, (tm, tm), 0)
    ci = lax.broadcasted_iota(jnp.int32, (tm, tm), 1)
    before = (ci < ri).astype(BF16)
    rank_all = jnp.dot(before, chosen.astype(BF16), preferred_element_type=F32) + run_ref[...]
    idx_out = jnp.zeros((tm, TOP_K), F32)
    gate_out = jnp.zeros((tm, TOP_K), F32)
    rank_out = jnp.zeros((tm, TOP_K), F32)
    for k in range(TOP_K):
        rk = jnp.sum(jnp.where(hits[k], rank_all, 0.0), axis=-1, keepdims=True)
        idx_out = jnp.where(col == k, top_i[k], idx_out)
        gate_out = jnp.where(col == k, exps[k] / den, gate_out)
        rank_out = jnp.where(col == k, rk, rank_out)
    idx_ref[...] = idx_out.astype(jnp.int32)
    gate_ref[...] = gate_out
    rank_ref[...] = rank_out.astype(jnp.int32)
    run_ref[...] = run_ref[...] + jnp.sum(chosen, axis=0, keepdims=True)
    cnt_ref[...] = run_ref[...]


def _router(x, g, scale, shift, w, b):
    n, d = x.shape
    e = w.shape[1]
    tm = ROW_TILE
    seg = lambda i: (_segment_of_tile(i, tm), 0, 0)
    small = pl.BlockSpec((tm, TOP_K), lambda i: (i, 0))
    return pl.pallas_call(
        _router_kernel,
        out_shape=(
            jax.ShapeDtypeStruct((n * (d // 2 // LANES), LANES), jnp.uint32),
            jax.ShapeDtypeStruct((n, TOP_K), jnp.int32),
            jax.ShapeDtypeStruct((n, TOP_K), F32),
            jax.ShapeDtypeStruct((n, TOP_K), jnp.int32),
            jax.ShapeDtypeStruct((1, e), F32),
        ),
        grid=(n // tm,),
        in_specs=[
            pl.BlockSpec((tm, d), lambda i: (i, 0)),
            pl.BlockSpec((1, d), lambda i: (0, 0)),
            pl.BlockSpec((1, 1, d), seg),
            pl.BlockSpec((1, 1, d), seg),
            pl.BlockSpec((d, e), lambda i: (0, 0)),
            pl.BlockSpec((1, e), lambda i: (0, 0)),
        ],
        out_specs=(pl.BlockSpec((tm * (d // 2 // LANES), LANES), lambda i: (i, 0)), small, small, small,
                   pl.BlockSpec((1, e), lambda i: (0, 0))),
        scratch_shapes=[pltpu.VMEM((1, e), F32)],
        compiler_params=_params(("arbitrary",)),
        name="router",
    )(x, g.reshape(1, d), scale, shift, w, b.reshape(1, e))


def _moe_kernel(te_ref, tv_ref, pos_ref, h_hbm, wg_ref, wu_ref, wd_ref, bg_ref, bu_ref, bd_ref, o_ref,
                src_ref, gather_buf, gather_sem, xs_scr, act_scr, wg_scr, wu_scr, wd_scr):
    t = pl.program_id(0)
    j = pl.program_id(1)
    n_tiles = pl.num_programs(0) - 1
    n_j = act_scr.shape[1]
    tile_rows = xs_scr.shape[0]
    n_sub_total = tile_rows // MOE_SUB_ROWS
    out_rows = o_ref.shape[0] // tile_rows
    ta = jnp.minimum(t, n_tiles - 1)
    tb = jnp.maximum(t - 1, 0)

    def sub_blocks(tile):
        return (tv_ref[tile] + MOE_SUB_ROWS - 1) // MOE_SUB_ROWS

    def sub_rows(r):
        return pl.ds(pl.multiple_of(r * MOE_SUB_ROWS, MOE_SUB_ROWS), MOE_SUB_ROWS)

    lane_rows = xs_scr.shape[1] // 2 // LANES
    half = xs_scr.shape[1] // 2

    def token_rows(first_token, count=1):
        return pl.ds(pl.multiple_of(first_token * lane_rows, lane_rows), count * lane_rows)

    def start_gather(tile):
        def issue(i, carry):
            for q in range(GATHER_UNROLL):
                r = i * GATHER_UNROLL + q
                token = src_ref[tile * tile_rows + r]
                pltpu.make_async_copy(h_hbm.at[token_rows(token), :], gather_buf.at[token_rows(r), :], gather_sem).start()
            return carry

        lax.fori_loop(0, sub_blocks(tile) * (MOE_SUB_ROWS // GATHER_UNROLL), issue, 0)

    def wait_gather(tile):
        def wait_sub(r, carry):
            block = token_rows(r * MOE_SUB_ROWS, MOE_SUB_ROWS)
            pltpu.make_async_copy(h_hbm.at[block, :], gather_buf.at[block, :], gather_sem).wait()
            return carry

        lax.fori_loop(0, sub_blocks(tile), wait_sub, 0)

    @pl.when((t == 0) & (j == 0))
    def _():
        tokens_per_trip = GATHER_UNROLL // TOP_K

        def put(i, carry):
            for q in range(tokens_per_trip):
                token = i * tokens_per_trip + q
                for k in range(TOP_K):
                    src_ref[pos_ref[token * TOP_K + k]] = token
            return carry

        lax.fori_loop(0, pos_ref.shape[0] // GATHER_UNROLL, put, 0)

        def pad_tile(tile, carry):
            def pad(r, inner):
                src_ref[tile * tile_rows + r] = 0
                return inner

            lax.fori_loop(tv_ref[tile], sub_blocks(tile) * MOE_SUB_ROWS, pad, 0)
            return carry

        lax.fori_loop(0, n_tiles, pad_tile, 0)

    run_a = (t < n_tiles) & (tv_ref[ta] > 0)
    run_b = (t >= 1) & (tv_ref[tb] > 0)
    n_a = jnp.where(run_a, sub_blocks(ta), 0)
    n_b = jnp.where(run_b, sub_blocks(tb), 0)
    slot_a = ta % 2
    slot_b = tb % 2

    @pl.when((j == 0) & run_a)
    def _():
        @pl.when(t == 0)
        def _():
            start_gather(0)

        wait_gather(t)

        def unpack(r, carry):
            for i in range(lane_rows):
                first = pl.multiple_of(r * MOE_SUB_ROWS * lane_rows, lane_rows) + i
                lo, hi = _unpack_bf16_pairs(gather_buf[pl.ds(first, MOE_SUB_ROWS, stride=lane_rows), :])
                xs_scr[sub_rows(r), i * LANES:(i + 1) * LANES] = lo
                xs_scr[sub_rows(r), half + i * LANES:half + (i + 1) * LANES] = hi
            return carry

        lax.fori_loop(0, n_a, unpack, 0)

        @pl.when(t + 1 < n_tiles)
        def _():
            start_gather(t + 1)

    @pl.when(run_a)
    def _():
        wg_scr[...] = wg_ref[0, 0].astype(BF16)
        wu_scr[...] = wu_ref[0, 0].astype(BF16)

    @pl.when(run_b)
    def _():
        wd_scr[...] = wd_ref[0, 0].astype(BF16)

    def span_rows(r, span):
        return pl.ds(pl.multiple_of(r * MOE_SUB_ROWS, MOE_SUB_ROWS), span * MOE_SUB_ROWS)

    def gate_up(r, span):
        rows = span_rows(r, span)
        x = xs_scr[rows, :]
        g = jnp.dot(x, wg_scr[...], preferred_element_type=F32) + bg_ref[0, j]
        u = jnp.dot(x, wu_scr[...], preferred_element_type=F32) + bu_ref[0, j]
        g = jnp.minimum(g, SWIGLU_LIMIT)
        u = jnp.clip(u, -SWIGLU_LIMIT, SWIGLU_LIMIT)
        act = (u + 1.0) * (g * jax.nn.sigmoid(SWIGLU_ALPHA * g))
        act_scr[slot_a, j, rows, :] = act.astype(BF16)

    def down(r, span):
        rows = span_rows(r, span)
        act = jnp.concatenate([act_scr[slot_b, jb, rows, :] for jb in range(n_j)], axis=1)
        y = jnp.dot(act, wd_scr[...], preferred_element_type=F32) + bd_ref[0, j]
        first = pl.multiple_of(r * MOE_SUB_ROWS * out_rows, out_rows) + j
        o_ref[pl.ds(first, span * MOE_SUB_ROWS, stride=out_rows), :] = _pack_bf16_pairs(y)

    def for_each_sub_block(n_sub, one, group, fused):
        def run(first, count):
            if fused:
                one(first, count)
            else:
                for q in range(count):
                    one(first + q, 1)

        def trip(i, carry):
            run(group * i, group)
            return carry

        lax.fori_loop(0, n_sub // group, trip, 0)
        rest = n_sub % group
        for count in range(1, group):
            @pl.when(rest == count)
            def _():
                run(n_sub - count, count)

    for_each_sub_block(n_a, gate_up, group=3, fused=False)
    for_each_sub_block(n_b, down, group=2, fused=True)

    @pl.when((t >= 1) & (j == 0))
    def _():
        def clear(r, carry):
            block = pl.ds(pl.multiple_of(r * MOE_SUB_ROWS * out_rows, out_rows), MOE_SUB_ROWS * out_rows)
            o_ref[block, :] = jnp.zeros((MOE_SUB_ROWS * out_rows, o_ref.shape[1]), o_ref.dtype)
            return carry

        lax.fori_loop(n_b, n_sub_total, clear, 0)


def _moe_experts(h_packed, pos, p_rows, tile_expert, tile_valid, layer, w_gate, b_gate, w_up, b_up, w_down, b_down):
    _, e, d, hdim = w_gate.shape
    tm, th, tn = MOE_TILE_ROWS, MOE_HIDDEN_BLOCK, MOE_OUT_BLOCK
    n_tiles = p_rows // tm
    n_j = hdim // th
    n_out = d // tn

    assert n_j == n_out, "gate/up and down halves share the block axis"
    assert tn == 2 * LANES, "one output column block packs into one 128-lane row of bf16 pairs"

    def up_tile(t):
        return jnp.minimum(t, n_tiles - 1)

    def down_tile(t):
        return jnp.maximum(t - 1, 0)

    def up_block(t, j, tv):
        return jnp.where((t < n_tiles) & (tv[up_tile(t)] > 0), j, n_j - 1)

    def down_block(t, j, tv):
        return jnp.where((t >= 1) & (tv[down_tile(t)] > 0), j, n_out - 1)

    grid_spec = pltpu.PrefetchScalarGridSpec(
        num_scalar_prefetch=3,
        grid=(n_tiles + 1, n_j),
        in_specs=[
            pl.BlockSpec(memory_space=pl.ANY),
            pl.BlockSpec((1, 1, d, th), lambda t, j, te, tv, src: (layer, te[up_tile(t)], 0, up_block(t, j, tv))),
            pl.BlockSpec((1, 1, d, th), lambda t, j, te, tv, src: (layer, te[up_tile(t)], 0, up_block(t, j, tv))),
            pl.BlockSpec((1, 1, hdim, tn), lambda t, j, te, tv, src: (layer, te[down_tile(t)], 0, down_block(t, j, tv))),
            pl.BlockSpec((1, n_j, 1, th), lambda t, j, te, tv, src: (te[up_tile(t)], 0, 0, 0)),
            pl.BlockSpec((1, n_j, 1, th), lambda t, j, te, tv, src: (te[up_tile(t)], 0, 0, 0)),
            pl.BlockSpec((1, n_out, 1, tn), lambda t, j, te, tv, src: (te[down_tile(t)], 0, 0, 0)),
        ],
        out_specs=pl.BlockSpec((tm * n_out, LANES), lambda t, j, te, tv, src: (down_tile(t), 0)),
        scratch_shapes=[
            pltpu.SMEM((p_rows,), jnp.int32),
            pltpu.VMEM((tm * (d // 2 // LANES), LANES), jnp.uint32),
            pltpu.SemaphoreType.DMA(()),
            pltpu.VMEM((tm, d), BF16),
            pltpu.VMEM((2, n_j, tm, th), BF16),
            pltpu.VMEM((d, th), BF16),
            pltpu.VMEM((d, th), BF16),
            pltpu.VMEM((hdim, tn), BF16),
        ],
    )
    return pl.pallas_call(
        _moe_kernel,
        out_shape=jax.ShapeDtypeStruct((p_rows * n_out, LANES), jnp.uint32),
        grid_spec=grid_spec,
        compiler_params=_params(("arbitrary", "arbitrary")),
        name="moe_experts",
    )(tile_expert, tile_valid, pos, h_packed, w_gate, w_up, w_down,
      b_gate[layer].reshape(e, n_j, 1, th), b_up[layer].reshape(e, n_j, 1, th), b_down[layer].reshape(e, n_out, 1, tn))


def _combine_kernel(pos_ref, ys_hbm, gates_ref, x_ref, gate2_ref, norm_ref, o_ref, rows_buf, rows_sem, *, final_norm):
    i = pl.program_id(0)
    tm = x_ref.shape[0]
    slot = i % 2
    lane_rows = rows_buf.shape[1] // (TOP_K * tm)

    def token_rows(first, count=1):
        return pl.ds(pl.multiple_of(first * lane_rows, lane_rows), count * lane_rows)

    def start_gather(tile, slot):
        def issue(i, carry):
            for q in range(GATHER_UNROLL // TOP_K):
                r = i * (GATHER_UNROLL // TOP_K) + q
                for k in range(TOP_K):
                    row = pos_ref[(tile * tm + r) * TOP_K + k]
                    pltpu.make_async_copy(ys_hbm.at[token_rows(row), :], rows_buf.at[slot, token_rows(k * tm + r), :],
                                          rows_sem.at[slot]).start()
            return carry

        lax.fori_loop(0, tm // (GATHER_UNROLL // TOP_K), issue, 0)

    @pl.when(i == 0)
    def _():
        start_gather(0, 0)

    @pl.when(i + 1 < pl.num_programs(0))
    def _():
        start_gather(i + 1, 1 - slot)

    pltpu.make_async_copy(ys_hbm.at[pl.ds(0, TOP_K * tm * lane_rows), :], rows_buf.at[slot], rows_sem.at[slot]).wait()
    gates = gates_ref[...]
    for jr in range(lane_rows):
        lo = hi = None
        for k in range(TOP_K):
            w = rows_buf[slot, pl.ds(k * tm * lane_rows + jr, tm, stride=lane_rows), :]
            g = gates[:, k:k + 1]
            lo_k = g * lax.bitcast_convert_type(w << 16, F32)
            hi_k = g * lax.bitcast_convert_type(w & jnp.uint32(0xFFFF0000), F32)
            lo = lo_k if lo is None else lo + lo_k
            hi = hi_k if hi is None else hi + hi_k
        for part, cols in ((lo, pl.ds(2 * jr * LANES, LANES)), (hi, pl.ds((2 * jr + 1) * LANES, LANES))):
            o_ref[:, cols] = x_ref[:, cols] + gate2_ref[0, :, cols] * part
    if final_norm:
        x = o_ref[...]
        ms = jnp.mean(x * x, axis=-1, keepdims=True)
        o_ref[...] = x * lax.rsqrt(ms + EPS) * norm_ref[...]


def _combine(ys, pos, gates, x, gate2, norm_g, final_norm):
    n, d = x.shape
    tm = COMBINE_ROWS
    lane_rows = d // (2 * LANES)
    grid_spec = pltpu.PrefetchScalarGridSpec(
        num_scalar_prefetch=1,
        grid=(n // tm,),
        in_specs=[
            pl.BlockSpec(memory_space=pl.ANY),
            pl.BlockSpec((tm, TOP_K), lambda i, pos: (i, 0)),
            pl.BlockSpec((tm, d), lambda i, pos: (i, 0)),
            pl.BlockSpec((1, 1, d), lambda i, pos: (_segment_of_tile(i, tm), 0, 0)),
            pl.BlockSpec((1, d), lambda i, pos: (0, 0)),
        ],
        out_specs=pl.BlockSpec((tm, d), lambda i, pos: (i, 0)),
        scratch_shapes=[pltpu.VMEM((2, TOP_K * tm * lane_rows, LANES), jnp.uint32), pltpu.SemaphoreType.DMA((2,))],
    )
    return pl.pallas_call(
        functools.partial(_combine_kernel, final_norm=final_norm),
        out_shape=jax.ShapeDtypeStruct((n, d), F32),
        grid_spec=grid_spec,
        compiler_params=_params(("arbitrary",)),
        name="moe_combine",
    )(pos.reshape(-1), ys, gates, x, gate2, norm_g.reshape(1, d))


def _moe_layer(x, g, scale, shift, gate2, layer, w_router, b_router, w_gate, b_gate, w_up, b_up, w_down, b_down,
               norm_g, final_norm):
    n, d = x.shape
    e = w_router.shape[-1]
    tm = MOE_TILE_ROWS
    h, top_i, gates, rank, counts = _router(x, g, scale, shift, w_router[layer], b_router[layer])
    counts = counts[0].astype(jnp.int32)
    padded = ((counts + tm - 1) // tm) * tm
    ends = jnp.cumsum(padded)
    starts = ends - padded
    n_tiles = -(-(n * TOP_K) // tm) + e
    p_rows = n_tiles * tm
    tile_start = jnp.arange(n_tiles, dtype=jnp.int32) * tm
    n_used = ends[-1] // tm
    tile_expert = jnp.minimum(jnp.searchsorted(ends, tile_start, side="right"), e - 1).astype(jnp.int32)
    tile_valid = jnp.clip(counts[tile_expert] - (tile_start - starts[tile_expert]), 0, tm)
    tile_valid = jnp.where(tile_start < ends[-1], tile_valid, 0).astype(jnp.int32)
    last = jnp.maximum(n_used - 1, 0)
    tile_expert = jnp.where(tile_start < ends[-1], tile_expert, tile_expert[last]).astype(jnp.int32)
    pos = starts[top_i] + rank
    ys = _moe_experts(h, pos.reshape(-1), p_rows, tile_expert, tile_valid, layer, w_gate, b_gate, w_up, b_up, w_down, b_down)
    return _combine(ys, pos, gates, x, gate2, norm_g, final_norm)


def kernel(x_prompt, x_sample, cache_k, cache_v, state_rglru, c, c_ctx, norm_mix, norm_ffn, w_mod, b_mod,
           attn_w_in, attn_b_in, attn_w_out, attn_b_out, attn_sink,
           rnn_w_in, rnn_b_in, rnn_conv_w, rnn_conv_b, rnn_w_a, rnn_b_a, rnn_w_x, rnn_b_x,
           rnn_lambda, rnn_w_out, rnn_b_out,
           moe_w_router, moe_b_router, moe_w_gate, moe_b_gate, moe_w_up, moe_b_up,
           moe_w_down, moe_b_down, final_norm):
    d = D_MODEL
    n_ctx = BATCH * SEQ
    attn_w = N_HEADS * HEAD_DIM
    kv_w = N_KV_HEADS * HEAD_DIM
    x = jnp.concatenate([x_prompt.reshape(n_ctx, d), x_sample.reshape(DEC_BATCH * DEC_SEQ, d)], axis=0)
    cond = jnp.concatenate([c_ctx[None, :], c, jnp.zeros((SUBLANES - 1 - DEC_BATCH, d), F32)], axis=0)
    mods = _modulation(cond, w_mod, b_mod)

    def mod(l, k):
        return mods[l, :, k * d:(k + 1) * d].reshape(SUBLANES, 1, d)

    new_k, new_v, new_s = [], [], []
    for l in range(DEPTH):
        j = l // 2
        sh1, sc1, g1, sh2, sc2, g2 = [mod(l, k) for k in range(N_MOD)]
        if l % 2 == 0:
            p = _proj_in(x, norm_mix[l], sc1, sh1, attn_w_in[j], attn_b_in[j], tn=PROJ_IN_ATTN_COLS)
            new_k.append(p[:n_ctx, attn_w:attn_w + kv_w].reshape(BATCH, SEQ, N_KV_HEADS, HEAD_DIM))
            new_v.append(p[:n_ctx, attn_w + kv_w:attn_w + 2 * kv_w].reshape(BATCH, SEQ, N_KV_HEADS, HEAD_DIM))
            mix_ctx = _ctx_mixer(p, attn_sink[j])
            mix_lat = _lat_mixer(p, cache_k[:, j].reshape(DEC_BATCH, PAST_LEN, kv_w),
                                 cache_v[:, j].reshape(DEC_BATCH, PAST_LEN, kv_w), attn_sink[j])
            x = _proj_out(mix_ctx, mix_lat, attn_w_out[j], attn_b_out[j], g1, x)
        else:
            p = _proj_in(x, norm_mix[l], sc1, sh1, rnn_w_in[j], rnn_b_in[j], tn=PROJ_IN_RNN_COLS)
            args = (rnn_conv_w[j], rnn_conv_b[j], rnn_w_a[j], rnn_b_a[j], rnn_w_x[j], rnn_b_x[j], rnn_lambda[j])
            h0_ctx = jnp.zeros((BATCH, 2, d), F32)
            mix_ctx, st = _rnn_core(p, 0, n_ctx, h0_ctx, *args, n_seq=SUBLANES, seq_len=SEQ)
            mix_lat, _ = _rnn_core(p, n_ctx, DEC_BATCH * DEC_SEQ, state_rglru[:, j], *args,
                                   n_seq=DEC_BATCH, seq_len=DEC_SEQ)
            new_s.append(st)
            x = _proj_out(mix_ctx, mix_lat, rnn_w_out[j], rnn_b_out[j], g1, x)
        x = _moe_layer(x, norm_ffn[l], sc2, sh2, g2, l, moe_w_router, moe_b_router,
                       moe_w_gate, moe_b_gate, moe_w_up, moe_b_up, moe_w_down, moe_b_down,
                       final_norm, final_norm=(l == DEPTH - 1))
    y = x
    y_prompt = y[:n_ctx].reshape(BATCH, SEQ, d)
    y_sample = y[n_ctx:].reshape(DEC_BATCH, DEC_SEQ, d)
    return (y_prompt, y_sample, jnp.stack(new_k, axis=1), jnp.stack(new_v, axis=1), jnp.stack(new_s, axis=1))
```

```python
import functools
import math

import numpy as np
import jax
import jax.numpy as jnp
from jax import lax
from jax.experimental import pallas as pl
from jax.experimental.pallas import tpu as pltpu

D_MODEL = 2048
BATCH = 32
SEQ = 256
DEPTH = 2
DEC_BATCH = 2
DEC_SEQ = 1024
PAST_LEN = 256
GRID_W = 64
N_HEADS = 16
N_KV_HEADS = 2
HEAD_DIM = 64
WINDOW = 128
ROPE_THETA = 10000.0
N_FOURIER_GROUPS = 4
N_RNN_BLOCKS = 8
CONV_WIDTH = 4
CONV_LEFT = 2
RG_C = 8.0
N_EXPERTS = 32
TOP_K = 4
SWIGLU_LIMIT = 7.0
SWIGLU_ALPHA = 1.702
N_MOD = 6
EPS = 1e-6

LANES = 128
SUBLANES = 8
VMEM_LIMIT_BYTES = 56 * 1024 * 1024

MOE_TILE_ROWS = 1536
MOE_SUB_ROWS = 256
MOE_HIDDEN_BLOCK = 256
MOE_OUT_BLOCK = 256
GATHER_UNROLL = 8
SCAN_ROW_PAD = 8
SCAN_UNROLL = 4
ROW_TILE = 512
PROJ_ROWS = 1024
COMBINE_ROWS = 128
PROJ_OUT_COLS = 512
PROJ_IN_ATTN_COLS = 768
PROJ_IN_RNN_COLS = 1024

F32 = jnp.float32
BF16 = jnp.bfloat16


def _params(semantics):
    return pltpu.CompilerParams(dimension_semantics=semantics, vmem_limit_bytes=VMEM_LIMIT_BYTES)


def _segment_of_tile(i, tile_rows):
    n_ctx_tiles = (BATCH * SEQ) // tile_rows
    tiles_per_latent = DEC_SEQ // tile_rows
    return jnp.where(i < n_ctx_tiles, 0, 1 + (i - n_ctx_tiles) // tiles_per_latent)


def _rms_modulate(x, g, scale, shift):
    ms = jnp.mean(x * x, axis=-1, keepdims=True)
    return (x * lax.rsqrt(ms + EPS) * g) * (1.0 + scale) + shift


def _modulation_kernel(c_ref, w_ref, b_ref, o_ref):
    c = c_ref[...]
    s = (c * jax.nn.sigmoid(c)).astype(BF16)
    o_ref[0] = jnp.dot(s, w_ref[0].astype(BF16), preferred_element_type=F32) + b_ref[0]


def _modulation(cond, w_mod, b_mod):
    d = D_MODEL
    tn = 512
    n_out = N_MOD * d
    return pl.pallas_call(
        _modulation_kernel,
        out_shape=jax.ShapeDtypeStruct((DEPTH, SUBLANES, n_out), F32),
        grid=(DEPTH, n_out // tn),
        in_specs=[
            pl.BlockSpec((SUBLANES, d), lambda l, j: (0, 0)),
            pl.BlockSpec((1, d, tn), lambda l, j: (l, 0, j)),
            pl.BlockSpec((1, 1, tn), lambda l, j: (l, 0, j)),
        ],
        out_specs=pl.BlockSpec((1, SUBLANES, tn), lambda l, j: (l, 0, j)),
        compiler_params=_params(("arbitrary", "arbitrary")),
        name="modulation",
    )(cond, w_mod, b_mod.reshape(DEPTH, 1, n_out))


def _row_part_specs(parts, tm, block_cols, col_of):
    specs, tiles, first = [], [], 0
    for part in parts:
        n_tiles = part.shape[0] // tm
        specs.append(pl.BlockSpec((tm, block_cols),
                                  lambda i, j, first=first, n_tiles=n_tiles: (jnp.clip(i - first, 0, n_tiles - 1), col_of(j))))
        tiles.append(n_tiles)
        first += n_tiles
    return specs, tuple(tiles)


def _for_row_part(i, part_tiles, fn):
    first = 0
    for p, n_tiles in enumerate(part_tiles):
        @pl.when((i >= first) & (i < first + n_tiles))
        def _():
            fn(p)

        first += n_tiles


def _proj_in_kernel(*refs, part_tiles):
    x_refs = refs[:len(part_tiles)]
    g_ref, sc_ref, sh_ref, w_ref, b_ref, o_ref, h_ref = refs[len(part_tiles):]

    @pl.when(pl.program_id(1) == 0)
    def _():
        def prologue(p):
            h_ref[...] = _rms_modulate(x_refs[p][...], g_ref[...], sc_ref[0], sh_ref[0]).astype(BF16)

        _for_row_part(pl.program_id(0), part_tiles, prologue)

    o_ref[...] = jnp.dot(h_ref[...], w_ref[...].astype(BF16), preferred_element_type=F32) + b_ref[...]


def _proj_in(x_parts, g, scale, shift, w, b, tn):
    n = sum(part.shape[0] for part in x_parts)
    d = x_parts[0].shape[1]
    n_out = w.shape[1]
    tm = PROJ_ROWS
    seg = lambda i, j: (_segment_of_tile(i, tm), 0, 0)
    x_specs, part_tiles = _row_part_specs(x_parts, tm, d, lambda j: 0)
    return pl.pallas_call(
        functools.partial(_proj_in_kernel, part_tiles=part_tiles),
        out_shape=jax.ShapeDtypeStruct((n, n_out), F32),
        grid=(n // tm, n_out // tn),
        in_specs=x_specs + [
            pl.BlockSpec((1, d), lambda i, j: (0, 0)),
            pl.BlockSpec((1, 1, d), seg),
            pl.BlockSpec((1, 1, d), seg),
            pl.BlockSpec((d, tn), lambda i, j: (0, j)),
            pl.BlockSpec((1, tn), lambda i, j: (0, j)),
        ],
        out_specs=pl.BlockSpec((tm, tn), lambda i, j: (i, j)),
        scratch_shapes=[pltpu.VMEM((tm, d), BF16)],
        compiler_params=_params(("arbitrary", "arbitrary")),
        name="proj_in",
    )(*x_parts, g.reshape(1, d), scale, shift, w, b.reshape(1, n_out))


def _proj_out_kernel(*refs, a_tiles, res_tiles):
    a_refs = refs[:len(a_tiles)]
    res_refs = refs[len(a_tiles):len(a_tiles) + len(res_tiles)]
    w_ref, b_ref, gate_ref, o_ref = refs[len(a_tiles) + len(res_tiles):]

    def project(p):
        res_ref = res_refs[p] if len(res_refs) > 1 else res_refs[0]
        y = jnp.dot(a_refs[p][...], w_ref[...].astype(BF16), preferred_element_type=F32) + b_ref[...]
        o_ref[...] = res_ref[...] + gate_ref[0] * y

    _for_row_part(pl.program_id(0), a_tiles, project)


def _proj_out(a_parts, w, b, gate, res_parts):
    n = sum(part.shape[0] for part in a_parts)
    k, d = w.shape
    tm, tn = PROJ_ROWS, PROJ_OUT_COLS
    a_specs, a_tiles = _row_part_specs(a_parts, tm, k, lambda j: 0)
    res_specs, res_tiles = _row_part_specs(res_parts, tm, tn, lambda j: j)
    assert len(res_tiles) == 1 or res_tiles == a_tiles
    return pl.pallas_call(
        functools.partial(_proj_out_kernel, a_tiles=a_tiles, res_tiles=res_tiles),
        out_shape=jax.ShapeDtypeStruct((n, d), F32),
        grid=(n // tm, d // tn),
        in_specs=a_specs + res_specs + [
            pl.BlockSpec((k, tn), lambda i, j: (0, j)),
            pl.BlockSpec((1, tn), lambda i, j: (0, j)),
            pl.BlockSpec((1, 1, tn), lambda i, j: (_segment_of_tile(i, tm), 0, j)),
        ],
        out_specs=pl.BlockSpec((tm, tn), lambda i, j: (i, j)),
        compiler_params=_params(("arbitrary", "arbitrary")),
        name="proj_out",
    )(*a_parts, *res_parts, w, b.reshape(1, d), gate)


def _dot_nt(a, b):
    return lax.dot_general(a, b, (((1,), (1,)), ((), ())), preferred_element_type=F32)


def _head_pair_operands(k, v, group):
    lane = lax.broadcasted_iota(jnp.int32, k.shape, 1)
    low = lane < HEAD_DIM
    k_sw = pltpu.roll(k, HEAD_DIM, 1)
    v_sw = pltpu.roll(v, HEAD_DIM, 1)
    if group == 0:
        kd = jnp.where(low, k, k_sw)
        vd = jnp.where(low, v, v_sw)
    else:
        kd = jnp.where(low, k_sw, k)
        vd = jnp.where(low, v_sw, v)
    v_lo = jnp.where(low, vd, 0.0).astype(BF16)
    v_hi = jnp.where(low, 0.0, vd).astype(BF16)
    return kd.astype(BF16), v_lo, v_hi


def _split_pair(q2):
    lane = lax.broadcasted_iota(jnp.int32, q2.shape, 1)
    low = lane < HEAD_DIM
    qs = q2 * (HEAD_DIM ** -0.5)
    return jnp.where(low, qs, 0.0).astype(BF16), jnp.where(low, 0.0, qs).astype(BF16)


def _softmax_pv(scores, values, sink):
    m = jnp.full((scores[0].shape[0], 1), sink, F32)
    for s in scores:
        m = jnp.maximum(m, jnp.max(s, axis=-1, keepdims=True))
    den = jnp.exp(sink - m)
    ps = []
    for s in scores:
        p = jnp.exp(s - m)
        den = den + jnp.sum(p, axis=-1, keepdims=True)
        ps.append(p)
    out = None
    for p, v in zip(ps, values):
        o = jnp.dot((p / den).astype(BF16), v, preferred_element_type=F32)
        out = o if out is None else out + o
    return out


def _dft_matrices(t):
    idx = np.arange(t)
    ang = 2.0 * np.pi * ((idx[:, None] * idx[None, :]) % t) / t
    m = np.concatenate([np.cos(ang), np.sin(ang)], axis=0) / math.sqrt(t)
    return jnp.asarray(m, dtype=BF16)


def _dft_channel_matrix(c):
    idx = np.arange(c)
    ang = 2.0 * np.pi * ((idx[:, None] * idx[None, :]) % c) / c
    m = np.concatenate([np.cos(ang), -np.sin(ang)], axis=0) / math.sqrt(c)
    return jnp.asarray(m, dtype=BF16)


def _fourier_group(f_g, ts_ref, cs_ref):
    t = f_g.shape[0]
    ab = jnp.dot(ts_ref[...], f_g.astype(BF16), preferred_element_type=F32)
    lhs = jnp.concatenate([ab[:t], ab[t:]], axis=1).astype(BF16)
    return jnp.dot(lhs, cs_ref[...], preferred_element_type=F32)


def _ctx_mixer_kernel(sink_ref, p_ref, ts_ref, cs_ref, o_ref):
    attn_w = N_HEADS * HEAD_DIM
    kv_w = N_KV_HEADS * HEAD_DIM
    pair_w = 2 * HEAD_DIM
    group_heads = N_HEADS // N_KV_HEADS
    k = p_ref[:, attn_w:attn_w + kv_w]
    v = p_ref[:, attn_w + kv_w:attn_w + 2 * kv_w]
    for g in range(N_KV_HEADS):
        kd, v_lo, v_hi = _head_pair_operands(k, v, g)
        for i in range(group_heads // 2):
            pair = g * (group_heads // 2) + i
            q_lo, q_hi = _split_pair(p_ref[:, pair * pair_w:(pair + 1) * pair_w])
            o = _softmax_pv([_dot_nt(q_lo, kd)], [v_lo], sink_ref[2 * pair])
            o = o + _softmax_pv([_dot_nt(q_hi, kd)], [v_hi], sink_ref[2 * pair + 1])
            o_ref[:, pair * pair_w:(pair + 1) * pair_w] = o.astype(o_ref.dtype)
    f0 = attn_w + 2 * kv_w
    fg = (D_MODEL - attn_w) // N_FOURIER_GROUPS
    for g in range(N_FOURIER_GROUPS):
        z = _fourier_group(p_ref[:, f0 + g * fg:f0 + (g + 1) * fg], ts_ref, cs_ref)
        o_ref[:, attn_w + g * fg:attn_w + (g + 1) * fg] = z.astype(o_ref.dtype)


def _ctx_mixer(p, sink):
    n, width = BATCH * SEQ, p.shape[1]
    fg = (D_MODEL - N_HEADS * HEAD_DIM) // N_FOURIER_GROUPS
    return pl.pallas_call(
        _ctx_mixer_kernel,
        out_shape=jax.ShapeDtypeStruct((n, D_MODEL), BF16),
        grid=(n // SEQ,),
        in_specs=[
            pl.BlockSpec(memory_space=pltpu.SMEM),
            pl.BlockSpec((SEQ, width), lambda b: (b, 0)),
            pl.BlockSpec((2 * SEQ, SEQ), lambda b: (0, 0)),
            pl.BlockSpec((2 * fg, fg), lambda b: (0, 0)),
        ],
        out_specs=pl.BlockSpec((SEQ, D_MODEL), lambda b: (b, 0)),
        compiler_params=_params(("arbitrary",)),
        name="ctx_mixer",
    )(sink, p, _dft_matrices(SEQ), _dft_channel_matrix(fg))


def _rope_tables():
    rows = DEC_SEQ // GRID_W
    row = np.repeat(np.arange(rows, dtype=np.float32), GRID_W)
    col = np.tile(np.arange(GRID_W, dtype=np.float32), rows)
    n_freq = HEAD_DIM // 4
    inv = jnp.asarray(ROPE_THETA, F32) ** (-jnp.arange(n_freq, dtype=F32) / n_freq)
    ang = jnp.concatenate([row[:, None] * inv, col[:, None] * inv], axis=-1)
    cos = jnp.repeat(jnp.cos(ang), 2, axis=-1)
    sin = jnp.repeat(jnp.sin(ang), 2, axis=-1)
    sign = jnp.tile(jnp.asarray([-1.0, 1.0], F32), HEAD_DIM // 2)
    return jnp.tile(cos, (1, 2)), jnp.tile(sin * sign, (1, 2))


def _rope(x, cos, sin_signed):
    lane = lax.broadcasted_iota(jnp.int32, x.shape, 1)
    width = x.shape[1]
    partner = jnp.where(lane % 2 == 0, pltpu.roll(x, width - 1, 1), pltpu.roll(x, 1, 1))
    return x * cos + partner * sin_signed


def _lat_mixer_kernel(sink_ref, p_ref, ck_ref, cv_ref, cos_ref, sin_ref, ts_ref, cs_ref, o_ref, q_scr, k_scr):
    attn_w = N_HEADS * HEAD_DIM
    kv_w = N_KV_HEADS * HEAD_DIM
    pair_w = 2 * HEAD_DIM
    group_heads = N_HEADS // N_KV_HEADS
    q_rows = 256
    cos = cos_ref[...]
    sin = sin_ref[...]
    for pair in range(N_HEADS // 2):
        q_scr[:, pair * pair_w:(pair + 1) * pair_w] = _rope(p_ref[:, pair * pair_w:(pair + 1) * pair_w], cos, sin)
    k_scr[...] = _rope(p_ref[:, attn_w:attn_w + kv_w], cos, sin)
    v = p_ref[:, attn_w + kv_w:attn_w + 2 * kv_w]
    ck = ck_ref[0]
    cv = cv_ref[0]
    k = k_scr[...]
    ops = []
    for g in range(N_KV_HEADS):
        ops.append(_head_pair_operands(k, v, g) + _head_pair_operands(ck, cv, g))

    def chunk(c, carry):
        r0 = pl.multiple_of(c * q_rows, q_rows)
        qi = r0 + lax.broadcasted_iota(jnp.int32, (q_rows, DEC_SEQ), 0)
        kj = lax.broadcasted_iota(jnp.int32, (q_rows, DEC_SEQ), 1)
        valid = jnp.abs(qi - kj) <= WINDOW
        for g in range(N_KV_HEADS):
            kd, v_lo, v_hi, ckd, cv_lo, cv_hi = ops[g]
            for i in range(group_heads // 2):
                pair = g * (group_heads // 2) + i
                q_lo, q_hi = _split_pair(q_scr[pl.ds(r0, q_rows), pair * pair_w:(pair + 1) * pair_w])
                s_lo = jnp.where(valid, _dot_nt(q_lo, kd), -jnp.inf)
                o = _softmax_pv([_dot_nt(q_lo, ckd), s_lo], [cv_lo, v_lo], sink_ref[2 * pair])
                s_hi = jnp.where(valid, _dot_nt(q_hi, kd), -jnp.inf)
                o = o + _softmax_pv([_dot_nt(q_hi, ckd), s_hi], [cv_hi, v_hi], sink_ref[2 * pair + 1])
                o_ref[pl.ds(r0, q_rows), pair * pair_w:(pair + 1) * pair_w] = o.astype(o_ref.dtype)
        return carry

    lax.fori_loop(0, DEC_SEQ // q_rows, chunk, 0)
    f0 = attn_w + 2 * kv_w
    fg = (D_MODEL - attn_w) // N_FOURIER_GROUPS
    for g in range(N_FOURIER_GROUPS):
        z = _fourier_group(p_ref[:, f0 + g * fg:f0 + (g + 1) * fg], ts_ref, cs_ref)
        o_ref[:, attn_w + g * fg:attn_w + (g + 1) * fg] = z.astype(o_ref.dtype)


def _lat_mixer(p, cache_k, cache_v, sink):
    n, width = DEC_BATCH * DEC_SEQ, p.shape[1]
    first = (BATCH * SEQ) // DEC_SEQ
    kv_w = N_KV_HEADS * HEAD_DIM
    attn_w = N_HEADS * HEAD_DIM
    fg = (D_MODEL - attn_w) // N_FOURIER_GROUPS
    cos, sin = _rope_tables()
    return pl.pallas_call(
        _lat_mixer_kernel,
        out_shape=jax.ShapeDtypeStruct((n, D_MODEL), BF16),
        grid=(n // DEC_SEQ,),
        in_specs=[
            pl.BlockSpec(memory_space=pltpu.SMEM),
            pl.BlockSpec((DEC_SEQ, width), lambda b: (first + b, 0)),
            pl.BlockSpec((1, PAST_LEN, kv_w), lambda b: (b, 0, 0)),
            pl.BlockSpec((1, PAST_LEN, kv_w), lambda b: (b, 0, 0)),
            pl.BlockSpec((DEC_SEQ, 2 * HEAD_DIM), lambda b: (0, 0)),
            pl.BlockSpec((DEC_SEQ, 2 * HEAD_DIM), lambda b: (0, 0)),
            pl.BlockSpec((2 * DEC_SEQ, DEC_SEQ), lambda b: (0, 0)),
            pl.BlockSpec((2 * fg, fg), lambda b: (0, 0)),
        ],
        out_specs=pl.BlockSpec((DEC_SEQ, D_MODEL), lambda b: (b, 0)),
        scratch_shapes=[pltpu.VMEM((DEC_SEQ, attn_w), F32), pltpu.VMEM((DEC_SEQ, kv_w), F32)],
        compiler_params=_params(("arbitrary",)),
        name="lat_mixer",
    )(sink, p, cache_k, cache_v, cos, sin, _dft_matrices(DEC_SEQ), _dft_channel_matrix(fg))


def _rnn_core_kernel(xr_ref, gr_ref, cw_ref, cb_ref, wa_ref, ba_ref, wx_ref, bx_ref, lam_ref, h0_ref,
                     y_ref, st_ref, a_scr, b_scr, h_scr, *, n_seq, seq_len):
    rows = n_seq * seq_len
    n_lb = xr_ref.shape[1] // LANES
    xr = xr_ref[...]
    t_idx = lax.broadcasted_iota(jnp.int32, (rows, 1), 0) % seq_len
    xc = jnp.broadcast_to(cb_ref[...], xr.shape)
    for tap in range(CONV_WIDTH):
        off = tap - CONV_LEFT
        shifted = xr if off == 0 else pltpu.roll(xr, (-off) % rows, 0)
        valid = (t_idx + off >= 0) & (t_idx + off < seq_len)
        xc = xc + jnp.where(valid, shifted, 0.0) * cw_ref[tap:tap + 1, :]
    xcb = xc.astype(BF16)
    pitch = seq_len + SCAN_ROW_PAD

    def sigmoid(z):
        return 0.5 * jnp.tanh(0.5 * z) + 0.5

    def seq_rows(s):
        return pl.ds(s * pitch, seq_len)

    for d in range(2):
        r = sigmoid(jnp.dot(xcb, wa_ref[d, 0].astype(BF16), preferred_element_type=F32) + ba_ref[d:d + 1, :])
        gi = sigmoid(jnp.dot(xcb, wx_ref[d, 0].astype(BF16), preferred_element_type=F32) + bx_ref[d:d + 1, :])
        neg_lam = -lam_ref[d:d + 1, :]
        softplus = jnp.maximum(neg_lam, 0.0) + jnp.log1p(jnp.exp(-jnp.abs(neg_lam)))
        a = jnp.exp(-RG_C * r * softplus)
        b = jnp.sqrt(1.0 - a * a) * (gi * xc)
        for lb in range(n_lb):
            for s in range(n_seq):
                a_scr[d * n_lb + lb, seq_rows(s), :] = a[s * seq_len:(s + 1) * seq_len, lb * LANES:(lb + 1) * LANES]
                b_scr[d * n_lb + lb, seq_rows(s), :] = b[s * seq_len:(s + 1) * seq_len, lb * LANES:(lb + 1) * LANES]

    def time_rows(t):
        return pl.ds(t, n_seq, stride=pitch)

    def lane_block(ref, k, lb):
        return ref[:, k, lb * LANES:(lb + 1) * LANES]

    def fwd(t, hs):
        out = []
        for lb in range(n_lb):
            h = a_scr[lb, time_rows(t), :] * hs[lb] + b_scr[lb, time_rows(t), :]
            h_scr[lb, time_rows(t), :] = h
            out.append(h)
        return tuple(out)

    hs = lax.fori_loop(0, seq_len, fwd, tuple(lane_block(h0_ref, 0, lb) for lb in range(n_lb)), unroll=SCAN_UNROLL)
    for lb in range(n_lb):
        st_ref[:, 0, lb * LANES:(lb + 1) * LANES] = hs[lb]

    def bwd(i, hs):
        t = seq_len - 1 - i
        out = []
        for lb in range(n_lb):
            h = a_scr[n_lb + lb, time_rows(t), :] * hs[lb] + b_scr[n_lb + lb, time_rows(t), :]
            h_scr[lb, time_rows(t), :] = h_scr[lb, time_rows(t), :] + h
            out.append(h)
        return tuple(out)

    hs = lax.fori_loop(0, seq_len, bwd, tuple(lane_block(h0_ref, 1, lb) for lb in range(n_lb)), unroll=SCAN_UNROLL)
    for lb in range(n_lb):
        st_ref[:, 1, lb * LANES:(lb + 1) * LANES] = hs[lb]
    for lb in range(n_lb):
        for s in range(n_seq):
            gr = gr_ref[s * seq_len:(s + 1) * seq_len, lb * LANES:(lb + 1) * LANES]
            gelu = 0.5 * gr * (1.0 + jnp.tanh(math.sqrt(2.0 / math.pi) * (gr + 0.044715 * (gr * gr * gr))))
            y_ref[s * seq_len:(s + 1) * seq_len, lb * LANES:(lb + 1) * LANES] = (
                gelu * h_scr[lb, seq_rows(s), :]).astype(y_ref.dtype)


def _rnn_core(p, first_row, n, h0, conv_w, conv_b, w_a, b_a, w_x, b_x, lam, n_seq, seq_len):
    d_rnn = p.shape[1] // 2
    cb = d_rnn // N_RNN_BLOCKS
    rows = n_seq * seq_len
    n_batch = n // seq_len
    first = first_row // rows
    scan_rows = n_seq * (seq_len + SCAN_ROW_PAD)
    kern = functools.partial(_rnn_core_kernel, n_seq=n_seq, seq_len=seq_len)
    return pl.pallas_call(
        kern,
        out_shape=(jax.ShapeDtypeStruct((n, d_rnn), BF16), jax.ShapeDtypeStruct((n_batch, 2, d_rnn), F32)),
        grid=(n // rows, N_RNN_BLOCKS),
        in_specs=[
            pl.BlockSpec((rows, cb), lambda i, c: (first + i, c)),
            pl.BlockSpec((rows, cb), lambda i, c: (first + i, N_RNN_BLOCKS + c)),
            pl.BlockSpec((CONV_WIDTH, cb), lambda i, c: (0, c)),
            pl.BlockSpec((1, cb), lambda i, c: (0, c)),
            pl.BlockSpec((2, 1, cb, cb), lambda i, c: (0, c, 0, 0)),
            pl.BlockSpec((2, cb), lambda i, c: (0, c)),
            pl.BlockSpec((2, 1, cb, cb), lambda i, c: (0, c, 0, 0)),
            pl.BlockSpec((2, cb), lambda i, c: (0, c)),
            pl.BlockSpec((2, cb), lambda i, c: (0, c)),
            pl.BlockSpec((n_seq, 2, cb), lambda i, c: (i, 0, c)),
        ],
        out_specs=(
            pl.BlockSpec((rows, cb), lambda i, c: (i, c)),
            pl.BlockSpec((n_seq, 2, cb), lambda i, c: (i, 0, c)),
        ),
        scratch_shapes=[pltpu.VMEM((2 * cb // LANES, scan_rows, LANES), F32),
                        pltpu.VMEM((2 * cb // LANES, scan_rows, LANES), F32),
                        pltpu.VMEM((cb // LANES, scan_rows, LANES), F32)],
        compiler_params=_params(("arbitrary", "arbitrary")),
        name="rnn_core",
    )(p, p, conv_w, conv_b.reshape(1, d_rnn), w_a, b_a, w_x, b_x, lam, h0)


def _pack_bf16_pairs(h):
    half = h.shape[1] // 2
    bits = lax.bitcast_convert_type(h.astype(BF16).astype(F32), jnp.uint32)
    return (bits[:, :half] >> 16) | (bits[:, half:] & jnp.uint32(0xFFFF0000))


def _unpack_bf16_pairs(w):
    lo = lax.bitcast_convert_type(w << 16, F32)
    hi = lax.bitcast_convert_type(w & jnp.uint32(0xFFFF0000), F32)
    return lo.astype(BF16), hi.astype(BF16)


def _router_kernel(x_ref, g_ref, sc_ref, sh_ref, w_ref, b_ref, h_ref, idx_ref, gate_ref, rank_ref, cnt_ref, run_ref):
    tm = x_ref.shape[0]

    @pl.when(pl.program_id(0) == 0)
    def _():
        run_ref[...] = jnp.zeros_like(run_ref)

    h = _rms_modulate(x_ref[...], g_ref[...], sc_ref[0], sh_ref[0])
    packed = _pack_bf16_pairs(h)
    lane_rows = packed.shape[1] // LANES
    for i in range(lane_rows):
        h_ref[pl.ds(i, tm, stride=lane_rows), :] = packed[:, i * LANES:(i + 1) * LANES]
    w = w_ref[...]
    h_hi, w_hi = h.astype(BF16), w.astype(BF16)
    h_lo = (h - h_hi.astype(F32)).astype(BF16)
    w_lo = (w - w_hi.astype(F32)).astype(BF16)
    logits = (jnp.dot(h_hi, w_hi, preferred_element_type=F32)
              + (jnp.dot(h_lo, w_hi, preferred_element_type=F32) + jnp.dot(h_hi, w_lo, preferred_element_type=F32))
              + b_ref[...])
    lane = lax.broadcasted_iota(jnp.int32, logits.shape, 1).astype(F32)
    col = lax.broadcasted_iota(jnp.int32, (tm, TOP_K), 1)
    chosen = jnp.zeros(logits.shape, F32)
    top_v, top_i, hits = [], [], []
    work = logits
    for _ in range(TOP_K):
        m = jnp.max(work, axis=-1, keepdims=True)
        first = jnp.min(jnp.where(work == m, lane, float(N_EXPERTS)), axis=-1, keepdims=True)
        hit = lane == first
        work = jnp.where(hit, -jnp.inf, work)
        chosen = jnp.where(hit, 1.0, chosen)
        top_v.append(m)
        top_i.append(first)
        hits.append(hit)
    exps = [jnp.exp(v - top_v[0]) for v in top_v]
    den = exps[0]
    for e in exps[1:]:
        den = den + e
    ri = lax.broadcasted_iota(jnp.int32, (tm, tm), 0)
    ci = lax.broadcasted_iota(jnp.int32, (tm, tm), 1)
    before = (ci < ri).astype(BF16)
    rank_all = jnp.dot(before, chosen.astype(BF16), preferred_element_type=F32) + run_ref[...]
    idx_out = jnp.zeros((tm, TOP_K), F32)
    gate_out = jnp.zeros((tm, TOP_K), F32)
    rank_out = jnp.zeros((tm, TOP_K), F32)
    for k in range(TOP_K):
        rk = jnp.sum(jnp.where(hits[k], rank_all, 0.0), axis=-1, keepdims=True)
        idx_out = jnp.where(col == k, top_i[k], idx_out)
        gate_out = jnp.where(col == k, exps[k] / den, gate_out)
        rank_out = jnp.where(col == k, rk, rank_out)
    idx_ref[...] = idx_out.astype(jnp.int32)
    gate_ref[...] = gate_out
    rank_ref[...] = rank_out.astype(jnp.int32)
    run_ref[...] = run_ref[...] + jnp.sum(chosen, axis=0, keepdims=True)
    cnt_ref[...] = run_ref[...]


def _router(x, g, scale, shift, w, b):
    n, d = x.shape
    e = w.shape[1]
    tm = ROW_TILE
    seg = lambda i: (_segment_of_tile(i, tm), 0, 0)
    small = pl.BlockSpec((tm, TOP_K), lambda i: (i, 0))
    return pl.pallas_call(
        _router_kernel,
        out_shape=(
            jax.ShapeDtypeStruct((n * (d // 2 // LANES), LANES), jnp.uint32),
            jax.ShapeDtypeStruct((n, TOP_K), jnp.int32),
            jax.ShapeDtypeStruct((n, TOP_K), F32),
            jax.ShapeDtypeStruct((n, TOP_K), jnp.int32),
            jax.ShapeDtypeStruct((1, e), F32),
        ),
        grid=(n // tm,),
        in_specs=[
            pl.BlockSpec((tm, d), lambda i: (i, 0)),
            pl.BlockSpec((1, d), lambda i: (0, 0)),
            pl.BlockSpec((1, 1, d), seg),
            pl.BlockSpec((1, 1, d), seg),
            pl.BlockSpec((d, e), lambda i: (0, 0)),
            pl.BlockSpec((1, e), lambda i: (0, 0)),
        ],
        out_specs=(pl.BlockSpec((tm * (d // 2 // LANES), LANES), lambda i: (i, 0)), small, small, small,
                   pl.BlockSpec((1, e), lambda i: (0, 0))),
        scratch_shapes=[pltpu.VMEM((1, e), F32)],
        compiler_params=_params(("arbitrary",)),
        name="router",
    )(x, g.reshape(1, d), scale, shift, w, b.reshape(1, e))


def _moe_kernel(te_ref, tv_ref, pos_ref, h_hbm, wg_ref, wu_ref, wd_ref, bg_ref, bu_ref, bd_ref, o_ref,
                src_ref, gather_buf, gather_sem, xs_scr, act_scr, wg_scr, wu_scr, wd_scr):
    t = pl.program_id(0)
    j = pl.program_id(1)
    n_tiles = pl.num_programs(0) - 1
    n_j = act_scr.shape[1]
    tile_rows = xs_scr.shape[0]
    n_sub_total = tile_rows // MOE_SUB_ROWS
    out_rows = o_ref.shape[0] // tile_rows
    ta = jnp.minimum(t, n_tiles - 1)
    tb = jnp.maximum(t - 1, 0)

    def sub_blocks(tile):
        return (tv_ref[tile] + MOE_SUB_ROWS - 1) // MOE_SUB_ROWS

    def sub_rows(r):
        return pl.ds(pl.multiple_of(r * MOE_SUB_ROWS, MOE_SUB_ROWS), MOE_SUB_ROWS)

    lane_rows = xs_scr.shape[1] // 2 // LANES
    half = xs_scr.shape[1] // 2

    def token_rows(first_token, count=1):
        return pl.ds(pl.multiple_of(first_token * lane_rows, lane_rows), count * lane_rows)

    def start_gather(tile):
        def issue(i, carry):
            for q in range(GATHER_UNROLL):
                r = i * GATHER_UNROLL + q
                token = src_ref[tile * tile_rows + r]
                pltpu.make_async_copy(h_hbm.at[token_rows(token), :], gather_buf.at[token_rows(r), :], gather_sem).start()
            return carry

        lax.fori_loop(0, sub_blocks(tile) * (MOE_SUB_ROWS // GATHER_UNROLL), issue, 0)

    def wait_gather(tile):
        def wait_sub(r, carry):
            block = token_rows(r * MOE_SUB_ROWS, MOE_SUB_ROWS)
            pltpu.make_async_copy(h_hbm.at[block, :], gather_buf.at[block, :], gather_sem).wait()
            return carry

        lax.fori_loop(0, sub_blocks(tile), wait_sub, 0)

    @pl.when((t == 0) & (j == 0))
    def _():
        tokens_per_trip = GATHER_UNROLL // TOP_K

        def put(i, carry):
            for q in range(tokens_per_trip):
                token = i * tokens_per_trip + q
                for k in range(TOP_K):
                    src_ref[pos_ref[token * TOP_K + k]] = token
            return carry

        lax.fori_loop(0, pos_ref.shape[0] // GATHER_UNROLL, put, 0)

        def pad_tile(tile, carry):
            def pad(r, inner):
                src_ref[tile * tile_rows + r] = 0
                return inner

            lax.fori_loop(tv_ref[tile], sub_blocks(tile) * MOE_SUB_ROWS, pad, 0)
            return carry

        lax.fori_loop(0, n_tiles, pad_tile, 0)

    run_a = (t < n_tiles) & (tv_ref[ta] > 0)
    run_b = (t >= 1) & (tv_ref[tb] > 0)
    n_a = jnp.where(run_a, sub_blocks(ta), 0)
    n_b = jnp.where(run_b, sub_blocks(tb), 0)
    slot_a = ta % 2
    slot_b = tb % 2

    @pl.when((j == 0) & run_a)
    def _():
        @pl.when(t == 0)
        def _():
            start_gather(0)

        wait_gather(t)

        def unpack(r, carry):
            for i in range(lane_rows):
                first = pl.multiple_of(r * MOE_SUB_ROWS * lane_rows, lane_rows) + i
                lo, hi = _unpack_bf16_pairs(gather_buf[pl.ds(first, MOE_SUB_ROWS, stride=lane_rows), :])
                xs_scr[sub_rows(r), i * LANES:(i + 1) * LANES] = lo
                xs_scr[sub_rows(r), half + i * LANES:half + (i + 1) * LANES] = hi
            return carry

        lax.fori_loop(0, n_a, unpack, 0)

        @pl.when(t + 1 < n_tiles)
        def _():
            start_gather(t + 1)

    @pl.when(run_a)
    def _():
        wg_scr[...] = wg_ref[0, 0].astype(BF16)
        wu_scr[...] = wu_ref[0, 0].astype(BF16)

    @pl.when(run_b)
    def _():
        wd_scr[...] = wd_ref[0, 0].astype(BF16)

    def span_rows(r, span):
        return pl.ds(pl.multiple_of(r * MOE_SUB_ROWS, MOE_SUB_ROWS), span * MOE_SUB_ROWS)

    def gate_up(r, span):
        rows = span_rows(r, span)
        x = xs_scr[rows, :]
        g = jnp.dot(x, wg_scr[...], preferred_element_type=F32) + bg_ref[0, j]
        u = jnp.dot(x, wu_scr[...], preferred_element_type=F32) + bu_ref[0, j]
        g = jnp.minimum(g, SWIGLU_LIMIT)
        u = jnp.clip(u, -SWIGLU_LIMIT, SWIGLU_LIMIT)
        act = (u + 1.0) * (g * jax.nn.sigmoid(SWIGLU_ALPHA * g))
        act_scr[slot_a, j, rows, :] = act.astype(BF16)

    def down(r, span):
        rows = span_rows(r, span)
        act = jnp.concatenate([act_scr[slot_b, jb, rows, :] for jb in range(n_j)], axis=1)
        y = jnp.dot(act, wd_scr[...], preferred_element_type=F32) + bd_ref[0, j]
        first = pl.multiple_of(r * MOE_SUB_ROWS * out_rows, out_rows) + j
        o_ref[pl.ds(first, span * MOE_SUB_ROWS, stride=out_rows), :] = _pack_bf16_pairs(y)

    def for_each_sub_block(n_sub, one, group, fused):
        def run(first, count):
            if fused:
                one(first, count)
            else:
                for q in range(count):
                    one(first + q, 1)

        def trip(i, carry):
            run(group * i, group)
            return carry

        lax.fori_loop(0, n_sub // group, trip, 0)
        rest = n_sub % group
        for count in range(1, group):
            @pl.when(rest == count)
            def _():
                run(n_sub - count, count)

    for_each_sub_block(n_a, gate_up, group=3, fused=False)
    for_each_sub_block(n_b, down, group=3, fused=True)

    @pl.when((t >= 1) & (j == 0))
    def _():
        def clear(r, carry):
            block = pl.ds(pl.multiple_of(r * MOE_SUB_ROWS * out_rows, out_rows), MOE_SUB_ROWS * out_rows)
            o_ref[block, :] = jnp.zeros((MOE_SUB_ROWS * out_rows, o_ref.shape[1]), o_ref.dtype)
            return carry

        lax.fori_loop(n_b, n_sub_total, clear, 0)


def _moe_experts(h_packed, pos, p_rows, tile_expert, tile_valid, layer, w_gate, b_gate, w_up, b_up, w_down, b_down):
    _, e, d, hdim = w_gate.shape
    tm, th, tn = MOE_TILE_ROWS, MOE_HIDDEN_BLOCK, MOE_OUT_BLOCK
    n_tiles = p_rows // tm
    n_j = hdim // th
    n_out = d // tn

    assert n_j == n_out, "gate/up and down halves share the block axis"
    assert tn == 2 * LANES, "one output column block packs into one 128-lane row of bf16 pairs"

    def up_tile(t):
        return jnp.minimum(t, n_tiles - 1)

    def down_tile(t):
        return jnp.maximum(t - 1, 0)

    def up_block(t, j, tv):
        return jnp.where((t < n_tiles) & (tv[up_tile(t)] > 0), j, n_j - 1)

    def down_block(t, j, tv):
        return jnp.where((t >= 1) & (tv[down_tile(t)] > 0), j, n_out - 1)

    grid_spec = pltpu.PrefetchScalarGridSpec(
        num_scalar_prefetch=3,
        grid=(n_tiles + 1, n_j),
        in_specs=[
            pl.BlockSpec(memory_space=pl.ANY),
            pl.BlockSpec((1, 1, d, th), lambda t, j, te, tv, src: (layer, te[up_tile(t)], 0, up_block(t, j, tv))),
            pl.BlockSpec((1, 1, d, th), lambda t, j, te, tv, src: (layer, te[up_tile(t)], 0, up_block(t, j, tv))),
            pl.BlockSpec((1, 1, hdim, tn), lambda t, j, te, tv, src: (layer, te[down_tile(t)], 0, down_block(t, j, tv))),
            pl.BlockSpec((1, n_j, 1, th), lambda t, j, te, tv, src: (te[up_tile(t)], 0, 0, 0)),
            pl.BlockSpec((1, n_j, 1, th), lambda t, j, te, tv, src: (te[up_tile(t)], 0, 0, 0)),
            pl.BlockSpec((1, n_out, 1, tn), lambda t, j, te, tv, src: (te[down_tile(t)], 0, 0, 0)),
        ],
        out_specs=pl.BlockSpec((tm * n_out, LANES), lambda t, j, te, tv, src: (down_tile(t), 0)),
        scratch_shapes=[
            pltpu.SMEM((p_rows,), jnp.int32),
            pltpu.VMEM((tm * (d // 2 // LANES), LANES), jnp.uint32),
            pltpu.SemaphoreType.DMA(()),
            pltpu.VMEM((tm, d), BF16),
            pltpu.VMEM((2, n_j, tm, th), BF16),
            pltpu.VMEM((d, th), BF16),
            pltpu.VMEM((d, th), BF16),
            pltpu.VMEM((hdim, tn), BF16),
        ],
    )
    return pl.pallas_call(
        _moe_kernel,
        out_shape=jax.ShapeDtypeStruct((p_rows * n_out, LANES), jnp.uint32),
        grid_spec=grid_spec,
        compiler_params=_params(("arbitrary", "arbitrary")),
        name="moe_experts",
    )(tile_expert, tile_valid, pos, h_packed, w_gate, w_up, w_down,
      b_gate[layer].reshape(e, n_j, 1, th), b_up[layer].reshape(e, n_j, 1, th), b_down[layer].reshape(e, n_out, 1, tn))


def _combine_kernel(pos_ref, ys_hbm, gates_ref, x_ref, gate2_ref, norm_ref, *rest, final_norm, out_tiles):
    o_refs = rest[:len(out_tiles)]
    rows_buf, rows_sem = rest[len(out_tiles):]
    i = pl.program_id(0)
    tm = x_ref.shape[0]
    slot = i % 2
    lane_rows = rows_buf.shape[1] // (TOP_K * tm)

    def token_rows(first, count=1):
        return pl.ds(pl.multiple_of(first * lane_rows, lane_rows), count * lane_rows)

    def start_gather(tile, slot):
        def issue(i, carry):
            for q in range(GATHER_UNROLL // TOP_K):
                r = i * (GATHER_UNROLL // TOP_K) + q
                for k in range(TOP_K):
                    row = pos_ref[(tile * tm + r) * TOP_K + k]
                    pltpu.make_async_copy(ys_hbm.at[token_rows(row), :], rows_buf.at[slot, token_rows(k * tm + r), :],
                                          rows_sem.at[slot]).start()
            return carry

        lax.fori_loop(0, tm // (GATHER_UNROLL // TOP_K), issue, 0)

    @pl.when(i == 0)
    def _():
        start_gather(0, 0)

    @pl.when(i + 1 < pl.num_programs(0))
    def _():
        start_gather(i + 1, 1 - slot)

    pltpu.make_async_copy(ys_hbm.at[pl.ds(0, TOP_K * tm * lane_rows), :], rows_buf.at[slot], rows_sem.at[slot]).wait()
    gates = gates_ref[...]

    def finish(p):
        o_ref = o_refs[p]
        for jr in range(lane_rows):
            lo = hi = None
            for k in range(TOP_K):
                w = rows_buf[slot, pl.ds(k * tm * lane_rows + jr, tm, stride=lane_rows), :]
                g = gates[:, k:k + 1]
                lo_k = g * lax.bitcast_convert_type(w << 16, F32)
                hi_k = g * lax.bitcast_convert_type(w & jnp.uint32(0xFFFF0000), F32)
                lo = lo_k if lo is None else lo + lo_k
                hi = hi_k if hi is None else hi + hi_k
            for part, cols in ((lo, pl.ds(2 * jr * LANES, LANES)), (hi, pl.ds((2 * jr + 1) * LANES, LANES))):
                o_ref[:, cols] = x_ref[:, cols] + gate2_ref[0, :, cols] * part
        if final_norm:
            x = o_ref[...]
            ms = jnp.mean(x * x, axis=-1, keepdims=True)
            o_ref[...] = x * lax.rsqrt(ms + EPS) * norm_ref[...]

    _for_row_part(i, out_tiles, finish)


def _combine(ys, pos, gates, x, gate2, norm_g, final_norm, out_rows):
    n, d = x.shape
    tm = COMBINE_ROWS
    lane_rows = d // (2 * LANES)
    out_specs, out_tiles, first = [], [], 0
    for rows in out_rows:
        n_tiles = rows // tm
        out_specs.append(pl.BlockSpec(
            (tm, d), lambda i, pos, first=first, n_tiles=n_tiles: (jnp.clip(i - first, 0, n_tiles - 1), 0)))
        out_tiles.append(n_tiles)
        first += n_tiles
    grid_spec = pltpu.PrefetchScalarGridSpec(
        num_scalar_prefetch=1,
        grid=(n // tm,),
        in_specs=[
            pl.BlockSpec(memory_space=pl.ANY),
            pl.BlockSpec((tm, TOP_K), lambda i, pos: (i, 0)),
            pl.BlockSpec((tm, d), lambda i, pos: (i, 0)),
            pl.BlockSpec((1, 1, d), lambda i, pos: (_segment_of_tile(i, tm), 0, 0)),
            pl.BlockSpec((1, d), lambda i, pos: (0, 0)),
        ],
        out_specs=out_specs,
        scratch_shapes=[pltpu.VMEM((2, TOP_K * tm * lane_rows, LANES), jnp.uint32), pltpu.SemaphoreType.DMA((2,))],
    )
    return pl.pallas_call(
        functools.partial(_combine_kernel, final_norm=final_norm, out_tiles=tuple(out_tiles)),
        out_shape=[jax.ShapeDtypeStruct((rows, d), F32) for rows in out_rows],
        grid_spec=grid_spec,
        compiler_params=_params(("arbitrary",)),
        name="moe_combine",
    )(pos.reshape(-1), ys, gates, x, gate2, norm_g.reshape(1, d))


def _moe_layer(x, g, scale, shift, gate2, layer, w_router, b_router, w_gate, b_gate, w_up, b_up, w_down, b_down,
               norm_g, final_norm, out_rows):
    n, d = x.shape
    e = w_router.shape[-1]
    tm = MOE_TILE_ROWS
    h, top_i, gates, rank, counts = _router(x, g, scale, shift, w_router[layer], b_router[layer])
    counts = counts[0].astype(jnp.int32)
    padded = ((counts + tm - 1) // tm) * tm
    ends = jnp.cumsum(padded)
    starts = ends - padded
    n_tiles = -(-(n * TOP_K) // tm) + e
    p_rows = n_tiles * tm
    tile_start = jnp.arange(n_tiles, dtype=jnp.int32) * tm
    n_used = ends[-1] // tm
    tile_expert = jnp.minimum(jnp.searchsorted(ends, tile_start, side="right"), e - 1).astype(jnp.int32)
    tile_valid = jnp.clip(counts[tile_expert] - (tile_start - starts[tile_expert]), 0, tm)
    tile_valid = jnp.where(tile_start < ends[-1], tile_valid, 0).astype(jnp.int32)
    last = jnp.maximum(n_used - 1, 0)
    tile_expert = jnp.where(tile_start < ends[-1], tile_expert, tile_expert[last]).astype(jnp.int32)
    pos = starts[top_i] + rank
    ys = _moe_experts(h, pos.reshape(-1), p_rows, tile_expert, tile_valid, layer, w_gate, b_gate, w_up, b_up, w_down, b_down)
    return _combine(ys, pos, gates, x, gate2, norm_g, final_norm, out_rows)


def kernel(x_prompt, x_sample, cache_k, cache_v, state_rglru, c, c_ctx, norm_mix, norm_ffn, w_mod, b_mod,
           attn_w_in, attn_b_in, attn_w_out, attn_b_out, attn_sink,
           rnn_w_in, rnn_b_in, rnn_conv_w, rnn_conv_b, rnn_w_a, rnn_b_a, rnn_w_x, rnn_b_x,
           rnn_lambda, rnn_w_out, rnn_b_out,
           moe_w_router, moe_b_router, moe_w_gate, moe_b_gate, moe_w_up, moe_b_up,
           moe_w_down, moe_b_down, final_norm):
    d = D_MODEL
    n_ctx = BATCH * SEQ
    attn_w = N_HEADS * HEAD_DIM
    kv_w = N_KV_HEADS * HEAD_DIM
    n_lat = DEC_BATCH * DEC_SEQ
    x_parts = (jnp.concatenate([x_prompt.reshape(n_ctx, d), x_sample.reshape(n_lat, d)], axis=0),)
    cond = jnp.concatenate([c_ctx[None, :], c, jnp.zeros((SUBLANES - 1 - DEC_BATCH, d), F32)], axis=0)
    mods = _modulation(cond, w_mod, b_mod)

    def mod(l, k):
        return mods[l, :, k * d:(k + 1) * d].reshape(SUBLANES, 1, d)

    new_k, new_v, new_s = [], [], []
    for l in range(DEPTH):
        j = l // 2
        sh1, sc1, g1, sh2, sc2, g2 = [mod(l, k) for k in range(N_MOD)]
        if l % 2 == 0:
            p = _proj_in(x_parts, norm_mix[l], sc1, sh1, attn_w_in[j], attn_b_in[j], tn=PROJ_IN_ATTN_COLS)
            new_k.append(p[:n_ctx, attn_w:attn_w + kv_w].reshape(BATCH, SEQ, N_KV_HEADS, HEAD_DIM))
            new_v.append(p[:n_ctx, attn_w + kv_w:attn_w + 2 * kv_w].reshape(BATCH, SEQ, N_KV_HEADS, HEAD_DIM))
            mix_ctx = _ctx_mixer(p, attn_sink[j])
            mix_lat = _lat_mixer(p, cache_k[:, j].reshape(DEC_BATCH, PAST_LEN, kv_w),
                                 cache_v[:, j].reshape(DEC_BATCH, PAST_LEN, kv_w), attn_sink[j])
            x = _proj_out((mix_ctx, mix_lat), attn_w_out[j], attn_b_out[j], g1, x_parts)
        else:
            p = _proj_in(x_parts, norm_mix[l], sc1, sh1, rnn_w_in[j], rnn_b_in[j], tn=PROJ_IN_RNN_COLS)
            args = (rnn_conv_w[j], rnn_conv_b[j], rnn_w_a[j], rnn_b_a[j], rnn_w_x[j], rnn_b_x[j], rnn_lambda[j])
            h0_ctx = jnp.zeros((BATCH, 2, d), F32)
            mix_ctx, st = _rnn_core(p, 0, n_ctx, h0_ctx, *args, n_seq=SUBLANES, seq_len=SEQ)
            mix_lat, _ = _rnn_core(p, n_ctx, DEC_BATCH * DEC_SEQ, state_rglru[:, j], *args,
                                   n_seq=DEC_BATCH, seq_len=DEC_SEQ)
            new_s.append(st)
            x = _proj_out((mix_ctx, mix_lat), rnn_w_out[j], rnn_b_out[j], g1, x_parts)
        last = l == DEPTH - 1
        x_parts = _moe_layer(x, norm_ffn[l], sc2, sh2, g2, l, moe_w_router, moe_b_router,
                             moe_w_gate, moe_b_gate, moe_w_up, moe_b_up, moe_w_down, moe_b_down,
                             final_norm, final_norm=last, out_rows=(n_ctx, n_lat) if last else (n_ctx + n_lat,))
    y_prompt = x_parts[0].reshape(BATCH, SEQ, d)
    y_sample = x_parts[1].reshape(DEC_BATCH, DEC_SEQ, d)
    return (y_prompt, y_sample, jnp.stack(new_k, axis=1), jnp.stack(new_v, axis=1), jnp.stack(new_s, axis=1))
```

```python
import functools
import math

import numpy as np
import jax
import jax.numpy as jnp
from jax import lax
from jax.experimental import pallas as pl
from jax.experimental.pallas import tpu as pltpu

D_MODEL = 2048
BATCH = 32
SEQ = 256
DEPTH = 2
DEC_BATCH = 2
DEC_SEQ = 1024
PAST_LEN = 256
GRID_W = 64
N_HEADS = 16
N_KV_HEADS = 2
HEAD_DIM = 64
WINDOW = 128
ROPE_THETA = 10000.0
N_FOURIER_GROUPS = 4
N_RNN_BLOCKS = 8
CONV_WIDTH = 4
CONV_LEFT = 2
RG_C = 8.0
N_EXPERTS = 32
TOP_K = 4
SWIGLU_LIMIT = 7.0
SWIGLU_ALPHA = 1.702
N_MOD = 6
EPS = 1e-6

LANES = 128
SUBLANES = 8
VMEM_LIMIT_BYTES = 56 * 1024 * 1024

MOE_TILE_ROWS = 1536
MOE_SUB_ROWS = 256
MOE_HIDDEN_BLOCK = 256
MOE_OUT_BLOCK = 256
GATHER_UNROLL = 8
SCAN_ROW_PAD = 8
SCAN_UNROLL = 4
ROW_TILE = 512
PROJ_ROWS = 1024
COMBINE_ROWS = 256
PROJ_OUT_COLS = 512
PROJ_IN_ATTN_COLS = 768
PROJ_IN_RNN_COLS = 1024

F32 = jnp.float32
BF16 = jnp.bfloat16


def _params(semantics):
    return pltpu.CompilerParams(dimension_semantics=semantics, vmem_limit_bytes=VMEM_LIMIT_BYTES)


def _segment_of_tile(i, tile_rows):
    n_ctx_tiles = (BATCH * SEQ) // tile_rows
    tiles_per_latent = DEC_SEQ // tile_rows
    return jnp.where(i < n_ctx_tiles, 0, 1 + (i - n_ctx_tiles) // tiles_per_latent)


def _rms_modulate(x, g, scale, shift):
    ms = jnp.mean(x * x, axis=-1, keepdims=True)
    return (x * lax.rsqrt(ms + EPS) * g) * (1.0 + scale) + shift


def _modulation_kernel(c_ref, w_ref, b_ref, o_ref):
    c = c_ref[...]
    s = (c * jax.nn.sigmoid(c)).astype(BF16)
    o_ref[0] = jnp.dot(s, w_ref[0].astype(BF16), preferred_element_type=F32) + b_ref[0]


def _modulation(cond, w_mod, b_mod):
    d = D_MODEL
    tn = 512
    n_out = N_MOD * d
    return pl.pallas_call(
        _modulation_kernel,
        out_shape=jax.ShapeDtypeStruct((DEPTH, SUBLANES, n_out), F32),
        grid=(DEPTH, n_out // tn),
        in_specs=[
            pl.BlockSpec((SUBLANES, d), lambda l, j: (0, 0)),
            pl.BlockSpec((1, d, tn), lambda l, j: (l, 0, j)),
            pl.BlockSpec((1, 1, tn), lambda l, j: (l, 0, j)),
        ],
        out_specs=pl.BlockSpec((1, SUBLANES, tn), lambda l, j: (l, 0, j)),
        compiler_params=_params(("arbitrary", "arbitrary")),
        name="modulation",
    )(cond, w_mod, b_mod.reshape(DEPTH, 1, n_out))


def _row_part_tiles(row_counts, tm):
    tiles, first = [], 0
    for rows in row_counts:
        tiles.append((first, rows // tm))
        first += rows // tm
    return tuple(tiles)


def _row_part_index(i, part):
    first, n_tiles = part
    return jnp.clip(i - first, 0, n_tiles - 1)


def _for_row_part(i, parts, fn):
    for p, (first, n_tiles) in enumerate(parts):
        @pl.when((i >= first) & (i < first + n_tiles))
        def _():
            fn(p)


def _proj_in_kernel(x_ref, g_ref, sc_ref, sh_ref, w_ref, b_ref, o_ref, h_ref):
    @pl.when(pl.program_id(1) == 0)
    def _():
        h_ref[...] = _rms_modulate(x_ref[...], g_ref[...], sc_ref[0], sh_ref[0]).astype(BF16)

    o_ref[...] = jnp.dot(h_ref[...], w_ref[...].astype(BF16), preferred_element_type=F32) + b_ref[...]


def _proj_in(x, g, scale, shift, w, b, tn):
    n, d = x.shape
    n_out = w.shape[1]
    tm = PROJ_ROWS
    seg = lambda i, j: (_segment_of_tile(i, tm), 0, 0)
    return pl.pallas_call(
        _proj_in_kernel,
        out_shape=jax.ShapeDtypeStruct((n, n_out), F32),
        grid=(n // tm, n_out // tn),
        in_specs=[
            pl.BlockSpec((tm, d), lambda i, j: (i, 0)),
            pl.BlockSpec((1, d), lambda i, j: (0, 0)),
            pl.BlockSpec((1, 1, d), seg),
            pl.BlockSpec((1, 1, d), seg),
            pl.BlockSpec((d, tn), lambda i, j: (0, j)),
            pl.BlockSpec((1, tn), lambda i, j: (0, j)),
        ],
        out_specs=pl.BlockSpec((tm, tn), lambda i, j: (i, j)),
        scratch_shapes=[pltpu.VMEM((tm, d), BF16)],
        compiler_params=_params(("arbitrary", "arbitrary")),
        name="proj_in",
    )(x, g.reshape(1, d), scale, shift, w, b.reshape(1, n_out))


def _proj_out_kernel(*refs, parts):
    a_refs = refs[:len(parts)]
    w_ref, b_ref, gate_ref, res_ref, o_ref = refs[len(parts):]

    def project(p):
        y = jnp.dot(a_refs[p][...], w_ref[...].astype(BF16), preferred_element_type=F32) + b_ref[...]
        o_ref[...] = res_ref[...] + gate_ref[0] * y

    _for_row_part(pl.program_id(0), parts, project)


def _proj_out(a_parts, w, b, gate, res):
    n, d = res.shape
    k = w.shape[0]
    tm, tn = PROJ_ROWS, PROJ_OUT_COLS
    parts = _row_part_tiles([a.shape[0] for a in a_parts], tm)
    return pl.pallas_call(
        functools.partial(_proj_out_kernel, parts=parts),
        out_shape=jax.ShapeDtypeStruct((n, d), F32),
        grid=(n // tm, d // tn),
        in_specs=[pl.BlockSpec((tm, k), lambda i, j, part=part: (_row_part_index(i, part), 0)) for part in parts] + [
            pl.BlockSpec((k, tn), lambda i, j: (0, j)),
            pl.BlockSpec((1, tn), lambda i, j: (0, j)),
            pl.BlockSpec((1, 1, tn), lambda i, j: (_segment_of_tile(i, tm), 0, j)),
            pl.BlockSpec((tm, tn), lambda i, j: (i, j)),
        ],
        out_specs=pl.BlockSpec((tm, tn), lambda i, j: (i, j)),
        compiler_params=_params(("arbitrary", "arbitrary")),
        name="proj_out",
    )(*a_parts, w, b.reshape(1, d), gate, res)


def _dot_nt(a, b):
    return lax.dot_general(a, b, (((1,), (1,)), ((), ())), preferred_element_type=F32)


def _head_pair_operands(k, v, group):
    lane = lax.broadcasted_iota(jnp.int32, k.shape, 1)
    low = lane < HEAD_DIM
    k_sw = pltpu.roll(k, HEAD_DIM, 1)
    v_sw = pltpu.roll(v, HEAD_DIM, 1)
    if group == 0:
        kd = jnp.where(low, k, k_sw)
        vd = jnp.where(low, v, v_sw)
    else:
        kd = jnp.where(low, k_sw, k)
        vd = jnp.where(low, v_sw, v)
    v_lo = jnp.where(low, vd, 0.0).astype(BF16)
    v_hi = jnp.where(low, 0.0, vd).astype(BF16)
    return kd.astype(BF16), v_lo, v_hi


def _split_pair(q2):
    lane = lax.broadcasted_iota(jnp.int32, q2.shape, 1)
    low = lane < HEAD_DIM
    qs = q2 * (HEAD_DIM ** -0.5)
    return jnp.where(low, qs, 0.0).astype(BF16), jnp.where(low, 0.0, qs).astype(BF16)


def _softmax_pv(scores, values, sink):
    m = jnp.full((scores[0].shape[0], 1), sink, F32)
    for s in scores:
        m = jnp.maximum(m, jnp.max(s, axis=-1, keepdims=True))
    den = jnp.exp(sink - m)
    ps = []
    for s in scores:
        p = jnp.exp(s - m)
        den = den + jnp.sum(p, axis=-1, keepdims=True)
        ps.append(p)
    out = None
    for p, v in zip(ps, values):
        o = jnp.dot((p / den).astype(BF16), v, preferred_element_type=F32)
        out = o if out is None else out + o
    return out


def _dft_matrices(t):
    idx = np.arange(t)
    ang = 2.0 * np.pi * ((idx[:, None] * idx[None, :]) % t) / t
    m = np.concatenate([np.cos(ang), np.sin(ang)], axis=0) / math.sqrt(t)
    return jnp.asarray(m, dtype=BF16)


def _dft_channel_matrix(c):
    idx = np.arange(c)
    ang = 2.0 * np.pi * ((idx[:, None] * idx[None, :]) % c) / c
    m = np.concatenate([np.cos(ang), -np.sin(ang)], axis=0) / math.sqrt(c)
    return jnp.asarray(m, dtype=BF16)


def _fourier_group(f_g, ts_ref, cs_ref):
    t = f_g.shape[0]
    ab = jnp.dot(ts_ref[...], f_g.astype(BF16), preferred_element_type=F32)
    lhs = jnp.concatenate([ab[:t], ab[t:]], axis=1).astype(BF16)
    return jnp.dot(lhs, cs_ref[...], preferred_element_type=F32)


def _ctx_mixer_kernel(sink_ref, p_ref, ts_ref, cs_ref, o_ref):
    attn_w = N_HEADS * HEAD_DIM
    kv_w = N_KV_HEADS * HEAD_DIM
    pair_w = 2 * HEAD_DIM
    group_heads = N_HEADS // N_KV_HEADS
    k = p_ref[:, attn_w:attn_w + kv_w]
    v = p_ref[:, attn_w + kv_w:attn_w + 2 * kv_w]
    for g in range(N_KV_HEADS):
        kd, v_lo, v_hi = _head_pair_operands(k, v, g)
        for i in range(group_heads // 2):
            pair = g * (group_heads // 2) + i
            q_lo, q_hi = _split_pair(p_ref[:, pair * pair_w:(pair + 1) * pair_w])
            o = _softmax_pv([_dot_nt(q_lo, kd)], [v_lo], sink_ref[2 * pair])
            o = o + _softmax_pv([_dot_nt(q_hi, kd)], [v_hi], sink_ref[2 * pair + 1])
            o_ref[:, pair * pair_w:(pair + 1) * pair_w] = o.astype(o_ref.dtype)
    f0 = attn_w + 2 * kv_w
    fg = (D_MODEL - attn_w) // N_FOURIER_GROUPS
    for g in range(N_FOURIER_GROUPS):
        z = _fourier_group(p_ref[:, f0 + g * fg:f0 + (g + 1) * fg], ts_ref, cs_ref)
        o_ref[:, attn_w + g * fg:attn_w + (g + 1) * fg] = z.astype(o_ref.dtype)


def _ctx_mixer(p, sink):
    n, width = BATCH * SEQ, p.shape[1]
    fg = (D_MODEL - N_HEADS * HEAD_DIM) // N_FOURIER_GROUPS
    return pl.pallas_call(
        _ctx_mixer_kernel,
        out_shape=jax.ShapeDtypeStruct((n, D_MODEL), BF16),
        grid=(n // SEQ,),
        in_specs=[
            pl.BlockSpec(memory_space=pltpu.SMEM),
            pl.BlockSpec((SEQ, width), lambda b: (b, 0)),
            pl.BlockSpec((2 * SEQ, SEQ), lambda b: (0, 0)),
            pl.BlockSpec((2 * fg, fg), lambda b: (0, 0)),
        ],
        out_specs=pl.BlockSpec((SEQ, D_MODEL), lambda b: (b, 0)),
        compiler_params=_params(("arbitrary",)),
        name="ctx_mixer",
    )(sink, p, _dft_matrices(SEQ), _dft_channel_matrix(fg))


def _rope_tables():
    rows = DEC_SEQ // GRID_W
    row = np.repeat(np.arange(rows, dtype=np.float32), GRID_W)
    col = np.tile(np.arange(GRID_W, dtype=np.float32), rows)
    n_freq = HEAD_DIM // 4
    inv = jnp.asarray(ROPE_THETA, F32) ** (-jnp.arange(n_freq, dtype=F32) / n_freq)
    ang = jnp.concatenate([row[:, None] * inv, col[:, None] * inv], axis=-1)
    cos = jnp.repeat(jnp.cos(ang), 2, axis=-1)
    sin = jnp.repeat(jnp.sin(ang), 2, axis=-1)
    sign = jnp.tile(jnp.asarray([-1.0, 1.0], F32), HEAD_DIM // 2)
    return jnp.tile(cos, (1, 2)), jnp.tile(sin * sign, (1, 2))


def _rope(x, cos, sin_signed):
    lane = lax.broadcasted_iota(jnp.int32, x.shape, 1)
    width = x.shape[1]
    partner = jnp.where(lane % 2 == 0, pltpu.roll(x, width - 1, 1), pltpu.roll(x, 1, 1))
    return x * cos + partner * sin_signed


def _lat_mixer_kernel(sink_ref, p_ref, ck_ref, cv_ref, cos_ref, sin_ref, ts_ref, cs_ref, o_ref, q_scr, k_scr):
    attn_w = N_HEADS * HEAD_DIM
    kv_w = N_KV_HEADS * HEAD_DIM
    pair_w = 2 * HEAD_DIM
    group_heads = N_HEADS // N_KV_HEADS
    q_rows = 256
    cos = cos_ref[...]
    sin = sin_ref[...]
    for pair in range(N_HEADS // 2):
        q_scr[:, pair * pair_w:(pair + 1) * pair_w] = _rope(p_ref[:, pair * pair_w:(pair + 1) * pair_w], cos, sin)
    k_scr[...] = _rope(p_ref[:, attn_w:attn_w + kv_w], cos, sin)
    v = p_ref[:, attn_w + kv_w:attn_w + 2 * kv_w]
    ck = ck_ref[0]
    cv = cv_ref[0]
    k = k_scr[...]
    ops = []
    for g in range(N_KV_HEADS):
        ops.append(_head_pair_operands(k, v, g) + _head_pair_operands(ck, cv, g))

    def chunk(c, carry):
        r0 = pl.multiple_of(c * q_rows, q_rows)
        qi = r0 + lax.broadcasted_iota(jnp.int32, (q_rows, DEC_SEQ), 0)
        kj = lax.broadcasted_iota(jnp.int32, (q_rows, DEC_SEQ), 1)
        valid = jnp.abs(qi - kj) <= WINDOW
        for g in range(N_KV_HEADS):
            kd, v_lo, v_hi, ckd, cv_lo, cv_hi = ops[g]
            for i in range(group_heads // 2):
                pair = g * (group_heads // 2) + i
                q_lo, q_hi = _split_pair(q_scr[pl.ds(r0, q_rows), pair * pair_w:(pair + 1) * pair_w])
                s_lo = jnp.where(valid, _dot_nt(q_lo, kd), -jnp.inf)
                o = _softmax_pv([_dot_nt(q_lo, ckd), s_lo], [cv_lo, v_lo], sink_ref[2 * pair])
                s_hi = jnp.where(valid, _dot_nt(q_hi, kd), -jnp.inf)
                o = o + _softmax_pv([_dot_nt(q_hi, ckd), s_hi], [cv_hi, v_hi], sink_ref[2 * pair + 1])
                o_ref[pl.ds(r0, q_rows), pair * pair_w:(pair + 1) * pair_w] = o.astype(o_ref.dtype)
        return carry

    lax.fori_loop(0, DEC_SEQ // q_rows, chunk, 0)
    f0 = attn_w + 2 * kv_w
    fg = (D_MODEL - attn_w) // N_FOURIER_GROUPS
    for g in range(N_FOURIER_GROUPS):
        z = _fourier_group(p_ref[:, f0 + g * fg:f0 + (g + 1) * fg], ts_ref, cs_ref)
        o_ref[:, attn_w + g * fg:attn_w + (g + 1) * fg] = z.astype(o_ref.dtype)


def _lat_mixer(p, cache_k, cache_v, sink):
    n, width = DEC_BATCH * DEC_SEQ, p.shape[1]
    first = (BATCH * SEQ) // DEC_SEQ
    kv_w = N_KV_HEADS * HEAD_DIM
    attn_w = N_HEADS * HEAD_DIM
    fg = (D_MODEL - attn_w) // N_FOURIER_GROUPS
    cos, sin = _rope_tables()
    return pl.pallas_call(
        _lat_mixer_kernel,
        out_shape=jax.ShapeDtypeStruct((n, D_MODEL), BF16),
        grid=(n // DEC_SEQ,),
        in_specs=[
            pl.BlockSpec(memory_space=pltpu.SMEM),
            pl.BlockSpec((DEC_SEQ, width), lambda b: (first + b, 0)),
            pl.BlockSpec((1, PAST_LEN, kv_w), lambda b: (b, 0, 0)),
            pl.BlockSpec((1, PAST_LEN, kv_w), lambda b: (b, 0, 0)),
            pl.BlockSpec((DEC_SEQ, 2 * HEAD_DIM), lambda b: (0, 0)),
            pl.BlockSpec((DEC_SEQ, 2 * HEAD_DIM), lambda b: (0, 0)),
            pl.BlockSpec((2 * DEC_SEQ, DEC_SEQ), lambda b: (0, 0)),
            pl.BlockSpec((2 * fg, fg), lambda b: (0, 0)),
        ],
        out_specs=pl.BlockSpec((DEC_SEQ, D_MODEL), lambda b: (b, 0)),
        scratch_shapes=[pltpu.VMEM((DEC_SEQ, attn_w), F32), pltpu.VMEM((DEC_SEQ, kv_w), F32)],
        compiler_params=_params(("arbitrary",)),
        name="lat_mixer",
    )(sink, p, cache_k, cache_v, cos, sin, _dft_matrices(DEC_SEQ), _dft_channel_matrix(fg))


def _rnn_core_kernel(xr_ref, gr_ref, cw_ref, cb_ref, wa_ref, ba_ref, wx_ref, bx_ref, lam_ref, h0_ref,
                     y_ref, st_ref, a_scr, b_scr, h_scr, *, n_seq, seq_len):
    rows = n_seq * seq_len
    n_lb = xr_ref.shape[1] // LANES
    xr = xr_ref[...]
    t_idx = lax.broadcasted_iota(jnp.int32, (rows, 1), 0) % seq_len
    xc = jnp.broadcast_to(cb_ref[...], xr.shape)
    for tap in range(CONV_WIDTH):
        off = tap - CONV_LEFT
        shifted = xr if off == 0 else pltpu.roll(xr, (-off) % rows, 0)
        valid = (t_idx + off >= 0) & (t_idx + off < seq_len)
        xc = xc + jnp.where(valid, shifted, 0.0) * cw_ref[tap:tap + 1, :]
    xcb = xc.astype(BF16)
    pitch = seq_len + SCAN_ROW_PAD

    def sigmoid(z):
        return 0.5 * jnp.tanh(0.5 * z) + 0.5

    def seq_rows(s):
        return pl.ds(s * pitch, seq_len)

    for d in range(2):
        r = sigmoid(jnp.dot(xcb, wa_ref[d, 0].astype(BF16), preferred_element_type=F32) + ba_ref[d:d + 1, :])
        gi = sigmoid(jnp.dot(xcb, wx_ref[d, 0].astype(BF16), preferred_element_type=F32) + bx_ref[d:d + 1, :])
        neg_lam = -lam_ref[d:d + 1, :]
        softplus = jnp.maximum(neg_lam, 0.0) + jnp.log1p(jnp.exp(-jnp.abs(neg_lam)))
        a = jnp.exp(-RG_C * r * softplus)
        b = jnp.sqrt(1.0 - a * a) * (gi * xc)
        for lb in range(n_lb):
            for s in range(n_seq):
                a_scr[d * n_lb + lb, seq_rows(s), :] = a[s * seq_len:(s + 1) * seq_len, lb * LANES:(lb + 1) * LANES]
                b_scr[d * n_lb + lb, seq_rows(s), :] = b[s * seq_len:(s + 1) * seq_len, lb * LANES:(lb + 1) * LANES]

    def time_rows(t):
        return pl.ds(t, n_seq, stride=pitch)

    def lane_block(ref, k, lb):
        return ref[:, k, lb * LANES:(lb + 1) * LANES]

    def fwd(t, hs):
        out = []
        for lb in range(n_lb):
            h = a_scr[lb, time_rows(t), :] * hs[lb] + b_scr[lb, time_rows(t), :]
            h_scr[lb, time_rows(t), :] = h
            out.append(h)
        return tuple(out)

    hs = lax.fori_loop(0, seq_len, fwd, tuple(lane_block(h0_ref, 0, lb) for lb in range(n_lb)), unroll=SCAN_UNROLL)
    for lb in range(n_lb):
        st_ref[:, 0, lb * LANES:(lb + 1) * LANES] = hs[lb]

    def bwd(i, hs):
        t = seq_len - 1 - i
        out = []
        for lb in range(n_lb):
            h = a_scr[n_lb + lb, time_rows(t), :] * hs[lb] + b_scr[n_lb + lb, time_rows(t), :]
            h_scr[lb, time_rows(t), :] = h_scr[lb, time_rows(t), :] + h
            out.append(h)
        return tuple(out)

    hs = lax.fori_loop(0, seq_len, bwd, tuple(lane_block(h0_ref, 1, lb) for lb in range(n_lb)), unroll=SCAN_UNROLL)
    for lb in range(n_lb):
        st_ref[:, 1, lb * LANES:(lb + 1) * LANES] = hs[lb]
    for lb in range(n_lb):
        for s in range(n_seq):
            gr = gr_ref[s * seq_len:(s + 1) * seq_len, lb * LANES:(lb + 1) * LANES]
            gelu = 0.5 * gr * (1.0 + jnp.tanh(math.sqrt(2.0 / math.pi) * (gr + 0.044715 * (gr * gr * gr))))
            y_ref[s * seq_len:(s + 1) * seq_len, lb * LANES:(lb + 1) * LANES] = (
                gelu * h_scr[lb, seq_rows(s), :]).astype(y_ref.dtype)


def _rnn_core(p, first_row, n, h0, conv_w, conv_b, w_a, b_a, w_x, b_x, lam, n_seq, seq_len):
    d_rnn = p.shape[1] // 2
    cb = d_rnn // N_RNN_BLOCKS
    rows = n_seq * seq_len
    n_batch = n // seq_len
    first = first_row // rows
    scan_rows = n_seq * (seq_len + SCAN_ROW_PAD)
    kern = functools.partial(_rnn_core_kernel, n_seq=n_seq, seq_len=seq_len)
    return pl.pallas_call(
        kern,
        out_shape=(jax.ShapeDtypeStruct((n, d_rnn), BF16), jax.ShapeDtypeStruct((n_batch, 2, d_rnn), F32)),
        grid=(n // rows, N_RNN_BLOCKS),
        in_specs=[
            pl.BlockSpec((rows, cb), lambda i, c: (first + i, c)),
            pl.BlockSpec((rows, cb), lambda i, c: (first + i, N_RNN_BLOCKS + c)),
            pl.BlockSpec((CONV_WIDTH, cb), lambda i, c: (0, c)),
            pl.BlockSpec((1, cb), lambda i, c: (0, c)),
            pl.BlockSpec((2, 1, cb, cb), lambda i, c: (0, c, 0, 0)),
            pl.BlockSpec((2, cb), lambda i, c: (0, c)),
            pl.BlockSpec((2, 1, cb, cb), lambda i, c: (0, c, 0, 0)),
            pl.BlockSpec((2, cb), lambda i, c: (0, c)),
            pl.BlockSpec((2, cb), lambda i, c: (0, c)),
            pl.BlockSpec((n_seq, 2, cb), lambda i, c: (i, 0, c)),
        ],
        out_specs=(
            pl.BlockSpec((rows, cb), lambda i, c: (i, c)),
            pl.BlockSpec((n_seq, 2, cb), lambda i, c: (i, 0, c)),
        ),
        scratch_shapes=[pltpu.VMEM((2 * cb // LANES, scan_rows, LANES), F32),
                        pltpu.VMEM((2 * cb // LANES, scan_rows, LANES), F32),
                        pltpu.VMEM((cb // LANES, scan_rows, LANES), F32)],
        compiler_params=_params(("arbitrary", "arbitrary")),
        name="rnn_core",
    )(p, p, conv_w, conv_b.reshape(1, d_rnn), w_a, b_a, w_x, b_x, lam, h0)


def _pack_bf16_pairs(h):
    half = h.shape[1] // 2
    bits = lax.bitcast_convert_type(h.astype(BF16).astype(F32), jnp.uint32)
    return (bits[:, :half] >> 16) | (bits[:, half:] & jnp.uint32(0xFFFF0000))


def _unpack_bf16_pairs(w):
    lo = lax.bitcast_convert_type(w << 16, F32)
    hi = lax.bitcast_convert_type(w & jnp.uint32(0xFFFF0000), F32)
    return lo.astype(BF16), hi.astype(BF16)


def _router_kernel(x_ref, g_ref, sc_ref, sh_ref, w_ref, b_ref, h_ref, idx_ref, gate_ref, rank_ref, cnt_ref, run_ref):
    tm = x_ref.shape[0]

    @pl.when(pl.program_id(0) == 0)
    def _():
        run_ref[...] = jnp.zeros_like(run_ref)

    h = _rms_modulate(x_ref[...], g_ref[...], sc_ref[0], sh_ref[0])
    packed = _pack_bf16_pairs(h)
    lane_rows = packed.shape[1] // LANES
    for i in range(lane_rows):
        h_ref[pl.ds(i, tm, stride=lane_rows), :] = packed[:, i * LANES:(i + 1) * LANES]
    w = w_ref[...]
    h_hi, w_hi = h.astype(BF16), w.astype(BF16)
    h_lo = (h - h_hi.astype(F32)).astype(BF16)
    w_lo = (w - w_hi.astype(F32)).astype(BF16)
    logits = (jnp.dot(h_hi, w_hi, preferred_element_type=F32)
              + (jnp.dot(h_lo, w_hi, preferred_element_type=F32) + jnp.dot(h_hi, w_lo, preferred_element_type=F32))
              + b_ref[...])
    lane = lax.broadcasted_iota(jnp.int32, logits.shape, 1).astype(F32)
    col = lax.broadcasted_iota(jnp.int32, (tm, TOP_K), 1)
    chosen = jnp.zeros(logits.shape, F32)
    top_v, top_i, hits = [], [], []
    work = logits
    for _ in range(TOP_K):
        m = jnp.max(work, axis=-1, keepdims=True)
        first = jnp.min(jnp.where(work == m, lane, float(N_EXPERTS)), axis=-1, keepdims=True)
        hit = lane == first
        work = jnp.where(hit, -jnp.inf, work)
        chosen = jnp.where(hit, 1.0, chosen)
        top_v.append(m)
        top_i.append(first)
        hits.append(hit)
    exps = [jnp.exp(v - top_v[0]) for v in top_v]
    den = exps[0]
    for e in exps[1:]:
        den = den + e
    ri = lax.broadcasted_iota(jnp.int32, (tm, tm), 0)
    ci = lax.broadcasted_iota(jnp.int32, (tm, tm), 1)
    before = (ci < ri).astype(BF16)
    rank_all = jnp.dot(before, chosen.astype(BF16), preferred_element_type=F32) + run_ref[...]
    idx_out = jnp.zeros((tm, TOP_K), F32)
    gate_out = jnp.zeros((tm, TOP_K), F32)
    rank_out = jnp.zeros((tm, TOP_K), F32)
    for k in range(TOP_K):
        rk = jnp.sum(jnp.where(hits[k], rank_all, 0.0), axis=-1, keepdims=True)
        idx_out = jnp.where(col == k, top_i[k], idx_out)
        gate_out = jnp.where(col == k, exps[k] / den, gate_out)
        rank_out = jnp.where(col == k, rk, rank_out)
    idx_ref[...] = idx_out.astype(jnp.int32)
    gate_ref[...] = gate_out
    rank_ref[...] = rank_out.astype(jnp.int32)
    run_ref[...] = run_ref[...] + jnp.sum(chosen, axis=0, keepdims=True)
    cnt_ref[...] = run_ref[...]


def _router(x, g, scale, shift, w, b):
    n, d = x.shape
    e = w.shape[1]
    tm = ROW_TILE
    seg = lambda i: (_segment_of_tile(i, tm), 0, 0)
    small = pl.BlockSpec((tm, TOP_K), lambda i: (i, 0))
    return pl.pallas_call(
        _router_kernel,
        out_shape=(
            jax.ShapeDtypeStruct((n * (d // 2 // LANES), LANES), jnp.uint32),
            jax.ShapeDtypeStruct((n, TOP_K), jnp.int32),
            jax.ShapeDtypeStruct((n, TOP_K), F32),
            jax.ShapeDtypeStruct((n, TOP_K), jnp.int32),
            jax.ShapeDtypeStruct((1, e), F32),
        ),
        grid=(n // tm,),
        in_specs=[
            pl.BlockSpec((tm, d), lambda i: (i, 0)),
            pl.BlockSpec((1, d), lambda i: (0, 0)),
            pl.BlockSpec((1, 1, d), seg),
            pl.BlockSpec((1, 1, d), seg),
            pl.BlockSpec((d, e), lambda i: (0, 0)),
            pl.BlockSpec((1, e), lambda i: (0, 0)),
        ],
        out_specs=(pl.BlockSpec((tm * (d // 2 // LANES), LANES), lambda i: (i, 0)), small, small, small,
                   pl.BlockSpec((1, e), lambda i: (0, 0))),
        scratch_shapes=[pltpu.VMEM((1, e), F32)],
        compiler_params=_params(("arbitrary",)),
        name="router",
    )(x, g.reshape(1, d), scale, shift, w, b.reshape(1, e))


def _moe_kernel(te_ref, tv_ref, pos_ref, h_hbm, wg_ref, wu_ref, wd_ref, bg_ref, bu_ref, bd_ref, o_ref,
                src_ref, gather_buf, gather_sem, xs_scr, act_scr, wg_scr, wu_scr, wd_scr):
    t = pl.program_id(0)
    j = pl.program_id(1)
    n_tiles = pl.num_programs(0) - 1
    n_j = act_scr.shape[1]
    tile_rows = xs_scr.shape[0]
    n_sub_total = tile_rows // MOE_SUB_ROWS
    out_rows = o_ref.shape[0] // tile_rows
    ta = jnp.minimum(t, n_tiles - 1)
    tb = jnp.maximum(t - 1, 0)

    def sub_blocks(tile):
        return (tv_ref[tile] + MOE_SUB_ROWS - 1) // MOE_SUB_ROWS

    def sub_rows(r):
        return pl.ds(pl.multiple_of(r * MOE_SUB_ROWS, MOE_SUB_ROWS), MOE_SUB_ROWS)

    lane_rows = xs_scr.shape[1] // 2 // LANES
    half = xs_scr.shape[1] // 2

    def token_rows(first_token, count=1):
        return pl.ds(pl.multiple_of(first_token * lane_rows, lane_rows), count * lane_rows)

    def start_gather(tile):
        def issue(i, carry):
            for q in range(GATHER_UNROLL):
                r = i * GATHER_UNROLL + q
                token = src_ref[tile * tile_rows + r]
                pltpu.make_async_copy(h_hbm.at[token_rows(token), :], gather_buf.at[token_rows(r), :], gather_sem).start()
            return carry

        lax.fori_loop(0, sub_blocks(tile) * (MOE_SUB_ROWS // GATHER_UNROLL), issue, 0)

    def wait_gather(tile):
        def wait_sub(r, carry):
            block = token_rows(r * MOE_SUB_ROWS, MOE_SUB_ROWS)
            pltpu.make_async_copy(h_hbm.at[block, :], gather_buf.at[block, :], gather_sem).wait()
            return carry

        lax.fori_loop(0, sub_blocks(tile), wait_sub, 0)

    @pl.when((t == 0) & (j == 0))
    def _():
        tokens_per_trip = GATHER_UNROLL // TOP_K

        def put(i, carry):
            for q in range(tokens_per_trip):
                token = i * tokens_per_trip + q
                for k in range(TOP_K):
                    src_ref[pos_ref[token * TOP_K + k]] = token
            return carry

        lax.fori_loop(0, pos_ref.shape[0] // GATHER_UNROLL, put, 0)

        def pad_tile(tile, carry):
            def pad(r, inner):
                src_ref[tile * tile_rows + r] = 0
                return inner

            lax.fori_loop(tv_ref[tile], sub_blocks(tile) * MOE_SUB_ROWS, pad, 0)
            return carry

        lax.fori_loop(0, n_tiles, pad_tile, 0)

    run_a = (t < n_tiles) & (tv_ref[ta] > 0)
    run_b = (t >= 1) & (tv_ref[tb] > 0)
    n_a = jnp.where(run_a, sub_blocks(ta), 0)
    n_b = jnp.where(run_b, sub_blocks(tb), 0)
    slot_a = ta % 2
    slot_b = tb % 2

    @pl.when((j == 0) & run_a)
    def _():
        @pl.when(t == 0)
        def _():
            start_gather(0)

        wait_gather(t)

        def unpack(r, carry):
            for i in range(lane_rows):
                first = pl.multiple_of(r * MOE_SUB_ROWS * lane_rows, lane_rows) + i
                lo, hi = _unpack_bf16_pairs(gather_buf[pl.ds(first, MOE_SUB_ROWS, stride=lane_rows), :])
                xs_scr[sub_rows(r), i * LANES:(i + 1) * LANES] = lo
                xs_scr[sub_rows(r), half + i * LANES:half + (i + 1) * LANES] = hi
            return carry

        lax.fori_loop(0, n_a, unpack, 0)

        @pl.when(t + 1 < n_tiles)
        def _():
            start_gather(t + 1)

    @pl.when(run_a)
    def _():
        wg_scr[...] = wg_ref[0, 0].astype(BF16)
        wu_scr[...] = wu_ref[0, 0].astype(BF16)

    @pl.when(run_b)
    def _():
        wd_scr[...] = wd_ref[0, 0].astype(BF16)

    def span_rows(r, span):
        return pl.ds(pl.multiple_of(r * MOE_SUB_ROWS, MOE_SUB_ROWS), span * MOE_SUB_ROWS)

    def gate_up(r, span):
        rows = span_rows(r, span)
        x = xs_scr[rows, :]
        g = jnp.dot(x, wg_scr[...], preferred_element_type=F32) + bg_ref[0, j]
        u = jnp.dot(x, wu_scr[...], preferred_element_type=F32) + bu_ref[0, j]
        g = jnp.minimum(g, SWIGLU_LIMIT)
        u = jnp.clip(u, -SWIGLU_LIMIT, SWIGLU_LIMIT)
        act = (u + 1.0) * (g * jax.nn.sigmoid(SWIGLU_ALPHA * g))
        act_scr[slot_a, j, rows, :] = act.astype(BF16)

    def down(r, span):
        rows = span_rows(r, span)
        act = jnp.concatenate([act_scr[slot_b, jb, rows, :] for jb in range(n_j)], axis=1)
        y = jnp.dot(act, wd_scr[...], preferred_element_type=F32) + bd_ref[0, j]
        first = pl.multiple_of(r * MOE_SUB_ROWS * out_rows, out_rows) + j
        o_ref[pl.ds(first, span * MOE_SUB_ROWS, stride=out_rows), :] = _pack_bf16_pairs(y)

    def for_each_sub_block(n_sub, one, group, fused):
        def run(first, count):
            if fused:
                one(first, count)
            else:
                for q in range(count):
                    one(first + q, 1)

        def trip(i, carry):
            run(group * i, group)
            return carry

        lax.fori_loop(0, n_sub // group, trip, 0)
        rest = n_sub % group
        for count in range(1, group):
            @pl.when(rest == count)
            def _():
                run(n_sub - count, count)

    for_each_sub_block(n_a, gate_up, group=3, fused=False)
    for_each_sub_block(n_b, down, group=3, fused=True)

    @pl.when((t >= 1) & (j == 0))
    def _():
        def clear(r, carry):
            block = pl.ds(pl.multiple_of(r * MOE_SUB_ROWS * out_rows, out_rows), MOE_SUB_ROWS * out_rows)
            o_ref[block, :] = jnp.zeros((MOE_SUB_ROWS * out_rows, o_ref.shape[1]), o_ref.dtype)
            return carry

        lax.fori_loop(n_b, n_sub_total, clear, 0)


def _moe_experts(h_packed, pos, p_rows, tile_expert, tile_valid, layer, w_gate, b_gate, w_up, b_up, w_down, b_down):
    _, e, d, hdim = w_gate.shape
    tm, th, tn = MOE_TILE_ROWS, MOE_HIDDEN_BLOCK, MOE_OUT_BLOCK
    n_tiles = p_rows // tm
    n_j = hdim // th
    n_out = d // tn

    assert n_j == n_out, "gate/up and down halves share the block axis"
    assert tn == 2 * LANES, "one output column block packs into one 128-lane row of bf16 pairs"

    def up_tile(t):
        return jnp.minimum(t, n_tiles - 1)

    def down_tile(t):
        return jnp.maximum(t - 1, 0)

    def up_block(t, j, tv):
        return jnp.where((t < n_tiles) & (tv[up_tile(t)] > 0), j, n_j - 1)

    def down_block(t, j, tv):
        return jnp.where((t >= 1) & (tv[down_tile(t)] > 0), j, n_out - 1)

    grid_spec = pltpu.PrefetchScalarGridSpec(
        num_scalar_prefetch=3,
        grid=(n_tiles + 1, n_j),
        in_specs=[
            pl.BlockSpec(memory_space=pl.ANY),
            pl.BlockSpec((1, 1, d, th), lambda t, j, te, tv, src: (layer, te[up_tile(t)], 0, up_block(t, j, tv))),
            pl.BlockSpec((1, 1, d, th), lambda t, j, te, tv, src: (layer, te[up_tile(t)], 0, up_block(t, j, tv))),
            pl.BlockSpec((1, 1, hdim, tn), lambda t, j, te, tv, src: (layer, te[down_tile(t)], 0, down_block(t, j, tv))),
            pl.BlockSpec((1, n_j, 1, th), lambda t, j, te, tv, src: (te[up_tile(t)], 0, 0, 0)),
            pl.BlockSpec((1, n_j, 1, th), lambda t, j, te, tv, src: (te[up_tile(t)], 0, 0, 0)),
            pl.BlockSpec((1, n_out, 1, tn), lambda t, j, te, tv, src: (te[down_tile(t)], 0, 0, 0)),
        ],
        out_specs=pl.BlockSpec((tm * n_out, LANES), lambda t, j, te, tv, src: (down_tile(t), 0)),
        scratch_shapes=[
            pltpu.SMEM((p_rows,), jnp.int32),
            pltpu.VMEM((tm * (d // 2 // LANES), LANES), jnp.uint32),
            pltpu.SemaphoreType.DMA(()),
            pltpu.VMEM((tm, d), BF16),
            pltpu.VMEM((2, n_j, tm, th), BF16),
            pltpu.VMEM((d, th), BF16),
            pltpu.VMEM((d, th), BF16),
            pltpu.VMEM((hdim, tn), BF16),
        ],
    )
    return pl.pallas_call(
        _moe_kernel,
        out_shape=jax.ShapeDtypeStruct((p_rows * n_out, LANES), jnp.uint32),
        grid_spec=grid_spec,
        compiler_params=_params(("arbitrary", "arbitrary")),
        name="moe_experts",
    )(tile_expert, tile_valid, pos, h_packed, w_gate, w_up, w_down,
      b_gate[layer].reshape(e, n_j, 1, th), b_up[layer].reshape(e, n_j, 1, th), b_down[layer].reshape(e, n_out, 1, tn))


def _combine_kernel(pos_ref, ys_hbm, gates_ref, x_ref, gate2_ref, norm_ref, *rest, final_norm, out_parts):
    o_refs = rest[:len(out_parts)]
    rows_buf, rows_sem = rest[len(out_parts):]
    i = pl.program_id(0)
    tm = x_ref.shape[0]
    slot = i % 2
    lane_rows = rows_buf.shape[1] // (TOP_K * tm)

    def token_rows(first, count=1):
        return pl.ds(pl.multiple_of(first * lane_rows, lane_rows), count * lane_rows)

    def start_gather(tile, slot):
        def issue(i, carry):
            for q in range(GATHER_UNROLL // TOP_K):
                r = i * (GATHER_UNROLL // TOP_K) + q
                for k in range(TOP_K):
                    row = pos_ref[(tile * tm + r) * TOP_K + k]
                    pltpu.make_async_copy(ys_hbm.at[token_rows(row), :], rows_buf.at[slot, token_rows(k * tm + r), :],
                                          rows_sem.at[slot]).start()
            return carry

        lax.fori_loop(0, tm // (GATHER_UNROLL // TOP_K), issue, 0)

    @pl.when(i == 0)
    def _():
        start_gather(0, 0)

    @pl.when(i + 1 < pl.num_programs(0))
    def _():
        start_gather(i + 1, 1 - slot)

    pltpu.make_async_copy(ys_hbm.at[pl.ds(0, TOP_K * tm * lane_rows), :], rows_buf.at[slot], rows_sem.at[slot]).wait()
    gates = gates_ref[...]

    def finish(p):
        o_ref = o_refs[p]
        for jr in range(lane_rows):
            lo = hi = None
            for k in range(TOP_K):
                w = rows_buf[slot, pl.ds(k * tm * lane_rows + jr, tm, stride=lane_rows), :]
                g = gates[:, k:k + 1]
                lo_k = g * lax.bitcast_convert_type(w << 16, F32)
                hi_k = g * lax.bitcast_convert_type(w & jnp.uint32(0xFFFF0000), F32)
                lo = lo_k if lo is None else lo + lo_k
                hi = hi_k if hi is None else hi + hi_k
            for part, cols in ((lo, pl.ds(2 * jr * LANES, LANES)), (hi, pl.ds((2 * jr + 1) * LANES, LANES))):
                o_ref[:, cols] = x_ref[:, cols] + gate2_ref[0, :, cols] * part
        if final_norm:
            x = o_ref[...]
            ms = jnp.mean(x * x, axis=-1, keepdims=True)
            o_ref[...] = x * lax.rsqrt(ms + EPS) * norm_ref[...]

    _for_row_part(i, out_parts, finish)


def _combine(ys, pos, gates, x, gate2, norm_g, final_norm, out_rows):
    n, d = x.shape
    tm = COMBINE_ROWS
    lane_rows = d // (2 * LANES)
    out_parts = _row_part_tiles(out_rows, tm)
    out_specs = [pl.BlockSpec((tm, d), lambda i, pos, part=part: (_row_part_index(i, part), 0)) for part in out_parts]
    grid_spec = pltpu.PrefetchScalarGridSpec(
        num_scalar_prefetch=1,
        grid=(n // tm,),
        in_specs=[
            pl.BlockSpec(memory_space=pl.ANY),
            pl.BlockSpec((tm, TOP_K), lambda i, pos: (i, 0)),
            pl.BlockSpec((tm, d), lambda i, pos: (i, 0)),
            pl.BlockSpec((1, 1, d), lambda i, pos: (_segment_of_tile(i, tm), 0, 0)),
            pl.BlockSpec((1, d), lambda i, pos: (0, 0)),
        ],
        out_specs=out_specs,
        scratch_shapes=[pltpu.VMEM((2, TOP_K * tm * lane_rows, LANES), jnp.uint32), pltpu.SemaphoreType.DMA((2,))],
    )
    return pl.pallas_call(
        functools.partial(_combine_kernel, final_norm=final_norm, out_parts=out_parts),
        out_shape=[jax.ShapeDtypeStruct((rows, d), F32) for rows in out_rows],
        grid_spec=grid_spec,
        compiler_params=_params(("arbitrary",)),
        name="moe_combine",
    )(pos.reshape(-1), ys, gates, x, gate2, norm_g.reshape(1, d))


def _moe_layer(x, g, scale, shift, gate2, layer, w_router, b_router, w_gate, b_gate, w_up, b_up, w_down, b_down,
               norm_g, final_norm, out_rows):
    n, d = x.shape
    e = w_router.shape[-1]
    tm = MOE_TILE_ROWS
    h, top_i, gates, rank, counts = _router(x, g, scale, shift, w_router[layer], b_router[layer])
    counts = counts[0].astype(jnp.int32)
    padded = ((counts + tm - 1) // tm) * tm
    ends = jnp.cumsum(padded)
    starts = ends - padded
    n_tiles = -(-(n * TOP_K) // tm) + e
    p_rows = n_tiles * tm
    tile_start = jnp.arange(n_tiles, dtype=jnp.int32) * tm
    n_used = ends[-1] // tm
    tile_expert = jnp.minimum(jnp.searchsorted(ends, tile_start, side="right"), e - 1).astype(jnp.int32)
    tile_valid = jnp.clip(counts[tile_expert] - (tile_start - starts[tile_expert]), 0, tm)
    tile_valid = jnp.where(tile_start < ends[-1], tile_valid, 0).astype(jnp.int32)
    last = jnp.maximum(n_used - 1, 0)
    tile_expert = jnp.where(tile_start < ends[-1], tile_expert, tile_expert[last]).astype(jnp.int32)
    pos = starts[top_i] + rank
    ys = _moe_experts(h, pos.reshape(-1), p_rows, tile_expert, tile_valid, layer, w_gate, b_gate, w_up, b_up, w_down, b_down)
    return _combine(ys, pos, gates, x, gate2, norm_g, final_norm, out_rows)


def kernel(x_prompt, x_sample, cache_k, cache_v, state_rglru, c, c_ctx, norm_mix, norm_ffn, w_mod, b_mod,
           attn_w_in, attn_b_in, attn_w_out, attn_b_out, attn_sink,
           rnn_w_in, rnn_b_in, rnn_conv_w, rnn_conv_b, rnn_w_a, rnn_b_a, rnn_w_x, rnn_b_x,
           rnn_lambda, rnn_w_out, rnn_b_out,
           moe_w_router, moe_b_router, moe_w_gate, moe_b_gate, moe_w_up, moe_b_up,
           moe_w_down, moe_b_down, final_norm):
    d = D_MODEL
    n_ctx = BATCH * SEQ
    attn_w = N_HEADS * HEAD_DIM
    kv_w = N_KV_HEADS * HEAD_DIM
    n_lat = DEC_BATCH * DEC_SEQ
    x = jnp.concatenate([x_prompt.reshape(n_ctx, d), x_sample.reshape(n_lat, d)], axis=0)
    cond = jnp.concatenate([c_ctx[None, :], c, jnp.zeros((SUBLANES - 1 - DEC_BATCH, d), F32)], axis=0)
    mods = _modulation(cond, w_mod, b_mod)

    def mod(l, k):
        return mods[l, :, k * d:(k + 1) * d].reshape(SUBLANES, 1, d)

    new_k, new_v, new_s = [], [], []
    for l in range(DEPTH):
        j = l // 2
        sh1, sc1, g1, sh2, sc2, g2 = [mod(l, k) for k in range(N_MOD)]
        if l % 2 == 0:
            p = _proj_in(x, norm_mix[l], sc1, sh1, attn_w_in[j], attn_b_in[j], tn=PROJ_IN_ATTN_COLS)
            new_k.append(p[:n_ctx, attn_w:attn_w + kv_w].reshape(BATCH, SEQ, N_KV_HEADS, HEAD_DIM))
            new_v.append(p[:n_ctx, attn_w + kv_w:attn_w + 2 * kv_w].reshape(BATCH, SEQ, N_KV_HEADS, HEAD_DIM))
            mix_ctx = _ctx_mixer(p, attn_sink[j])
            mix_lat = _lat_mixer(p, cache_k[:, j].reshape(DEC_BATCH, PAST_LEN, kv_w),
                                 cache_v[:, j].reshape(DEC_BATCH, PAST_LEN, kv_w), attn_sink[j])
            x = _proj_out((mix_ctx, mix_lat), attn_w_out[j], attn_b_out[j], g1, x)
        else:
            p = _proj_in(x, norm_mix[l], sc1, sh1, rnn_w_in[j], rnn_b_in[j], tn=PROJ_IN_RNN_COLS)
            args = (rnn_conv_w[j], rnn_conv_b[j], rnn_w_a[j], rnn_b_a[j], rnn_w_x[j], rnn_b_x[j], rnn_lambda[j])
            h0_ctx = jnp.zeros((BATCH, 2, d), F32)
            mix_ctx, st = _rnn_core(p, 0, n_ctx, h0_ctx, *args, n_seq=SUBLANES, seq_len=SEQ)
            mix_lat, _ = _rnn_core(p, n_ctx, DEC_BATCH * DEC_SEQ, state_rglru[:, j], *args,
                                   n_seq=DEC_BATCH, seq_len=DEC_SEQ)
            new_s.append(st)
            x = _proj_out((mix_ctx, mix_lat), rnn_w_out[j], rnn_b_out[j], g1, x)
        last = l == DEPTH - 1
        outs = _moe_layer(x, norm_ffn[l], sc2, sh2, g2, l, moe_w_router, moe_b_router,
                          moe_w_gate, moe_b_gate, moe_w_up, moe_b_up, moe_w_down, moe_b_down,
                          final_norm, final_norm=last, out_rows=(n_ctx, n_lat) if last else (n_ctx + n_lat,))
        x = outs[0]
    y_prompt = outs[0].reshape(BATCH, SEQ, d)
    y_sample = outs[1].reshape(DEC_BATCH, DEC_SEQ, d)
    return (y_prompt, y_sample, jnp.stack(new_k, axis=1), jnp.stack(new_v, axis=1), jnp.stack(new_s, axis=1))
```

```python
import functools
import math

import numpy as np
import jax
import jax.numpy as jnp
from jax import lax
from jax.experimental import pallas as pl
from jax.experimental.pallas import tpu as pltpu

D_MODEL = 2048
BATCH = 32
SEQ = 256
DEPTH = 2
DEC_BATCH = 2
DEC_SEQ = 1024
PAST_LEN = 256
GRID_W = 64
N_HEADS = 16
N_KV_HEADS = 2
HEAD_DIM = 64
WINDOW = 128
ROPE_THETA = 10000.0
N_FOURIER_GROUPS = 4
N_RNN_BLOCKS = 8
CONV_WIDTH = 4
CONV_LEFT = 2
RG_C = 8.0
N_EXPERTS = 32
TOP_K = 4
SWIGLU_LIMIT = 7.0
SWIGLU_ALPHA = 1.702
N_MOD = 6
EPS = 1e-6

LANES = 128
SUBLANES = 8
VMEM_LIMIT_BYTES = 56 * 1024 * 1024

MOE_TILE_ROWS = 1536
MOE_SUB_ROWS = 256
MOE_HIDDEN_BLOCK = 256
MOE_OUT_BLOCK = 256
GATHER_UNROLL = 8
SCAN_ROW_PAD = 8
SCAN_UNROLL = 4
ROW_TILE = 512
PROJ_ROWS = 1024
COMBINE_ROWS = 256
PROJ_OUT_COLS = 512
PROJ_IN_ATTN_COLS = 768
PROJ_IN_RNN_COLS = 1024

F32 = jnp.float32
BF16 = jnp.bfloat16


def _params(semantics):
    return pltpu.CompilerParams(dimension_semantics=semantics, vmem_limit_bytes=VMEM_LIMIT_BYTES)


def _segment_of_tile(i, tile_rows):
    n_ctx_tiles = (BATCH * SEQ) // tile_rows
    tiles_per_latent = DEC_SEQ // tile_rows
    return jnp.where(i < n_ctx_tiles, 0, 1 + (i - n_ctx_tiles) // tiles_per_latent)


def _rms_modulate(x, g, scale, shift):
    ms = jnp.mean(x * x, axis=-1, keepdims=True)
    return (x * lax.rsqrt(ms + EPS) * g) * (1.0 + scale) + shift


def _modulation_kernel(c_ref, w_ref, b_ref, o_ref):
    c = c_ref[...]
    s = (c * jax.nn.sigmoid(c)).astype(BF16)
    o_ref[0] = jnp.dot(s, w_ref[0].astype(BF16), preferred_element_type=F32) + b_ref[0]


def _modulation(cond, w_mod, b_mod):
    d = D_MODEL
    tn = 512
    n_out = N_MOD * d
    return pl.pallas_call(
        _modulation_kernel,
        out_shape=jax.ShapeDtypeStruct((DEPTH, SUBLANES, n_out), F32),
        grid=(DEPTH, n_out // tn),
        in_specs=[
            pl.BlockSpec((SUBLANES, d), lambda l, j: (0, 0)),
            pl.BlockSpec((1, d, tn), lambda l, j: (l, 0, j)),
            pl.BlockSpec((1, 1, tn), lambda l, j: (l, 0, j)),
        ],
        out_specs=pl.BlockSpec((1, SUBLANES, tn), lambda l, j: (l, 0, j)),
        compiler_params=_params(("arbitrary", "arbitrary")),
        name="modulation",
    )(cond, w_mod, b_mod.reshape(DEPTH, 1, n_out))


def _row_part_tiles(row_counts, tm):
    tiles, first = [], 0
    for rows in row_counts:
        tiles.append((first, rows // tm))
        first += rows // tm
    return tuple(tiles)


def _row_part_index(i, part):
    first, n_tiles = part
    return jnp.clip(i - first, 0, n_tiles - 1)


def _for_row_part(i, parts, fn):
    for p, (first, n_tiles) in enumerate(parts):
        @pl.when((i >= first) & (i < first + n_tiles))
        def _():
            fn(p)


def _proj_in_kernel(x_ref, g_ref, sc_ref, sh_ref, w_ref, b_ref, o_ref, h_ref):
    @pl.when(pl.program_id(1) == 0)
    def _():
        h_ref[...] = _rms_modulate(x_ref[...], g_ref[...], sc_ref[0], sh_ref[0]).astype(BF16)

    o_ref[...] = jnp.dot(h_ref[...], w_ref[...].astype(BF16), preferred_element_type=F32) + b_ref[...]


def _proj_in(x, g, scale, shift, w, b, tn):
    n, d = x.shape
    n_out = w.shape[1]
    tm = PROJ_ROWS
    seg = lambda i, j: (_segment_of_tile(i, tm), 0, 0)
    return pl.pallas_call(
        _proj_in_kernel,
        out_shape=jax.ShapeDtypeStruct((n, n_out), F32),
        grid=(n // tm, n_out // tn),
        in_specs=[
            pl.BlockSpec((tm, d), lambda i, j: (i, 0)),
            pl.BlockSpec((1, d), lambda i, j: (0, 0)),
            pl.BlockSpec((1, 1, d), seg),
            pl.BlockSpec((1, 1, d), seg),
            pl.BlockSpec((d, tn), lambda i, j: (0, j)),
            pl.BlockSpec((1, tn), lambda i, j: (0, j)),
        ],
        out_specs=pl.BlockSpec((tm, tn), lambda i, j: (i, j)),
        scratch_shapes=[pltpu.VMEM((tm, d), BF16)],
        compiler_params=_params(("arbitrary", "arbitrary")),
        name="proj_in",
    )(x, g.reshape(1, d), scale, shift, w, b.reshape(1, n_out))


def _proj_out_kernel(*refs, parts):
    a_refs = refs[:len(parts)]
    w_ref, b_ref, gate_ref, res_ref, o_ref = refs[len(parts):]

    def project(p):
        y = jnp.dot(a_refs[p][...], w_ref[...].astype(BF16), preferred_element_type=F32) + b_ref[...]
        o_ref[...] = res_ref[...] + gate_ref[0] * y

    _for_row_part(pl.program_id(0), parts, project)


def _proj_out(a_parts, w, b, gate, res):
    n, d = res.shape
    k = w.shape[0]
    tm, tn = PROJ_ROWS, PROJ_OUT_COLS
    parts = _row_part_tiles([a.shape[0] for a in a_parts], tm)
    return pl.pallas_call(
        functools.partial(_proj_out_kernel, parts=parts),
        out_shape=jax.ShapeDtypeStruct((n, d), F32),
        grid=(n // tm, d // tn),
        in_specs=[pl.BlockSpec((tm, k), lambda i, j, part=part: (_row_part_index(i, part), 0)) for part in parts] + [
            pl.BlockSpec((k, tn), lambda i, j: (0, j)),
            pl.BlockSpec((1, tn), lambda i, j: (0, j)),
            pl.BlockSpec((1, 1, tn), lambda i, j: (_segment_of_tile(i, tm), 0, j)),
            pl.BlockSpec((tm, tn), lambda i, j: (i, j)),
        ],
        out_specs=pl.BlockSpec((tm, tn), lambda i, j: (i, j)),
        compiler_params=_params(("arbitrary", "arbitrary")),
        name="proj_out",
    )(*a_parts, w, b.reshape(1, d), gate, res)


def _dot_nt(a, b):
    return lax.dot_general(a, b, (((1,), (1,)), ((), ())), preferred_element_type=F32)


def _head_pair_operands(k, v, group):
    lane = lax.broadcasted_iota(jnp.int32, k.shape, 1)
    low = lane < HEAD_DIM
    k_sw = pltpu.roll(k, HEAD_DIM, 1)
    v_sw = pltpu.roll(v, HEAD_DIM, 1)
    if group == 0:
        kd = jnp.where(low, k, k_sw)
        vd = jnp.where(low, v, v_sw)
    else:
        kd = jnp.where(low, k_sw, k)
        vd = jnp.where(low, v_sw, v)
    v_lo = jnp.where(low, vd, 0.0).astype(BF16)
    v_hi = jnp.where(low, 0.0, vd).astype(BF16)
    return kd.astype(BF16), v_lo, v_hi


def _split_pair(q2):
    lane = lax.broadcasted_iota(jnp.int32, q2.shape, 1)
    low = lane < HEAD_DIM
    qs = q2 * (HEAD_DIM ** -0.5)
    return jnp.where(low, qs, 0.0).astype(BF16), jnp.where(low, 0.0, qs).astype(BF16)


def _softmax_pv(scores, values, sink):
    m = jnp.full((scores[0].shape[0], 1), sink, F32)
    for s in scores:
        m = jnp.maximum(m, jnp.max(s, axis=-1, keepdims=True))
    den = jnp.exp(sink - m)
    ps = []
    for s in scores:
        p = jnp.exp(s - m)
        den = den + jnp.sum(p, axis=-1, keepdims=True)
        ps.append(p)
    out = None
    for p, v in zip(ps, values):
        o = jnp.dot(p.astype(BF16), v, preferred_element_type=F32)
        out = o if out is None else out + o
    return out / den


def _dft_matrices(t):
    idx = np.arange(t)
    ang = 2.0 * np.pi * ((idx[:, None] * idx[None, :]) % t) / t
    m = np.concatenate([np.cos(ang), np.sin(ang)], axis=0) / math.sqrt(t)
    return jnp.asarray(m, dtype=BF16)


def _dft_channel_matrix(c):
    idx = np.arange(c)
    ang = 2.0 * np.pi * ((idx[:, None] * idx[None, :]) % c) / c
    m = np.concatenate([np.cos(ang), -np.sin(ang)], axis=0) / math.sqrt(c)
    return jnp.asarray(m, dtype=BF16)


def _fourier_group(f_g, ts_ref, cs_ref):
    t = f_g.shape[0]
    ab = jnp.dot(ts_ref[...], f_g.astype(BF16), preferred_element_type=F32)
    lhs = jnp.concatenate([ab[:t], ab[t:]], axis=1).astype(BF16)
    return jnp.dot(lhs, cs_ref[...], preferred_element_type=F32)


def _ctx_mixer_kernel(sink_ref, p_ref, ts_ref, cs_ref, o_ref):
    attn_w = N_HEADS * HEAD_DIM
    kv_w = N_KV_HEADS * HEAD_DIM
    pair_w = 2 * HEAD_DIM
    group_heads = N_HEADS // N_KV_HEADS
    k = p_ref[:, attn_w:attn_w + kv_w]
    v = p_ref[:, attn_w + kv_w:attn_w + 2 * kv_w]
    for g in range(N_KV_HEADS):
        kd, v_lo, v_hi = _head_pair_operands(k, v, g)
        for i in range(group_heads // 2):
            pair = g * (group_heads // 2) + i
            q_lo, q_hi = _split_pair(p_ref[:, pair * pair_w:(pair + 1) * pair_w])
            o = _softmax_pv([_dot_nt(q_lo, kd)], [v_lo], sink_ref[2 * pair])
            o = o + _softmax_pv([_dot_nt(q_hi, kd)], [v_hi], sink_ref[2 * pair + 1])
            o_ref[:, pair * pair_w:(pair + 1) * pair_w] = o.astype(o_ref.dtype)
    f0 = attn_w + 2 * kv_w
    fg = (D_MODEL - attn_w) // N_FOURIER_GROUPS
    for g in range(N_FOURIER_GROUPS):
        z = _fourier_group(p_ref[:, f0 + g * fg:f0 + (g + 1) * fg], ts_ref, cs_ref)
        o_ref[:, attn_w + g * fg:attn_w + (g + 1) * fg] = z.astype(o_ref.dtype)


def _ctx_mixer(p, sink):
    n, width = BATCH * SEQ, p.shape[1]
    fg = (D_MODEL - N_HEADS * HEAD_DIM) // N_FOURIER_GROUPS
    return pl.pallas_call(
        _ctx_mixer_kernel,
        out_shape=jax.ShapeDtypeStruct((n, D_MODEL), BF16),
        grid=(n // SEQ,),
        in_specs=[
            pl.BlockSpec(memory_space=pltpu.SMEM),
            pl.BlockSpec((SEQ, width), lambda b: (b, 0)),
            pl.BlockSpec((2 * SEQ, SEQ), lambda b: (0, 0)),
            pl.BlockSpec((2 * fg, fg), lambda b: (0, 0)),
        ],
        out_specs=pl.BlockSpec((SEQ, D_MODEL), lambda b: (b, 0)),
        compiler_params=_params(("arbitrary",)),
        name="ctx_mixer",
    )(sink, p, _dft_matrices(SEQ), _dft_channel_matrix(fg))


def _rope_tables():
    rows = DEC_SEQ // GRID_W
    row = np.repeat(np.arange(rows, dtype=np.float32), GRID_W)
    col = np.tile(np.arange(GRID_W, dtype=np.float32), rows)
    n_freq = HEAD_DIM // 4
    inv = jnp.asarray(ROPE_THETA, F32) ** (-jnp.arange(n_freq, dtype=F32) / n_freq)
    ang = jnp.concatenate([row[:, None] * inv, col[:, None] * inv], axis=-1)
    cos = jnp.repeat(jnp.cos(ang), 2, axis=-1)
    sin = jnp.repeat(jnp.sin(ang), 2, axis=-1)
    sign = jnp.tile(jnp.asarray([-1.0, 1.0], F32), HEAD_DIM // 2)
    return jnp.tile(cos, (1, 2)), jnp.tile(sin * sign, (1, 2))


def _rope(x, cos, sin_signed):
    lane = lax.broadcasted_iota(jnp.int32, x.shape, 1)
    width = x.shape[1]
    partner = jnp.where(lane % 2 == 0, pltpu.roll(x, width - 1, 1), pltpu.roll(x, 1, 1))
    return x * cos + partner * sin_signed


def _lat_mixer_kernel(sink_ref, p_ref, ck_ref, cv_ref, cos_ref, sin_ref, ts_ref, cs_ref, o_ref, q_scr, k_scr):
    attn_w = N_HEADS * HEAD_DIM
    kv_w = N_KV_HEADS * HEAD_DIM
    pair_w = 2 * HEAD_DIM
    group_heads = N_HEADS // N_KV_HEADS
    q_rows = 256
    cos = cos_ref[...]
    sin = sin_ref[...]
    for pair in range(N_HEADS // 2):
        q_scr[:, pair * pair_w:(pair + 1) * pair_w] = _rope(p_ref[:, pair * pair_w:(pair + 1) * pair_w], cos, sin)
    k_scr[...] = _rope(p_ref[:, attn_w:attn_w + kv_w], cos, sin)
    v = p_ref[:, attn_w + kv_w:attn_w + 2 * kv_w]
    ck = ck_ref[0]
    cv = cv_ref[0]
    k = k_scr[...]
    ops = []
    for g in range(N_KV_HEADS):
        ops.append(_head_pair_operands(k, v, g) + _head_pair_operands(ck, cv, g))

    def chunk(c, carry):
        r0 = pl.multiple_of(c * q_rows, q_rows)
        qi = r0 + lax.broadcasted_iota(jnp.int32, (q_rows, DEC_SEQ), 0)
        kj = lax.broadcasted_iota(jnp.int32, (q_rows, DEC_SEQ), 1)
        valid = jnp.abs(qi - kj) <= WINDOW
        for g in range(N_KV_HEADS):
            kd, v_lo, v_hi, ckd, cv_lo, cv_hi = ops[g]
            for i in range(group_heads // 2):
                pair = g * (group_heads // 2) + i
                q_lo, q_hi = _split_pair(q_scr[pl.ds(r0, q_rows), pair * pair_w:(pair + 1) * pair_w])
                s_lo = jnp.where(valid, _dot_nt(q_lo, kd), -jnp.inf)
                o = _softmax_pv([_dot_nt(q_lo, ckd), s_lo], [cv_lo, v_lo], sink_ref[2 * pair])
                s_hi = jnp.where(valid, _dot_nt(q_hi, kd), -jnp.inf)
                o = o + _softmax_pv([_dot_nt(q_hi, ckd), s_hi], [cv_hi, v_hi], sink_ref[2 * pair + 1])
                o_ref[pl.ds(r0, q_rows), pair * pair_w:(pair + 1) * pair_w] = o.astype(o_ref.dtype)
        return carry

    lax.fori_loop(0, DEC_SEQ // q_rows, chunk, 0)
    f0 = attn_w + 2 * kv_w
    fg = (D_MODEL - attn_w) // N_FOURIER_GROUPS
    for g in range(N_FOURIER_GROUPS):
        z = _fourier_group(p_ref[:, f0 + g * fg:f0 + (g + 1) * fg], ts_ref, cs_ref)
        o_ref[:, attn_w + g * fg:attn_w + (g + 1) * fg] = z.astype(o_ref.dtype)


def _lat_mixer(p, cache_k, cache_v, sink):
    n, width = DEC_BATCH * DEC_SEQ, p.shape[1]
    first = (BATCH * SEQ) // DEC_SEQ
    kv_w = N_KV_HEADS * HEAD_DIM
    attn_w = N_HEADS * HEAD_DIM
    fg = (D_MODEL - attn_w) // N_FOURIER_GROUPS
    cos, sin = _rope_tables()
    return pl.pallas_call(
        _lat_mixer_kernel,
        out_shape=jax.ShapeDtypeStruct((n, D_MODEL), BF16),
        grid=(n // DEC_SEQ,),
        in_specs=[
            pl.BlockSpec(memory_space=pltpu.SMEM),
            pl.BlockSpec((DEC_SEQ, width), lambda b: (first + b, 0)),
            pl.BlockSpec((1, PAST_LEN, kv_w), lambda b: (b, 0, 0)),
            pl.BlockSpec((1, PAST_LEN, kv_w), lambda b: (b, 0, 0)),
            pl.BlockSpec((DEC_SEQ, 2 * HEAD_DIM), lambda b: (0, 0)),
            pl.BlockSpec((DEC_SEQ, 2 * HEAD_DIM), lambda b: (0, 0)),
            pl.BlockSpec((2 * DEC_SEQ, DEC_SEQ), lambda b: (0, 0)),
            pl.BlockSpec((2 * fg, fg), lambda b: (0, 0)),
        ],
        out_specs=pl.BlockSpec((DEC_SEQ, D_MODEL), lambda b: (b, 0)),
        scratch_shapes=[pltpu.VMEM((DEC_SEQ, attn_w), F32), pltpu.VMEM((DEC_SEQ, kv_w), F32)],
        compiler_params=_params(("arbitrary",)),
        name="lat_mixer",
    )(sink, p, cache_k, cache_v, cos, sin, _dft_matrices(DEC_SEQ), _dft_channel_matrix(fg))


def _rnn_core_kernel(xr_ref, gr_ref, cw_ref, cb_ref, wa_ref, ba_ref, wx_ref, bx_ref, lam_ref, h0_ref,
                     y_ref, st_ref, a_scr, b_scr, h_scr, *, n_seq, seq_len):
    rows = n_seq * seq_len
    n_lb = xr_ref.shape[1] // LANES
    xr = xr_ref[...]
    t_idx = lax.broadcasted_iota(jnp.int32, (rows, 1), 0) % seq_len
    xc = jnp.broadcast_to(cb_ref[...], xr.shape)
    for tap in range(CONV_WIDTH):
        off = tap - CONV_LEFT
        shifted = xr if off == 0 else pltpu.roll(xr, (-off) % rows, 0)
        valid = (t_idx + off >= 0) & (t_idx + off < seq_len)
        xc = xc + jnp.where(valid, shifted, 0.0) * cw_ref[tap:tap + 1, :]
    xcb = xc.astype(BF16)
    pitch = seq_len + SCAN_ROW_PAD

    def sigmoid(z):
        return 0.5 * jnp.tanh(0.5 * z) + 0.5

    def seq_rows(s):
        return pl.ds(s * pitch, seq_len)

    for d in range(2):
        r = sigmoid(jnp.dot(xcb, wa_ref[d, 0].astype(BF16), preferred_element_type=F32) + ba_ref[d:d + 1, :])
        gi = sigmoid(jnp.dot(xcb, wx_ref[d, 0].astype(BF16), preferred_element_type=F32) + bx_ref[d:d + 1, :])
        neg_lam = -lam_ref[d:d + 1, :]
        softplus = jnp.maximum(neg_lam, 0.0) + jnp.log1p(jnp.exp(-jnp.abs(neg_lam)))
        a = jnp.exp(-RG_C * r * softplus)
        b = jnp.sqrt(1.0 - a * a) * (gi * xc)
        for lb in range(n_lb):
            for s in range(n_seq):
                a_scr[d * n_lb + lb, seq_rows(s), :] = a[s * seq_len:(s + 1) * seq_len, lb * LANES:(lb + 1) * LANES]
                b_scr[d * n_lb + lb, seq_rows(s), :] = b[s * seq_len:(s + 1) * seq_len, lb * LANES:(lb + 1) * LANES]

    def time_rows(t):
        return pl.ds(t, n_seq, stride=pitch)

    def lane_block(ref, k, lb):
        return ref[:, k, lb * LANES:(lb + 1) * LANES]

    def fwd(t, hs):
        out = []
        for lb in range(n_lb):
            h = a_scr[lb, time_rows(t), :] * hs[lb] + b_scr[lb, time_rows(t), :]
            h_scr[lb, time_rows(t), :] = h
            out.append(h)
        return tuple(out)

    hs = lax.fori_loop(0, seq_len, fwd, tuple(lane_block(h0_ref, 0, lb) for lb in range(n_lb)), unroll=SCAN_UNROLL)
    for lb in range(n_lb):
        st_ref[:, 0, lb * LANES:(lb + 1) * LANES] = hs[lb]

    def bwd(i, hs):
        t = seq_len - 1 - i
        out = []
        for lb in range(n_lb):
            h = a_scr[n_lb + lb, time_rows(t), :] * hs[lb] + b_scr[n_lb + lb, time_rows(t), :]
            h_scr[lb, time_rows(t), :] = h_scr[lb, time_rows(t), :] + h
            out.append(h)
        return tuple(out)

    hs = lax.fori_loop(0, seq_len, bwd, tuple(lane_block(h0_ref, 1, lb) for lb in range(n_lb)), unroll=SCAN_UNROLL)
    for lb in range(n_lb):
        st_ref[:, 1, lb * LANES:(lb + 1) * LANES] = hs[lb]
    for lb in range(n_lb):
        for s in range(n_seq):
            gr = gr_ref[s * seq_len:(s + 1) * seq_len, lb * LANES:(lb + 1) * LANES]
            gelu = 0.5 * gr * (1.0 + jnp.tanh(math.sqrt(2.0 / math.pi) * (gr + 0.044715 * (gr * gr * gr))))
            y_ref[s * seq_len:(s + 1) * seq_len, lb * LANES:(lb + 1) * LANES] = (
                gelu * h_scr[lb, seq_rows(s), :]).astype(y_ref.dtype)


def _rnn_core(p, first_row, n, h0, conv_w, conv_b, w_a, b_a, w_x, b_x, lam, n_seq, seq_len):
    d_rnn = p.shape[1] // 2
    cb = d_rnn // N_RNN_BLOCKS
    rows = n_seq * seq_len
    n_batch = n // seq_len
    first = first_row // rows
    scan_rows = n_seq * (seq_len + SCAN_ROW_PAD)
    kern = functools.partial(_rnn_core_kernel, n_seq=n_seq, seq_len=seq_len)
    return pl.pallas_call(
        kern,
        out_shape=(jax.ShapeDtypeStruct((n, d_rnn), BF16), jax.ShapeDtypeStruct((n_batch, 2, d_rnn), F32)),
        grid=(n // rows, N_RNN_BLOCKS),
        in_specs=[
            pl.BlockSpec((rows, cb), lambda i, c: (first + i, c)),
            pl.BlockSpec((rows, cb), lambda i, c: (first + i, N_RNN_BLOCKS + c)),
            pl.BlockSpec((CONV_WIDTH, cb), lambda i, c: (0, c)),
            pl.BlockSpec((1, cb), lambda i, c: (0, c)),
            pl.BlockSpec((2, 1, cb, cb), lambda i, c: (0, c, 0, 0)),
            pl.BlockSpec((2, cb), lambda i, c: (0, c)),
            pl.BlockSpec((2, 1, cb, cb), lambda i, c: (0, c, 0, 0)),
            pl.BlockSpec((2, cb), lambda i, c: (0, c)),
            pl.BlockSpec((2, cb), lambda i, c: (0, c)),
            pl.BlockSpec((n_seq, 2, cb), lambda i, c: (i, 0, c)),
        ],
        out_specs=(
            pl.BlockSpec((rows, cb), lambda i, c: (i, c)),
            pl.BlockSpec((n_seq, 2, cb), lambda i, c: (i, 0, c)),
        ),
        scratch_shapes=[pltpu.VMEM((2 * cb // LANES, scan_rows, LANES), F32),
                        pltpu.VMEM((2 * cb // LANES, scan_rows, LANES), F32),
                        pltpu.VMEM((cb // LANES, scan_rows, LANES), F32)],
        compiler_params=_params(("arbitrary", "arbitrary")),
        name="rnn_core",
    )(p, p, conv_w, conv_b.reshape(1, d_rnn), w_a, b_a, w_x, b_x, lam, h0)


def _pack_bf16_pairs(h):
    half = h.shape[1] // 2
    bits = lax.bitcast_convert_type(h.astype(BF16).astype(F32), jnp.uint32)
    return (bits[:, :half] >> 16) | (bits[:, half:] & jnp.uint32(0xFFFF0000))


def _unpack_bf16_pairs(w):
    lo = lax.bitcast_convert_type(w << 16, F32)
    hi = lax.bitcast_convert_type(w & jnp.uint32(0xFFFF0000), F32)
    return lo.astype(BF16), hi.astype(BF16)


def _router_kernel(x_ref, g_ref, sc_ref, sh_ref, w_ref, b_ref, h_ref, idx_ref, gate_ref, rank_ref, cnt_ref, run_ref):
    tm = x_ref.shape[0]

    @pl.when(pl.program_id(0) == 0)
    def _():
        run_ref[...] = jnp.zeros_like(run_ref)

    h = _rms_modulate(x_ref[...], g_ref[...], sc_ref[0], sh_ref[0])
    packed = _pack_bf16_pairs(h)
    lane_rows = packed.shape[1] // LANES
    for i in range(lane_rows):
        h_ref[pl.ds(i, tm, stride=lane_rows), :] = packed[:, i * LANES:(i + 1) * LANES]
    w = w_ref[...]
    h_hi, w_hi = h.astype(BF16), w.astype(BF16)
    h_lo = (h - h_hi.astype(F32)).astype(BF16)
    w_lo = (w - w_hi.astype(F32)).astype(BF16)
    logits = (jnp.dot(h_hi, w_hi, preferred_element_type=F32)
              + (jnp.dot(h_lo, w_hi, preferred_element_type=F32) + jnp.dot(h_hi, w_lo, preferred_element_type=F32))
              + b_ref[...])
    lane = lax.broadcasted_iota(jnp.int32, logits.shape, 1).astype(F32)
    col = lax.broadcasted_iota(jnp.int32, (tm, TOP_K), 1)
    chosen = jnp.zeros(logits.shape, F32)
    top_v, top_i, hits = [], [], []
    work = logits
    for _ in range(TOP_K):
        m = jnp.max(work, axis=-1, keepdims=True)
        first = jnp.min(jnp.where(work == m, lane, float(N_EXPERTS)), axis=-1, keepdims=True)
        hit = lane == first
        work = jnp.where(hit, -jnp.inf, work)
        chosen = jnp.where(hit, 1.0, chosen)
        top_v.append(m)
        top_i.append(first)
        hits.append(hit)
    exps = [jnp.exp(v - top_v[0]) for v in top_v]
    den = exps[0]
    for e in exps[1:]:
        den = den + e
    ri = lax.broadcasted_iota(jnp.int32, (tm, tm), 0)
    ci = lax.broadcasted_iota(jnp.int32, (tm, tm), 1)
    before = (ci < ri).astype(BF16)
    rank_all = jnp.dot(before, chosen.astype(BF16), preferred_element_type=F32) + run_ref[...]
    idx_out = jnp.zeros((tm, TOP_K), F32)
    gate_out = jnp.zeros((tm, TOP_K), F32)
    rank_out = jnp.zeros((tm, TOP_K), F32)
    for k in range(TOP_K):
        rk = jnp.sum(jnp.where(hits[k], rank_all, 0.0), axis=-1, keepdims=True)
        idx_out = jnp.where(col == k, top_i[k], idx_out)
        gate_out = jnp.where(col == k, exps[k] / den, gate_out)
        rank_out = jnp.where(col == k, rk, rank_out)
    idx_ref[...] = idx_out.astype(jnp.int32)
    gate_ref[...] = gate_out
    rank_ref[...] = rank_out.astype(jnp.int32)
    run_ref[...] = run_ref[...] + jnp.sum(chosen, axis=0, keepdims=True)
    cnt_ref[...] = run_ref[...]


def _router(x, g, scale, shift, w, b):
    n, d = x.shape
    e = w.shape[1]
    tm = ROW_TILE
    seg = lambda i: (_segment_of_tile(i, tm), 0, 0)
    small = pl.BlockSpec((tm, TOP_K), lambda i: (i, 0))
    return pl.pallas_call(
        _router_kernel,
        out_shape=(
            jax.ShapeDtypeStruct((n * (d // 2 // LANES), LANES), jnp.uint32),
            jax.ShapeDtypeStruct((n, TOP_K), jnp.int32),
            jax.ShapeDtypeStruct((n, TOP_K), F32),
            jax.ShapeDtypeStruct((n, TOP_K), jnp.int32),
            jax.ShapeDtypeStruct((1, e), F32),
        ),
        grid=(n // tm,),
        in_specs=[
            pl.BlockSpec((tm, d), lambda i: (i, 0)),
            pl.BlockSpec((1, d), lambda i: (0, 0)),
            pl.BlockSpec((1, 1, d), seg),
            pl.BlockSpec((1, 1, d), seg),
            pl.BlockSpec((d, e), lambda i: (0, 0)),
            pl.BlockSpec((1, e), lambda i: (0, 0)),
        ],
        out_specs=(pl.BlockSpec((tm * (d // 2 // LANES), LANES), lambda i: (i, 0)), small, small, small,
                   pl.BlockSpec((1, e), lambda i: (0, 0))),
        scratch_shapes=[pltpu.VMEM((1, e), F32)],
        compiler_params=_params(("arbitrary",)),
        name="router",
    )(x, g.reshape(1, d), scale, shift, w, b.reshape(1, e))


def _moe_kernel(te_ref, tv_ref, pos_ref, h_hbm, wg_ref, wu_ref, wd_ref, bg_ref, bu_ref, bd_ref, o_ref,
                src_ref, gather_buf, gather_sem, xs_scr, act_scr, wg_scr, wu_scr, wd_scr):
    t = pl.program_id(0)
    j = pl.program_id(1)
    n_tiles = pl.num_programs(0) - 1
    n_j = act_scr.shape[1]
    tile_rows = xs_scr.shape[0]
    n_sub_total = tile_rows // MOE_SUB_ROWS
    out_rows = o_ref.shape[0] // tile_rows
    ta = jnp.minimum(t, n_tiles - 1)
    tb = jnp.maximum(t - 1, 0)

    def sub_blocks(tile):
        return (tv_ref[tile] + MOE_SUB_ROWS - 1) // MOE_SUB_ROWS

    def sub_rows(r):
        return pl.ds(pl.multiple_of(r * MOE_SUB_ROWS, MOE_SUB_ROWS), MOE_SUB_ROWS)

    lane_rows = xs_scr.shape[1] // 2 // LANES
    half = xs_scr.shape[1] // 2

    def token_rows(first_token, count=1):
        return pl.ds(pl.multiple_of(first_token * lane_rows, lane_rows), count * lane_rows)

    def start_gather(tile):
        def issue(i, carry):
            for q in range(GATHER_UNROLL):
                r = i * GATHER_UNROLL + q
                token = src_ref[tile * tile_rows + r]
                pltpu.make_async_copy(h_hbm.at[token_rows(token), :], gather_buf.at[token_rows(r), :], gather_sem).start()
            return carry

        lax.fori_loop(0, sub_blocks(tile) * (MOE_SUB_ROWS // GATHER_UNROLL), issue, 0)

    def wait_gather(tile):
        def wait_sub(r, carry):
            block = token_rows(r * MOE_SUB_ROWS, MOE_SUB_ROWS)
            pltpu.make_async_copy(h_hbm.at[block, :], gather_buf.at[block, :], gather_sem).wait()
            return carry

        lax.fori_loop(0, sub_blocks(tile), wait_sub, 0)

    @pl.when((t == 0) & (j == 0))
    def _():
        tokens_per_trip = GATHER_UNROLL // TOP_K

        def put(i, carry):
            for q in range(tokens_per_trip):
                token = i * tokens_per_trip + q
                for k in range(TOP_K):
                    src_ref[pos_ref[token * TOP_K + k]] = token
            return carry

        lax.fori_loop(0, pos_ref.shape[0] // GATHER_UNROLL, put, 0)

        def pad_tile(tile, carry):
            def pad(r, inner):
                src_ref[tile * tile_rows + r] = 0
                return inner

            lax.fori_loop(tv_ref[tile], sub_blocks(tile) * MOE_SUB_ROWS, pad, 0)
            return carry

        lax.fori_loop(0, n_tiles, pad_tile, 0)

    run_a = (t < n_tiles) & (tv_ref[ta] > 0)
    run_b = (t >= 1) & (tv_ref[tb] > 0)
    n_a = jnp.where(run_a, sub_blocks(ta), 0)
    n_b = jnp.where(run_b, sub_blocks(tb), 0)
    slot_a = ta % 2
    slot_b = tb % 2

    @pl.when((j == 0) & run_a)
    def _():
        @pl.when(t == 0)
        def _():
            start_gather(0)

        wait_gather(t)

        def unpack(r, carry):
            for i in range(lane_rows):
                first = pl.multiple_of(r * MOE_SUB_ROWS * lane_rows, lane_rows) + i
                lo, hi = _unpack_bf16_pairs(gather_buf[pl.ds(first, MOE_SUB_ROWS, stride=lane_rows), :])
                xs_scr[sub_rows(r), i * LANES:(i + 1) * LANES] = lo
                xs_scr[sub_rows(r), half + i * LANES:half + (i + 1) * LANES] = hi
            return carry

        lax.fori_loop(0, n_a, unpack, 0)

        @pl.when(t + 1 < n_tiles)
        def _():
            start_gather(t + 1)

    @pl.when(run_a)
    def _():
        wg_scr[...] = wg_ref[0, 0].astype(BF16)
        wu_scr[...] = wu_ref[0, 0].astype(BF16)

    @pl.when(run_b)
    def _():
        wd_scr[...] = wd_ref[0, 0].astype(BF16)

    def span_rows(r, span):
        return pl.ds(pl.multiple_of(r * MOE_SUB_ROWS, MOE_SUB_ROWS), span * MOE_SUB_ROWS)

    def gate_up(r, span):
        rows = span_rows(r, span)
        x = xs_scr[rows, :]
        g = jnp.dot(x, wg_scr[...], preferred_element_type=F32) + bg_ref[0, j]
        u = jnp.dot(x, wu_scr[...], preferred_element_type=F32) + bu_ref[0, j]
        g = jnp.minimum(g, SWIGLU_LIMIT)
        u = jnp.clip(u, -SWIGLU_LIMIT, SWIGLU_LIMIT)
        act = (u + 1.0) * (g * jax.nn.sigmoid(SWIGLU_ALPHA * g))
        act_scr[slot_a, j, rows, :] = act.astype(BF16)

    def down(r, span):
        rows = span_rows(r, span)
        act = jnp.concatenate([act_scr[slot_b, jb, rows, :] for jb in range(n_j)], axis=1)
        y = jnp.dot(act, wd_scr[...], preferred_element_type=F32) + bd_ref[0, j]
        first = pl.multiple_of(r * MOE_SUB_ROWS * out_rows, out_rows) + j
        o_ref[pl.ds(first, span * MOE_SUB_ROWS, stride=out_rows), :] = _pack_bf16_pairs(y)

    def for_each_sub_block(n_sub, one, group, fused):
        def run(first, count):
            if fused:
                one(first, count)
            else:
                for q in range(count):
                    one(first + q, 1)

        def trip(i, carry):
            run(group * i, group)
            return carry

        lax.fori_loop(0, n_sub // group, trip, 0)
        rest = n_sub % group
        for count in range(1, group):
            @pl.when(rest == count)
            def _():
                run(n_sub - count, count)

    for_each_sub_block(n_a, gate_up, group=3, fused=False)
    for_each_sub_block(n_b, down, group=3, fused=True)

    @pl.when((t >= 1) & (j == 0))
    def _():
        def clear(r, carry):
            block = pl.ds(pl.multiple_of(r * MOE_SUB_ROWS * out_rows, out_rows), MOE_SUB_ROWS * out_rows)
            o_ref[block, :] = jnp.zeros((MOE_SUB_ROWS * out_rows, o_ref.shape[1]), o_ref.dtype)
            return carry

        lax.fori_loop(n_b, n_sub_total, clear, 0)


def _moe_experts(h_packed, pos, p_rows, tile_expert, tile_valid, layer, w_gate, b_gate, w_up, b_up, w_down, b_down):
    _, e, d, hdim = w_gate.shape
    tm, th, tn = MOE_TILE_ROWS, MOE_HIDDEN_BLOCK, MOE_OUT_BLOCK
    n_tiles = p_rows // tm
    n_j = hdim // th
    n_out = d // tn

    assert n_j == n_out, "gate/up and down halves share the block axis"
    assert tn == 2 * LANES, "one output column block packs into one 128-lane row of bf16 pairs"

    def up_tile(t):
        return jnp.minimum(t, n_tiles - 1)

    def down_tile(t):
        return jnp.maximum(t - 1, 0)

    def up_block(t, j, tv):
        return jnp.where((t < n_tiles) & (tv[up_tile(t)] > 0), j, n_j - 1)

    def down_block(t, j, tv):
        return jnp.where((t >= 1) & (tv[down_tile(t)] > 0), j, n_out - 1)

    grid_spec = pltpu.PrefetchScalarGridSpec(
        num_scalar_prefetch=3,
        grid=(n_tiles + 1, n_j),
        in_specs=[
            pl.BlockSpec(memory_space=pl.ANY),
            pl.BlockSpec((1, 1, d, th), lambda t, j, te, tv, src: (layer, te[up_tile(t)], 0, up_block(t, j, tv))),
            pl.BlockSpec((1, 1, d, th), lambda t, j, te, tv, src: (layer, te[up_tile(t)], 0, up_block(t, j, tv))),
            pl.BlockSpec((1, 1, hdim, tn), lambda t, j, te, tv, src: (layer, te[down_tile(t)], 0, down_block(t, j, tv))),
            pl.BlockSpec((1, n_j, 1, th), lambda t, j, te, tv, src: (te[up_tile(t)], 0, 0, 0)),
            pl.BlockSpec((1, n_j, 1, th), lambda t, j, te, tv, src: (te[up_tile(t)], 0, 0, 0)),
            pl.BlockSpec((1, n_out, 1, tn), lambda t, j, te, tv, src: (te[down_tile(t)], 0, 0, 0)),
        ],
        out_specs=pl.BlockSpec((tm * n_out, LANES), lambda t, j, te, tv, src: (down_tile(t), 0)),
        scratch_shapes=[
            pltpu.SMEM((p_rows,), jnp.int32),
            pltpu.VMEM((tm * (d // 2 // LANES), LANES), jnp.uint32),
            pltpu.SemaphoreType.DMA(()),
            pltpu.VMEM((tm, d), BF16),
            pltpu.VMEM((2, n_j, tm, th), BF16),
            pltpu.VMEM((d, th), BF16),
            pltpu.VMEM((d, th), BF16),
            pltpu.VMEM((hdim, tn), BF16),
        ],
    )
    return pl.pallas_call(
        _moe_kernel,
        out_shape=jax.ShapeDtypeStruct((p_rows * n_out, LANES), jnp.uint32),
        grid_spec=grid_spec,
        compiler_params=_params(("arbitrary", "arbitrary")),
        name="moe_experts",
    )(tile_expert, tile_valid, pos, h_packed, w_gate, w_up, w_down,
      b_gate[layer].reshape(e, n_j, 1, th), b_up[layer].reshape(e, n_j, 1, th), b_down[layer].reshape(e, n_out, 1, tn))


def _combine_kernel(pos_ref, ys_hbm, gates_ref, x_ref, gate2_ref, norm_ref, *rest, final_norm, out_parts):
    o_refs = rest[:len(out_parts)]
    rows_buf, rows_sem = rest[len(out_parts):]
    i = pl.program_id(0)
    tm = x_ref.shape[0]
    slot = i % 2
    lane_rows = rows_buf.shape[1] // (TOP_K * tm)

    def token_rows(first, count=1):
        return pl.ds(pl.multiple_of(first * lane_rows, lane_rows), count * lane_rows)

    def start_gather(tile, slot):
        def issue(i, carry):
            for q in range(GATHER_UNROLL // TOP_K):
                r = i * (GATHER_UNROLL // TOP_K) + q
                for k in range(TOP_K):
                    row = pos_ref[(tile * tm + r) * TOP_K + k]
                    pltpu.make_async_copy(ys_hbm.at[token_rows(row), :], rows_buf.at[slot, token_rows(k * tm + r), :],
                                          rows_sem.at[slot]).start()
            return carry

        lax.fori_loop(0, tm // (GATHER_UNROLL // TOP_K), issue, 0)

    @pl.when(i == 0)
    def _():
        start_gather(0, 0)

    @pl.when(i + 1 < pl.num_programs(0))
    def _():
        start_gather(i + 1, 1 - slot)

    pltpu.make_async_copy(ys_hbm.at[pl.ds(0, TOP_K * tm * lane_rows), :], rows_buf.at[slot], rows_sem.at[slot]).wait()
    gates = gates_ref[...]

    def finish(p):
        o_ref = o_refs[p]
        for jr in range(lane_rows):
            lo = hi = None
            for k in range(TOP_K):
                w = rows_buf[slot, pl.ds(k * tm * lane_rows + jr, tm, stride=lane_rows), :]
                g = gates[:, k:k + 1]
                lo_k = g * lax.bitcast_convert_type(w << 16, F32)
                hi_k = g * lax.bitcast_convert_type(w & jnp.uint32(0xFFFF0000), F32)
                lo = lo_k if lo is None else lo + lo_k
                hi = hi_k if hi is None else hi + hi_k
            for part, cols in ((lo, pl.ds(2 * jr * LANES, LANES)), (hi, pl.ds((2 * jr + 1) * LANES, LANES))):
                o_ref[:, cols] = x_ref[:, cols] + gate2_ref[0, :, cols] * part
        if final_norm:
            x = o_ref[...]
            ms = jnp.mean(x * x, axis=-1, keepdims=True)
            o_ref[...] = x * lax.rsqrt(ms + EPS) * norm_ref[...]

    _for_row_part(i, out_parts, finish)


def _combine(ys, pos, gates, x, gate2, norm_g, final_norm, out_rows):
    n, d = x.shape
    tm = COMBINE_ROWS
    lane_rows = d // (2 * LANES)
    out_parts = _row_part_tiles(out_rows, tm)
    out_specs = [pl.BlockSpec((tm, d), lambda i, pos, part=part: (_row_part_index(i, part), 0)) for part in out_parts]
    grid_spec = pltpu.PrefetchScalarGridSpec(
        num_scalar_prefetch=1,
        grid=(n // tm,),
        in_specs=[
            pl.BlockSpec(memory_space=pl.ANY),
            pl.BlockSpec((tm, TOP_K), lambda i, pos: (i, 0)),
            pl.BlockSpec((tm, d), lambda i, pos: (i, 0)),
            pl.BlockSpec((1, 1, d), lambda i, pos: (_segment_of_tile(i, tm), 0, 0)),
            pl.BlockSpec((1, d), lambda i, pos: (0, 0)),
        ],
        out_specs=out_specs,
        scratch_shapes=[pltpu.VMEM((2, TOP_K * tm * lane_rows, LANES), jnp.uint32), pltpu.SemaphoreType.DMA((2,))],
    )
    return pl.pallas_call(
        functools.partial(_combine_kernel, final_norm=final_norm, out_parts=out_parts),
        out_shape=[jax.ShapeDtypeStruct((rows, d), F32) for rows in out_rows],
        grid_spec=grid_spec,
        compiler_params=_params(("arbitrary",)),
        name="moe_combine",
    )(pos.reshape(-1), ys, gates, x, gate2, norm_g.reshape(1, d))


def _moe_layer(x, g, scale, shift, gate2, layer, w_router, b_router, w_gate, b_gate, w_up, b_up, w_down, b_down,
               norm_g, final_norm, out_rows):
    n, d = x.shape
    e = w_router.shape[-1]
    tm = MOE_TILE_ROWS
    h, top_i, gates, rank, counts = _router(x, g, scale, shift, w_router[layer], b_router[layer])
    counts = counts[0].astype(jnp.int32)
    padded = ((counts + tm - 1) // tm) * tm
    ends = jnp.cumsum(padded)
    starts = ends - padded
    n_tiles = -(-(n * TOP_K) // tm) + e
    p_rows = n_tiles * tm
    tile_start = jnp.arange(n_tiles, dtype=jnp.int32) * tm
    n_used = ends[-1] // tm
    tile_expert = jnp.minimum(jnp.searchsorted(ends, tile_start, side="right"), e - 1).astype(jnp.int32)
    tile_valid = jnp.clip(counts[tile_expert] - (tile_start - starts[tile_expert]), 0, tm)
    tile_valid = jnp.where(tile_start < ends[-1], tile_valid, 0).astype(jnp.int32)
    last = jnp.maximum(n_used - 1, 0)
    tile_expert = jnp.where(tile_start < ends[-1], tile_expert, tile_expert[last]).astype(jnp.int32)
    pos = starts[top_i] + rank
    ys = _moe_experts(h, pos.reshape(-1), p_rows, tile_expert, tile_valid, layer, w_gate, b_gate, w_up, b_up, w_down, b_down)
    return _combine(ys, pos, gates, x, gate2, norm_g, final_norm, out_rows)


def kernel(x_prompt, x_sample, cache_k, cache_v, state_rglru, c, c_ctx, norm_mix, norm_ffn, w_mod, b_mod,
           attn_w_in, attn_b_in, attn_w_out, attn_b_out, attn_sink,
           rnn_w_in, rnn_b_in, rnn_conv_w, rnn_conv_b, rnn_w_a, rnn_b_a, rnn_w_x, rnn_b_x,
           rnn_lambda, rnn_w_out, rnn_b_out,
           moe_w_router, moe_b_router, moe_w_gate, moe_b_gate, moe_w_up, moe_b_up,
           moe_w_down, moe_b_down, final_norm):
    d = D_MODEL
    n_ctx = BATCH * SEQ
    attn_w = N_HEADS * HEAD_DIM
    kv_w = N_KV_HEADS * HEAD_DIM
    n_lat = DEC_BATCH * DEC_SEQ
    x = jnp.concatenate([x_prompt.reshape(n_ctx, d), x_sample.reshape(n_lat, d)], axis=0)
    cond = jnp.concatenate([c_ctx[None, :], c, jnp.zeros((SUBLANES - 1 - DEC_BATCH, d), F32)], axis=0)
    mods = _modulation(cond, w_mod, b_mod)

    def mod(l, k):
        return mods[l, :, k * d:(k + 1) * d].reshape(SUBLANES, 1, d)

    new_k, new_v, new_s = [], [], []
    for l in range(DEPTH):
        j = l // 2
        sh1, sc1, g1, sh2, sc2, g2 = [mod(l, k) for k in range(N_MOD)]
        if l % 2 == 0:
            p = _proj_in(x, norm_mix[l], sc1, sh1, attn_w_in[j], attn_b_in[j], tn=PROJ_IN_ATTN_COLS)
            new_k.append(p[:n_ctx, attn_w:attn_w + kv_w].reshape(BATCH, SEQ, N_KV_HEADS, HEAD_DIM))
            new_v.append(p[:n_ctx, attn_w + kv_w:attn_w + 2 * kv_w].reshape(BATCH, SEQ, N_KV_HEADS, HEAD_DIM))
            mix_ctx = _ctx_mixer(p, attn_sink[j])
            mix_lat = _lat_mixer(p, cache_k[:, j].reshape(DEC_BATCH, PAST_LEN, kv_w),
                                 cache_v[:, j].reshape(DEC_BATCH, PAST_LEN, kv_w), attn_sink[j])
            x = _proj_out((mix_ctx, mix_lat), attn_w_out[j], attn_b_out[j], g1, x)
        else:
            p = _proj_in(x, norm_mix[l], sc1, sh1, rnn_w_in[j], rnn_b_in[j], tn=PROJ_IN_RNN_COLS)
            args = (rnn_conv_w[j], rnn_conv_b[j], rnn_w_a[j], rnn_b_a[j], rnn_w_x[j], rnn_b_x[j], rnn_lambda[j])
            h0_ctx = jnp.zeros((BATCH, 2, d), F32)
            mix_ctx, st = _rnn_core(p, 0, n_ctx, h0_ctx, *args, n_seq=SUBLANES, seq_len=SEQ)
            mix_lat, _ = _rnn_core(p, n_ctx, DEC_BATCH * DEC_SEQ, state_rglru[:, j], *args,
                                   n_seq=DEC_BATCH, seq_len=DEC_SEQ)
            new_s.append(st)
            x = _proj_out((mix_ctx, mix_lat), rnn_w_out[j], rnn_b_out[j], g1, x)
        last = l == DEPTH - 1
        outs = _moe_layer(x, norm_ffn[l], sc2, sh2, g2, l, moe_w_router, moe_b_router,
                          moe_w_gate, moe_b_gate, moe_w_up, moe_b_up, moe_w_down, moe_b_down,
                          final_norm, final_norm=last, out_rows=(n_ctx, n_lat) if last else (n_ctx + n_lat,))
        x = outs[0]
    y_prompt = outs[0].reshape(BATCH, SEQ, d)
    y_sample = outs[1].reshape(DEC_BATCH, DEC_SEQ, d)
    return (y_prompt, y_sample, jnp.stack(new_k, axis=1), jnp.stack(new_v, axis=1), jnp.stack(new_s, axis=1))
```
